```python
import jax
import jax.numpy as jnp
from jax import lax
import numpy as np

D_MODEL = 1024
BATCH = 4
SEQ = 4096
DEPTH = 2

GRID_W = 64
CTX_LEN = 256
HEAD_DIM = 64
NORM_EPS = 1e-6
ROPE_THETA = 10000.0
SW_HEADS = D_MODEL // (2 * HEAD_DIM)
SW_KV_HEADS = SW_HEADS // 4
SW_GROUP = SW_HEADS // SW_KV_HEADS
SW_WINDOW = 128
SW_BLOCK = 128
SC_DIM = D_MODEL // 2
SC_WIDTH = 3
NA_HEADS = D_MODEL // (2 * HEAD_DIM)
NA_WIN_ROWS = 8
NA_WIN_COLS = 16
FOURIER_DIM = D_MODEL // 2
FOURIER_GROUPS = 4
SW_Q_DIM = SW_HEADS * HEAD_DIM
SW_KV_DIM = SW_KV_HEADS * HEAD_DIM
EVEN_IN = SW_Q_DIM + 2 * SW_KV_DIM + 3 * SC_DIM
EVEN_OUT = SW_Q_DIM + SC_DIM
NA_DIM = NA_HEADS * HEAD_DIM
ODD_IN = 3 * NA_DIM + FOURIER_DIM
ODD_OUT = NA_DIM + FOURIER_DIM
N_EXPERTS = 32
TOP_K = 4
D_FF = D_MODEL
SWIGLU_LIMIT = 7.0
SWIGLU_ALPHA = 1.702
MOE_BLOCK = 128

kernel_name = 'hybrid_conv_swa_natten_fnet_moe_dit'


def rms_norm(x, g):
    x32 = x.astype(jnp.float32)
    y = x32 * lax.rsqrt(jnp.mean(x32 * x32, axis=-1, keepdims=True) + NORM_EPS)
    return (y * g.astype(jnp.float32)).astype(x.dtype)


def axial_rope(n_tok):
    t = jnp.arange(n_tok, dtype=jnp.int32)
    row = (t // GRID_W).astype(jnp.float32)
    col = (t % GRID_W).astype(jnp.float32)
    n_freq = HEAD_DIM // 4
    inv_freq = jnp.power(ROPE_THETA, -jnp.arange(n_freq, dtype=jnp.float32) / n_freq)
    ang = jnp.concatenate([row[:, None] * inv_freq, col[:, None] * inv_freq], axis=-1)
    return jnp.cos(ang), jnp.sin(ang)


def apply_rope(x, cos, sin):
    shape = (x.shape[1],) + (1,) * (x.ndim - 3) + (x.shape[-1] // 2,)
    cos = cos.reshape(shape).astype(x.dtype)
    sin = sin.reshape(shape).astype(x.dtype)
    x1, x2 = jnp.split(x, 2, axis=-1)
    return jnp.concatenate([x1 * cos - x2 * sin, x2 * cos + x1 * sin], axis=-1)


def context_attention(q, k, v, sink):
    b, n_ctx, hk, g, _ = q.shape
    s = jnp.einsum('bqhgd,bkhd->bhgqk', q, k).astype(jnp.float32) * (HEAD_DIM ** -0.5)
    if sink is not None:
        s_sink = jnp.broadcast_to(sink.astype(jnp.float32).reshape(1, hk, g, 1, 1), s.shape[:-1] + (1,))
        s = jnp.concatenate([s, s_sink], axis=-1)
    p = jax.nn.softmax(s, axis=-1).astype(v.dtype)
    o = jnp.einsum('bhgqk,bkhd->bqhgd', p[..., :n_ctx], v)
    return o.reshape(b, n_ctx, hk * g * HEAD_DIM)


def window_attention(q, k, v, k_ctx, v_ctx, sink):
    b, s = q.shape[0], q.shape[1]
    nb = s // SW_BLOCK
    nk = 3 * SW_BLOCK
    scale = HEAD_DIM ** -0.5
    qb = q.reshape(b, nb, SW_BLOCK, SW_KV_HEADS, SW_GROUP, HEAD_DIM)
    pad = ((0, 0), (SW_BLOCK, SW_BLOCK), (0, 0), (0, 0))

    def band(t):
        tp = jnp.pad(t, pad).reshape(b, nb + 2, SW_BLOCK, SW_KV_HEADS, HEAD_DIM)
        return jnp.concatenate([tp[:, :-2], tp[:, 1:-1], tp[:, 2:]], axis=2)

    kb, vb = band(k), band(v)
    q_pos = jnp.arange(nb)[:, None] * SW_BLOCK + jnp.arange(SW_BLOCK)[None, :]
    k_pos = jnp.arange(nb)[:, None] * SW_BLOCK - SW_BLOCK + jnp.arange(nk)[None, :]
    valid = ((jnp.abs(q_pos[:, :, None] - k_pos[:, None, :]) <= SW_WINDOW)
             & (k_pos >= 0)[:, None, :] & (k_pos < s)[:, None, :])
    s_loc = jnp.einsum('bnqhgd,bnkhd->bnhgqk', qb, kb).astype(jnp.float32) * scale
    s_loc = jnp.where(valid[None, :, None, None], s_loc, -jnp.inf)
    s_ctx = jnp.einsum('bnqhgd,bkhd->bnhgqk', qb, k_ctx).astype(jnp.float32) * scale
    s_sink = jnp.broadcast_to(sink.astype(jnp.float32).reshape(1, 1, SW_KV_HEADS, SW_GROUP, 1, 1),
                              s_loc.shape[:-1] + (1,))
    p = jax.nn.softmax(jnp.concatenate([s_loc, s_ctx, s_sink], axis=-1), axis=-1).astype(v.dtype)
    o = (jnp.einsum('bnhgqk,bnkhd->bnqhgd', p[..., :nk], vb)
         + jnp.einsum('bnhgqk,bkhd->bnqhgd', p[..., nk:nk + k_ctx.shape[1]], v_ctx))
    return o.reshape(b, s, SW_Q_DIM)


def short_gated_conv(p, conv_w):
    gate_b, gate_c, u = jnp.split(p, 3, axis=-1)
    z = gate_c * u
    n = z.shape[1]
    half = SC_WIDTH // 2
    zp = jnp.pad(z, ((0, 0), (half, half), (0, 0)))
    y = zp[:, 0:n] * conv_w[0]
    for j in range(1, SC_WIDTH):
        y = y + zp[:, j:j + n] * conv_w[j]
    return gate_b * y


def neighborhood_attention(q, k, v, k_ctx, v_ctx, rpb):
    b, s = q.shape[0], q.shape[1]
    rows = s // GRID_W
    win_r = min(NA_WIN_ROWS, rows)
    scale = HEAD_DIM ** -0.5
    grid = (b, rows, GRID_W, NA_HEADS, HEAD_DIM)
    qg, kg, vg = q.reshape(grid), k.reshape(grid), v.reshape(grid)
    r = jnp.arange(rows)
    r0 = jnp.clip(r - win_r // 2, 0, rows - win_r)
    row_idx = r0[:, None] + jnp.arange(win_r)[None, :]
    k_rows = kg[:, row_idx]
    v_rows = vg[:, row_idx]
    col = jnp.arange(GRID_W)
    c0 = jnp.clip(col - NA_WIN_COLS // 2, 0, GRID_W - NA_WIN_COLS)
    col_ok = (col[None, :] >= c0[:, None]) & (col[None, :] < c0[:, None] + NA_WIN_COLS)
    dr = row_idx - r[:, None] + NA_WIN_ROWS - 1
    dc = jnp.clip(col[None, :] - col[:, None] + NA_WIN_COLS - 1, 0, 2 * NA_WIN_COLS - 2)
    bias = rpb[:, dr[:, None, :, None], dc[None, :, None, :]]
    bias = jnp.where(col_ok[None, None, :, None, :], bias.astype(jnp.float32), -jnp.inf)
    s_loc = (jnp.einsum('brqhd,brwkhd->brhqwk', qg, k_rows).astype(jnp.float32) * scale
             + jnp.transpose(bias, (1, 0, 2, 3, 4))[None])
    n_loc = win_r * GRID_W
    s_loc = s_loc.reshape(b, rows, NA_HEADS, GRID_W, n_loc)
    s_ctx = jnp.einsum('brqhd,bkhd->brhqk', qg, k_ctx).astype(jnp.float32) * scale
    p = jax.nn.softmax(jnp.concatenate([s_loc, s_ctx], axis=-1), axis=-1).astype(v.dtype)
    p_loc = p[..., :n_loc].reshape(b, rows, NA_HEADS, GRID_W, win_r, GRID_W)
    o = (jnp.einsum('brhqwk,brwkhd->brqhd', p_loc, v_rows)
         + jnp.einsum('brhqk,bkhd->brqhd', p[..., n_loc:], v_ctx))
    return o.reshape(b, s, NA_DIM)


def fourier_mix(f):
    b, n, _ = f.shape
    fg = f.reshape(b, n, FOURIER_GROUPS, FOURIER_DIM // FOURIER_GROUPS).astype(jnp.float32)
    y = jnp.fft.fft2(fg, axes=(1, 3), norm='ortho').real
    return y.reshape(b, n, FOURIER_DIM).astype(f.dtype)


def even_mixer(h, h_ctx, w_in, w_out, conv_w, sink, cos, sin, need_ctx):
    b, s, _ = h.shape
    n_ctx = h_ctx.shape[1]
    cuts = [SW_Q_DIM, SW_Q_DIM + SW_KV_DIM, SW_Q_DIM + 2 * SW_KV_DIM]
    q, k, v, sc = jnp.split(h @ w_in, cuts, axis=-1)
    q = apply_rope(q.reshape(b, s, SW_KV_HEADS, SW_GROUP, HEAD_DIM), cos, sin)
    k = apply_rope(k.reshape(b, s, SW_KV_HEADS, HEAD_DIM), cos, sin)
    v = v.reshape(b, s, SW_KV_HEADS, HEAD_DIM)
    if need_ctx:
        q_c, k_c, v_c, sc_c = jnp.split(h_ctx @ w_in, cuts, axis=-1)
    else:
        k_c, v_c = jnp.split(h_ctx @ w_in[:, cuts[0]:cuts[2]], 2, axis=-1)
    k_c = k_c.reshape(b, n_ctx, SW_KV_HEADS, HEAD_DIM)
    v_c = v_c.reshape(b, n_ctx, SW_KV_HEADS, HEAD_DIM)
    attn = window_attention(q, k, v, k_c, v_c, sink)
    conv = short_gated_conv(sc, conv_w)
    y = jnp.concatenate([attn, conv], axis=-1) @ w_out
    y_ctx = None
    if need_ctx:
        q_c = q_c.reshape(b, n_ctx, SW_KV_HEADS, SW_GROUP, HEAD_DIM)
        attn_c = context_attention(q_c, k_c, v_c, sink)
        conv_c = short_gated_conv(sc_c, conv_w)
        y_ctx = jnp.concatenate([attn_c, conv_c], axis=-1) @ w_out
    return y, y_ctx


def odd_mixer(h, h_ctx, w_in, w_out, rpb, need_ctx):
    b, s, _ = h.shape
    n_ctx = h_ctx.shape[1]
    cuts = [NA_DIM, 2 * NA_DIM, 3 * NA_DIM]
    q, k, v, f = jnp.split(h @ w_in, cuts, axis=-1)
    heads = (b, s, NA_HEADS, HEAD_DIM)
    q, k, v = q.reshape(heads), k.reshape(heads), v.reshape(heads)
    if need_ctx:
        q_c, k_c, v_c, f_c = jnp.split(h_ctx @ w_in, cuts, axis=-1)
    else:
        k_c, v_c = jnp.split(h_ctx @ w_in[:, cuts[0]:cuts[2]], 2, axis=-1)
    k_c = k_c.reshape(b, n_ctx, NA_HEADS, HEAD_DIM)
    v_c = v_c.reshape(b, n_ctx, NA_HEADS, HEAD_DIM)
    attn = neighborhood_attention(q, k, v, k_c, v_c, rpb)
    four = fourier_mix(f)
    y = jnp.concatenate([attn, four], axis=-1) @ w_out
    y_ctx = None
    if need_ctx:
        q_c = q_c.reshape(b, n_ctx, NA_HEADS, 1, HEAD_DIM)
        attn_c = context_attention(q_c, k_c, v_c, None)
        y_ctx = jnp.concatenate([attn_c, fourier_mix(f_c)], axis=-1) @ w_out
    return y, y_ctx


def moe_ffn(h, router_w, router_b, w_gu, b_gu, w_dn, b_dn):
    n_tok = h.shape[0]
    logits = (h @ router_w + router_b).astype(jnp.float32)
    top_logit, top_e = lax.top_k(logits, TOP_K)
    top_w = jax.nn.softmax(top_logit, axis=-1)
    n_assign = n_tok * TOP_K
    flat_e = top_e.reshape(-1)
    flat_tok = jnp.repeat(jnp.arange(n_tok, dtype=jnp.int32), TOP_K)
    flat_w = top_w.reshape(-1)
    order = jnp.argsort(flat_e)
    sorted_e = flat_e[order]
    counts = jnp.bincount(flat_e, length=N_EXPERTS)
    padded = (counts + MOE_BLOCK - 1) // MOE_BLOCK * MOE_BLOCK
    pad_end = jnp.cumsum(padded)
    pad_start = pad_end - padded
    start = jnp.cumsum(counts) - counts
    dest = pad_start[sorted_e] + jnp.arange(n_assign, dtype=jnp.int32) - start[sorted_e]
    n_blocks = -(-n_assign // MOE_BLOCK) + N_EXPERTS
    cap = n_blocks * MOE_BLOCK
    tok_buf = jnp.zeros((cap,), jnp.int32).at[dest].set(flat_tok[order])
    w_buf = jnp.zeros((cap,), h.dtype).at[dest].set(flat_w[order].astype(h.dtype))
    block_e = jnp.minimum(
        jnp.searchsorted(pad_end, jnp.arange(n_blocks, dtype=jnp.int32) * MOE_BLOCK, side='right'),
        N_EXPERTS - 1)

    def run_block(args):
        tok, e, w = args
        xb = h[tok]
        gu = xb @ w_gu[e] + b_gu[e]
        g, u = jnp.split(gu, 2, axis=-1)
        g = jnp.minimum(g, SWIGLU_LIMIT)
        u = jnp.clip(u, -SWIGLU_LIMIT, SWIGLU_LIMIT)
        act = (u + 1) * (g * jax.nn.sigmoid(SWIGLU_ALPHA * g))
        return (act @ w_dn[e] + b_dn[e]) * w[:, None]

    ys = lax.map(run_block, (tok_buf.reshape(n_blocks, MOE_BLOCK), block_e,
                             w_buf.reshape(n_blocks, MOE_BLOCK)))
    return jnp.zeros_like(h).at[tok_buf].add(ys.reshape(cap, -1))


def setup_inputs(seed: int = 0) -> dict:
    key = jax.random.key(seed)
    ks = jax.random.split(key, 22)
    n_even = (DEPTH + 1) // 2
    n_odd = DEPTH // 2
    D = D_MODEL

    def nrm(k, shape, scale):
        return jax.random.normal(k, shape, jnp.float32) * scale

    return {
        'x': nrm(ks[0], (BATCH, SEQ, D), 1.0),
        'c': nrm(ks[1], (BATCH, D), 1.0),
        'ctx': nrm(ks[2], (BATCH, CTX_LEN, D), 1.0),
        'c_ctx': nrm(ks[3], (D,), 1.0),
        'ada_w': nrm(ks[4], (DEPTH, D, 6 * D), D ** -0.5),
        'ada_b': nrm(ks[5], (DEPTH, 6 * D), 0.02),
        'norm_mix': 1.0 + nrm(ks[6], (DEPTH, D), 0.02),
        'norm_ffn': 1.0 + nrm(ks[7], (DEPTH, D), 0.02),
        'even_w_in': nrm(ks[8], (n_even, D, EVEN_IN), D ** -0.5),
        'even_w_out': nrm(ks[9], (n_even, EVEN_OUT, D), EVEN_OUT ** -0.5),
        'even_conv_w': nrm(ks[10], (n_even, SC_WIDTH, SC_DIM), SC_WIDTH ** -0.5),
        'even_sink': nrm(ks[11], (n_even, SW_HEADS), 0.5),
        'odd_w_in': nrm(ks[12], (n_odd, D, ODD_IN), D ** -0.5),
        'odd_w_out': nrm(ks[13], (n_odd, ODD_OUT, D), ODD_OUT ** -0.5),
        'odd_rpb': nrm(ks[14], (n_odd, NA_HEADS, 2 * NA_WIN_ROWS - 1, 2 * NA_WIN_COLS - 1), 0.1),
        'router_w': nrm(ks[15], (DEPTH, D, N_EXPERTS), D ** -0.5),
        'router_b': nrm(ks[16], (DEPTH, N_EXPERTS), 0.01),
        'moe_w_gu': nrm(ks[17], (DEPTH, N_EXPERTS, D, 2 * D_FF), D ** -0.5),
        'moe_b_gu': nrm(ks[18], (DEPTH, N_EXPERTS, 2 * D_FF), 0.01),
        'moe_w_dn': nrm(ks[19], (DEPTH, N_EXPERTS, D_FF, D), D_FF ** -0.5),
        'moe_b_dn': nrm(ks[20], (DEPTH, N_EXPERTS, D), 0.01),
        'final_norm': 1.0 + nrm(ks[21], (D,), 0.02),
    }


def reference(x, c, ctx, c_ctx, ada_w, ada_b, norm_mix, norm_ffn, even_w_in, even_w_out,
              even_conv_w, even_sink, odd_w_in, odd_w_out, odd_rpb, router_w, router_b,
              moe_w_gu, moe_b_gu, moe_w_dn, moe_b_dn, final_norm):
    b, s, d = x.shape
    cos, sin = axial_rope(s)
    x_ctx = ctx
    for l in range(DEPTH):
        need_ctx = l < DEPTH - 1
        mod = jax.nn.silu(c) @ ada_w[l] + ada_b[l]
        mod_c = jax.nn.silu(c_ctx) @ ada_w[l] + ada_b[l]
        sh1, sc1, g1, sh2, sc2, g2 = jnp.split(mod[:, None, :], 6, axis=-1)
        sh1c, sc1c, g1c, sh2c, sc2c, g2c = jnp.split(mod_c, 6, axis=-1)
        h = rms_norm(x, norm_mix[l]) * (1 + sc1) + sh1
        h_c = rms_norm(x_ctx, norm_mix[l]) * (1 + sc1c) + sh1c
        if l % 2 == 0:
            i = l // 2
            y, y_c = even_mixer(h, h_c, even_w_in[i], even_w_out[i], even_conv_w[i], even_sink[i],
                                cos, sin, need_ctx)
        else:
            i = l // 2
            y, y_c = odd_mixer(h, h_c, odd_w_in[i], odd_w_out[i], odd_rpb[i], need_ctx)
        x = x + g1 * y
        h2 = (rms_norm(x, norm_ffn[l]) * (1 + sc2) + sh2).reshape(b * s, d)
        if need_ctx:
            x_ctx = x_ctx + g1c * y_c
            h2_c = (rms_norm(x_ctx, norm_ffn[l]) * (1 + sc2c) + sh2c).reshape(-1, d)
            out = moe_ffn(jnp.concatenate([h2, h2_c], axis=0), router_w[l], router_b[l],
                          moe_w_gu[l], moe_b_gu[l], moe_w_dn[l], moe_b_dn[l])
            x = x + g2 * out[:b * s].reshape(b, s, d)
            x_ctx = x_ctx + g2c * out[b * s:].reshape(x_ctx.shape)
        else:
            out = moe_ffn(h2, router_w[l], router_b[l], moe_w_gu[l], moe_b_gu[l],
                          moe_w_dn[l], moe_b_dn[l])
            x = x + g2 * out.reshape(b, s, d)
    return rms_norm(x, final_norm)
```

```python
import functools

import numpy as np
import jax
import jax.numpy as jnp
from jax import lax
from jax.experimental import pallas as pl
from jax.experimental.pallas import tpu as pltpu

F32 = jnp.float32
BF16 = jnp.bfloat16
I32 = jnp.int32

D = 1024
BATCH = 4
SEQ = 4096
CTX = 256
GRID_W = 64
HEAD_DIM = 64
EPS = 1e-6
ROPE_THETA = 10000.0
N_EXPERTS = 32
TOP_K = 4
SWIGLU_LIMIT = 7.0
SWIGLU_ALPHA = 1.702
NA_ROWS = 8
NA_COLS = 16

N_MAIN = BATCH * SEQ
N_CTX = BATCH * CTX
N_ALL = N_MAIN + N_CTX
TM = 256
NT_MAIN = N_MAIN // TM
NT_ALL = N_ALL // TM
TILES_PER_SEQ = SEQ // TM
TMM = 256
TT = 256
VMEM_LIMIT = 56 * 1024 * 1024


def _cparams(sem, vmem=VMEM_LIMIT):
    return pltpu.CompilerParams(dimension_semantics=sem, vmem_limit_bytes=vmem)


def _rms_mod(x, g, sc, sh):
    ms = jnp.mean(x * x, axis=-1, keepdims=True)
    return (x * lax.rsqrt(ms + EPS) * g) * (1.0 + sc) + sh


def _ada_kernel(c_ref, w_ref, b_ref, o_ref):
    c = c_ref[...]
    s = c * (1.0 / (1.0 + jnp.exp(-c)))
    o_ref[...] = jnp.dot(s, w_ref[...], preferred_element_type=F32,
                         precision=lax.Precision.HIGHEST) + b_ref[...]


def _ada(cc, ada_w, ada_b):
    n_l = ada_w.shape[0]
    tn = 1024
    return pl.pallas_call(
        _ada_kernel,
        out_shape=jax.ShapeDtypeStruct((n_l, 8, 6 * D), F32),
        grid=(n_l, 6 * D // tn),
        in_specs=[pl.BlockSpec((8, D), lambda l, j: (0, 0)),
                  pl.BlockSpec((None, D, tn), lambda l, j: (l, 0, j)),
                  pl.BlockSpec((None, 1, tn), lambda l, j: (l, 0, j))],
        out_specs=pl.BlockSpec((None, 8, tn), lambda l, j: (l, 0, j)),
        compiler_params=_cparams(("arbitrary", "arbitrary")),
        name="ada_mod",
    )(cc, ada_w, ada_b.reshape(n_l, 1, 6 * D))


def _rope_apply(t, cos, sin):
    n = t.shape[1]
    lane = lax.broadcasted_iota(I32, t.shape, 1)
    fwd = pltpu.roll(t, n - 32, 1)
    bwd = pltpu.roll(t, 32, 1)
    rot = jnp.where((lane % 64) < 32, fwd, bwd)
    reps = n // 128
    cosf = jnp.concatenate([cos] * reps, axis=1) if reps > 1 else cos
    sinf = jnp.concatenate([sin] * reps, axis=1) if reps > 1 else sin
    return t * cosf + rot * sinf


def _in_even_kernel(grp_ref, rblk_ref, x_ref, mod_ref, g_ref, w_ref, cos_ref, sin_ref,
                    q_ref, k_ref, ks_ref, v_ref, vs_ref, gb_ref, z_ref):
    h = _rms_mod(x_ref[...], g_ref[...], mod_ref[1:2, :], mod_ref[0:1, :])
    p = jnp.dot(h.astype(BF16), w_ref[...], preferred_element_type=F32)
    cos = cos_ref[...]
    sin = sin_ref[...]
    q = _rope_apply(p[:, 0:512], cos, sin) * (HEAD_DIM ** -0.5)
    k = _rope_apply(p[:, 512:640], cos, sin)
    v = p[:, 640:768]
    q_ref[...] = q.astype(BF16)
    k_ref[...] = k.astype(BF16)
    ks_ref[...] = pltpu.roll(k, 64, 1).astype(BF16)
    v_ref[...] = v.astype(BF16)
    vs_ref[...] = pltpu.roll(v, 64, 1).astype(BF16)
    gb_ref[...] = p[:, 768:1280]
    z_ref[...] = p[:, 1280:1792] * p[:, 1792:2304]


def _in_odd_kernel(grp_ref, x_ref, mod_ref, g_ref, w_ref, q_ref, k_ref, v_ref, f_ref):
    h = _rms_mod(x_ref[...], g_ref[...], mod_ref[1:2, :], mod_ref[0:1, :])
    p = jnp.dot(h.astype(BF16), w_ref[...], preferred_element_type=F32)
    q_ref[...] = (p[:, 0:512] * (HEAD_DIM ** -0.5)).astype(BF16)
    k_ref[...] = p[:, 512:1024].astype(BF16)
    v_ref[...] = p[:, 1024:1536].astype(BF16)
    f_ref[...] = p[:, 1536:2048].astype(BF16)


def _tile_tables():
    t = np.arange(NT_ALL)
    main = t < NT_MAIN
    grp = np.where(main, t // TILES_PER_SEQ, BATCH).astype(np.int32)
    rblk = np.where(main, t % TILES_PER_SEQ, TILES_PER_SEQ).astype(np.int32)
    first = np.where(main, (t % TILES_PER_SEQ) == 0, True).astype(np.int32)
    last = np.where(main, (t % TILES_PER_SEQ) == TILES_PER_SEQ - 1, True).astype(np.int32)
    return grp, rblk, first, last


def _rope_tables():
    t = jnp.arange(SEQ, dtype=I32)
    row = (t // GRID_W).astype(F32)
    col = (t % GRID_W).astype(F32)
    n_freq = HEAD_DIM // 4
    inv_freq = jnp.power(ROPE_THETA, -jnp.arange(n_freq, dtype=F32) / n_freq)
    ang = jnp.concatenate([row[:, None] * inv_freq, col[:, None] * inv_freq], axis=-1)
    cos = jnp.cos(ang)
    sin = jnp.sin(ang)
    cos_f = jnp.concatenate([cos, cos, cos, cos], axis=1)
    sin_f = jnp.concatenate([-sin, sin, -sin, sin], axis=1)
    cos_f = jnp.concatenate([cos_f, jnp.ones((TM, 128), F32)], axis=0)
    sin_f = jnp.concatenate([sin_f, jnp.zeros((TM, 128), F32)], axis=0)
    return cos_f, sin_f


def _in_even(xall, mod, g, w_bf, cos_f, sin_f):
    grp, rblk, _, _ = _tile_tables()
    row = lambda n, dt: jax.ShapeDtypeStruct((N_ALL, n), dt)
    tile = lambda n: pl.BlockSpec((TM, n), lambda i, grp, rb: (i, 0))
    return pl.pallas_call(
        _in_even_kernel,
        out_shape=(row(512, BF16), row(128, BF16), row(128, BF16), row(128, BF16), row(128, BF16),
                   row(512, F32), row(512, F32)),
        grid_spec=pltpu.PrefetchScalarGridSpec(
            num_scalar_prefetch=2,
            grid=(NT_ALL,),
            in_specs=[tile(D),
                      pl.BlockSpec((None, 6, D), lambda i, grp, rb: (grp[i], 0, 0)),
                      pl.BlockSpec((1, D), lambda i, grp, rb: (0, 0)),
                      pl.BlockSpec((D, 2304), lambda i, grp, rb: (0, 0)),
                      pl.BlockSpec((TM, 128), lambda i, grp, rb: (rb[i], 0)),
                      pl.BlockSpec((TM, 128), lambda i, grp, rb: (rb[i], 0))],
            out_specs=(tile(512), tile(128), tile(128), tile(128), tile(128), tile(512), tile(512)),
        ),
        compiler_params=_cparams(("arbitrary",)),
        name="in_proj_even",
    )(jnp.asarray(grp), jnp.asarray(rblk), xall, mod, g, w_bf, cos_f, sin_f)


def _in_odd(xall, mod, g, w_bf):
    grp, _, _, _ = _tile_tables()
    row = lambda n, dt: jax.ShapeDtypeStruct((N_ALL, n), dt)
    tile = lambda n: pl.BlockSpec((TM, n), lambda i, grp: (i, 0))
    return pl.pallas_call(
        _in_odd_kernel,
        out_shape=(row(512, BF16), row(512, BF16), row(512, BF16), row(512, BF16)),
        grid_spec=pltpu.PrefetchScalarGridSpec(
            num_scalar_prefetch=1,
            grid=(NT_ALL,),
            in_specs=[tile(D),
                      pl.BlockSpec((None, 6, D), lambda i, grp: (grp[i], 0, 0)),
                      pl.BlockSpec((1, D), lambda i, grp: (0, 0)),
                      pl.BlockSpec((D, 2048), lambda i, grp: (0, 0))],
            out_specs=(tile(512), tile(512), tile(512), tile(512)),
        ),
        compiler_params=_cparams(("arbitrary",)),
        name="in_proj_odd",
    )(jnp.asarray(grp), xall, mod, g, w_bf)


def _nt(a, b):
    return lax.dot_general(a, b, (((1,), (1,)), ((), ())), preferred_element_type=F32)


def _half_mask(shape, half):
    lane = lax.broadcasted_iota(I32, shape, 1)
    return (lane < 64) if half == 0 else (lane >= 64)


def _win_kernel(sink_ref, q_ref, k_ref, ks_ref, v_ref, vs_ref, kc_ref, ksc_ref, vc_ref, vsc_ref, o_ref):
    n = pl.program_id(1)
    start = pl.multiple_of(jnp.clip((n - 1) * 128, 0, SEQ - 384), 128)
    win = pl.ds(start, 384)
    row = lax.broadcasted_iota(I32, (128, 384), 0)
    col = lax.broadcasted_iota(I32, (128, 384), 1)
    valid = jnp.abs((n * 128 + row) - (start + col)) <= 128
    kk = (jnp.concatenate([k_ref[win, :], kc_ref[...]], axis=0),
          jnp.concatenate([ks_ref[win, :], ksc_ref[...]], axis=0))
    vv = (jnp.concatenate([v_ref[win, :], vc_ref[...]], axis=0),
          jnp.concatenate([vs_ref[win, :], vsc_ref[...]], axis=0))
    for c in range(4):
        qc = q_ref[:, c * 128:(c + 1) * 128]
        halves = []
        for hf in range(2):
            h = 2 * c + hf
            swapped = 0 if (h // 4) == hf else 1
            qm = jnp.where(_half_mask(qc.shape, hf), qc, jnp.zeros_like(qc))
            s = _nt(qm, kk[swapped])
            s_loc = jnp.where(valid, s[:, :384], -jnp.inf)
            s_ctx = s[:, 384:]
            sink = sink_ref[h]
            m = jnp.maximum(jnp.maximum(jnp.max(s_loc, axis=1, keepdims=True),
                                        jnp.max(s_ctx, axis=1, keepdims=True)), sink)
            p_loc = jnp.exp(s_loc - m)
            p_ctx = jnp.exp(s_ctx - m)
            den = (jnp.sum(p_loc, axis=1, keepdims=True) + jnp.sum(p_ctx, axis=1, keepdims=True)
                   + jnp.exp(sink - m))
            p = jnp.concatenate([p_loc, p_ctx], axis=1).astype(BF16)
            halves.append(jnp.dot(p, vv[swapped], preferred_element_type=F32) / den)
        o_ref[:, c * 128:(c + 1) * 128] = jnp.where(_half_mask(halves[0].shape, 0),
                                                    halves[0], halves[1]).astype(BF16)


def _win_attn(sink, q, k, ks, v, vs):
    nb = SEQ // 128
    seq_spec = pl.BlockSpec((SEQ, 128), lambda b, n: (b, 0))
    ctx_spec = pl.BlockSpec((CTX, 128), lambda b, n: (N_MAIN // CTX + b, 0))
    return pl.pallas_call(
        _win_kernel,
        out_shape=jax.ShapeDtypeStruct((N_MAIN, 512), BF16),
        grid=(BATCH, nb),
        in_specs=[pl.BlockSpec(memory_space=pltpu.SMEM),
                  pl.BlockSpec((128, 512), lambda b, n: (b * (SEQ // 128) + n, 0)),
                  seq_spec, seq_spec, seq_spec, seq_spec,
                  ctx_spec, ctx_spec, ctx_spec, ctx_spec],
        out_specs=pl.BlockSpec((128, 512), lambda b, n: (b * (SEQ // 128) + n, 0)),
        compiler_params=_cparams(("arbitrary", "arbitrary")),
        name="window_attn",
    )(sink, q, k, ks, v, vs, k, ks, v, vs)


def _ctx_attn_kernel(sink_ref, q_ref, k_ref, ks_ref, v_ref, vs_ref, o_ref):
    kk = (k_ref[...], ks_ref[...])
    vv = (v_ref[...], vs_ref[...])
    for c in range(4):
        qc = q_ref[:, c * 128:(c + 1) * 128]
        halves = []
        for hf in range(2):
            h = 2 * c + hf
            swapped = 0 if (h // 4) == hf else 1
            qm = jnp.where(_half_mask(qc.shape, hf), qc, jnp.zeros_like(qc))
            s = _nt(qm, kk[swapped])
            sink = sink_ref[h]
            m = jnp.maximum(jnp.max(s, axis=1, keepdims=True), sink)
            p = jnp.exp(s - m)
            den = jnp.sum(p, axis=1, keepdims=True) + jnp.exp(sink - m)
            halves.append(jnp.dot(p.astype(BF16), vv[swapped], preferred_element_type=F32) / den)
        o_ref[:, c * 128:(c + 1) * 128] = jnp.where(_half_mask(halves[0].shape, 0),
                                                    halves[0], halves[1]).astype(BF16)


def _ctx_attn(sink, q, k, ks, v, vs):
    ctx_spec = lambda n: pl.BlockSpec((CTX, n), lambda b: (N_MAIN // CTX + b, 0))
    return pl.pallas_call(
        _ctx_attn_kernel,
        out_shape=jax.ShapeDtypeStruct((N_CTX, 512), BF16),
        grid=(BATCH,),
        in_specs=[pl.BlockSpec(memory_space=pltpu.SMEM),
                  ctx_spec(512), ctx_spec(128), ctx_spec(128), ctx_spec(128), ctx_spec(128)],
        out_specs=pl.BlockSpec((CTX, 512), lambda b: (b, 0)),
        compiler_params=_cparams(("arbitrary",)),
        name="context_attn",
    )(sink, q, k, ks, v, vs)


NA_GROUP = 8
N_GRID_ROWS = SEQ // GRID_W


def _na_kernel(q_ref, k_ref, v_ref, kc_ref, vc_ref, nb_ref, o_ref):
    g = pl.program_id(1)

    def body(i, carry):
        r = g * NA_GROUP + i
        r0 = jnp.clip(r - NA_ROWS // 2, 0, N_GRID_ROWS - NA_ROWS)
        shift = r0 - r + NA_ROWS - 1
        qrows = pl.ds(pl.multiple_of(i * GRID_W, GRID_W), GRID_W)
        krows = pl.ds(pl.multiple_of(r0 * GRID_W, GRID_W), NA_ROWS * GRID_W)
        for c in range(4):
            lanes = slice(c * 128, (c + 1) * 128)
            qc = q_ref[qrows, lanes]
            kl = k_ref[krows, lanes]
            vl = v_ref[krows, lanes]
            kx = kc_ref[:, lanes]
            vx = vc_ref[:, lanes]
            halves = []
            for hf in range(2):
                h = 2 * c + hf
                qm = jnp.where(_half_mask(qc.shape, hf), qc, jnp.zeros_like(qc))
                s_loc = _nt(qm, kl) + nb_ref[h, shift]
                s_ctx = _nt(qm, kx)
                m = jnp.maximum(jnp.max(s_loc, axis=1, keepdims=True),
                                jnp.max(s_ctx, axis=1, keepdims=True))
                p_loc = jnp.exp(s_loc - m)
                p_ctx = jnp.exp(s_ctx - m)
                den = jnp.sum(p_loc, axis=1, keepdims=True) + jnp.sum(p_ctx, axis=1, keepdims=True)
                o = (jnp.dot(p_loc.astype(BF16), vl, preferred_element_type=F32)
                     + jnp.dot(p_ctx.astype(BF16), vx, preferred_element_type=F32))
                halves.append(o / den)
            o_ref[qrows, lanes] = jnp.where(_half_mask(halves[0].shape, 0),
                                            halves[0], halves[1]).astype(BF16)
        return carry

    lax.fori_loop(0, NA_GROUP, body, 0)


def _na_bias(rpb):
    col = np.arange(GRID_W)
    c0 = np.clip(col - NA_COLS // 2, 0, GRID_W - NA_COLS)
    col_ok = (col[None, :] >= c0[:, None]) & (col[None, :] < c0[:, None] + NA_COLS)
    dc = np.clip(col[None, :] - col[:, None] + NA_COLS - 1, 0, 2 * NA_COLS - 2)
    dr = np.arange(NA_ROWS)[:, None] + np.arange(NA_ROWS)[None, :]
    b = rpb[:, dr[:, :, None, None], dc[None, None, :, :]]
    b = jnp.where(col_ok[None, None, None], b.astype(F32), -jnp.inf)
    b = jnp.transpose(b, (0, 1, 3, 2, 4))
    return b.reshape(rpb.shape[0], NA_ROWS, GRID_W, NA_ROWS * GRID_W)


def _na_attn(q, k, v, nb):
    qrows = NA_GROUP * GRID_W
    n_g = SEQ // qrows
    seq_spec = pl.BlockSpec((SEQ, 512), lambda b, g: (b, 0))
    ctx_spec = pl.BlockSpec((CTX, 512), lambda b, g: (N_MAIN // CTX + b, 0))
    return pl.pallas_call(
        _na_kernel,
        out_shape=jax.ShapeDtypeStruct((N_MAIN, 512), BF16),
        grid=(BATCH, n_g),
        in_specs=[pl.BlockSpec((qrows, 512), lambda b, g: (b * n_g + g, 0)),
                  seq_spec, seq_spec, ctx_spec, ctx_spec,
                  pl.BlockSpec(nb.shape, lambda b, g: (0, 0, 0, 0))],
        out_specs=pl.BlockSpec((qrows, 512), lambda b, g: (b * n_g + g, 0)),
        compiler_params=_cparams(("arbitrary", "arbitrary")),
        name="neighborhood_attn",
    )(q, k, v, k, v, nb)


F_N2_CHUNK = 8
F_K1_CHUNK = 8


def _four1_kernel(x_ref, w_ref, t_ref):
    w = w_ref[...]
    for j in range(F_N2_CHUNK):
        res = jnp.dot(w, x_ref[:, j * 512:(j + 1) * 512], preferred_element_type=F32)
        t_ref[0, j] = res[:64].astype(BF16)
        t_ref[1, j] = res[64:].astype(BF16)


def _four2_kernel(t_ref, m_ref, cs_ref, y_ref):
    cs = cs_ref[...]
    for j in range(F_K1_CHUNK):
        lanes = slice(j * 512, (j + 1) * 512)
        tt = jnp.concatenate([t_ref[0, :, lanes], t_ref[1, :, lanes]], axis=0)
        pp = jnp.dot(m_ref[j], tt, preferred_element_type=F32)
        pc = jnp.concatenate([pp[:64], pp[64:]], axis=1).astype(BF16)
        y_ref[:, lanes] = jnp.dot(pc, cs, preferred_element_type=F32).astype(BF16)


def _fourier_tables():
    a = np.arange(64)
    ang1 = 2.0 * np.pi * np.outer(a, a) / 64.0
    w1 = np.concatenate([np.cos(ang1), -np.sin(ang1)], axis=0)
    k1 = a[:, None, None]
    k2 = a[None, :, None]
    n2 = a[None, None, :]
    theta = 2.0 * np.pi * (n2 * k2 / 64.0 + n2 * k1 / 4096.0)
    mr = np.cos(theta) / 64.0
    mi = -np.sin(theta) / 64.0
    m = np.concatenate([np.concatenate([mr, -mi], axis=2),
                        np.concatenate([mi, mr], axis=2)], axis=1)
    c = np.arange(128)
    angc = 2.0 * np.pi * np.outer(c, c) / 128.0
    eye4 = np.eye(4)
    cc = np.kron(eye4, np.cos(angc)) / np.sqrt(128.0)
    sc = np.kron(eye4, np.sin(angc)) / np.sqrt(128.0)
    cs = np.concatenate([cc, sc], axis=0)
    return tuple(jnp.asarray(t, F32).astype(BF16) for t in (w1, m, cs))


def _fourier(f):
    w1, m, cs = _fourier_tables()
    fv = f.reshape(BATCH * 64, 64 * 512)
    n_c = 64 // F_N2_CHUNK
    t = pl.pallas_call(
        _four1_kernel,
        out_shape=jax.ShapeDtypeStruct((BATCH, 2, 64, 64, 512), BF16),
        grid=(BATCH, n_c),
        in_specs=[pl.BlockSpec((64, F_N2_CHUNK * 512), lambda b, c: (b, c)),
                  pl.BlockSpec((128, 64), lambda b, c: (0, 0))],
        out_specs=pl.BlockSpec((None, 2, F_N2_CHUNK, 64, 512), lambda b, c: (b, 0, c, 0, 0)),
        compiler_params=_cparams(("arbitrary", "arbitrary")),
        name="fourier_rows",
    )(fv, w1)
    n_k = 64 // F_K1_CHUNK
    t2 = t.reshape(BATCH, 2, 64, 64 * 512)
    y = pl.pallas_call(
        _four2_kernel,
        out_shape=jax.ShapeDtypeStruct((BATCH * 64, 64 * 512), BF16),
        grid=(BATCH, n_k),
        in_specs=[pl.BlockSpec((None, 2, 64, F_K1_CHUNK * 512), lambda b, c: (b, 0, 0, c)),
                  pl.BlockSpec((F_K1_CHUNK, 128, 128), lambda b, c: (c, 0, 0)),
                  pl.BlockSpec((1024, 512), lambda b, c: (0, 0))],
        out_specs=pl.BlockSpec((64, F_K1_CHUNK * 512), lambda b, c: (b, c)),
        compiler_params=_cparams(("arbitrary", "arbitrary")),
        name="fourier_cols",
    )(t2, m, cs)
    return y.reshape(N_MAIN, 512)


def _route(h2, rwt_ref, rb_ref, carry_ref, te_ref, tw_ref, rk_ref, cnt_ref):
    logits = lax.dot_general(rwt_ref[...], h2, (((1,), (1,)), ((), ())),
                             preferred_element_type=F32,
                             precision=lax.Precision.HIGHEST) + rb_ref[...]
    eidx = lax.broadcasted_iota(I32, logits.shape, 0)
    vals = logits
    sels, tops, idxs = [], [], []
    for _ in range(TOP_K):
        m = jnp.max(vals, axis=0, keepdims=True)
        idx = jnp.min(jnp.where(vals == m, eidx, N_EXPERTS), axis=0, keepdims=True)
        sel = eidx == idx
        sels.append(sel)
        tops.append(m)
        idxs.append(idx)
        vals = jnp.where(sel, -jnp.inf, vals)
    ex = [jnp.exp(t - tops[0]) for t in tops]
    den = ex[0] + ex[1] + ex[2] + ex[3]
    onehot = jnp.zeros(logits.shape, F32)
    for sel in sels:
        onehot = onehot + sel.astype(F32)
    r_i = lax.broadcasted_iota(I32, (TM, TM), 0)
    c_i = lax.broadcasted_iota(I32, (TM, TM), 1)
    upper = (r_i < c_i).astype(BF16)
    prefix = jnp.dot(onehot.astype(BF16), upper, preferred_element_type=F32)
    base = carry_ref[:, 0:1] + prefix
    for k in range(TOP_K):
        te_ref[k:k + 1, :] = idxs[k]
        tw_ref[k:k + 1, :] = ex[k] / den
        rk_ref[k:k + 1, :] = jnp.sum(jnp.where(sels[k], base, 0.0), axis=0, keepdims=True).astype(I32)
    new_carry = carry_ref[...] + jnp.sum(onehot, axis=1, keepdims=True)
    carry_ref[...] = new_carry
    cnt_ref[...] = new_carry


def _out_tail(i, x, y, mod_ref, g_ref, rwt_ref, rb_ref, carry_ref,
              xn_ref, h2_ref, te_ref, tw_ref, rk_ref, cnt_ref):
    @pl.when(i == 0)
    def _():
        carry_ref[...] = jnp.zeros_like(carry_ref)

    xn = x + mod_ref[2:3, :] * y
    xn_ref[...] = xn
    h2 = _rms_mod(xn, g_ref[...], mod_ref[4:5, :], mod_ref[3:4, :])
    h2_ref[...] = h2
    _route(h2, rwt_ref, rb_ref, carry_ref, te_ref, tw_ref, rk_ref, cnt_ref)


def _out_even_kernel(grp_ref, first_ref, last_ref,
                     x_ref, a_ref, gb_ref, z_ref, zp_ref, zn_ref, cw_ref, w_ref, mod_ref, g_ref,
                     rwt_ref, rb_ref,
                     xn_ref, h2_ref, te_ref, tw_ref, rk_ref, cnt_ref, carry_ref):
    i = pl.program_id(0)
    z = z_ref[...]
    zprev = jnp.where(first_ref[i] == 1, 0.0, zp_ref[7:8, :])
    znext = jnp.where(last_ref[i] == 1, 0.0, zn_ref[0:1, :])
    rid = lax.broadcasted_iota(I32, z.shape, 0)
    zm1 = jnp.where(rid == 0, zprev, pltpu.roll(z, 1, 0))
    zp1 = jnp.where(rid == TM - 1, znext, pltpu.roll(z, TM - 1, 0))
    conv = gb_ref[...] * (zm1 * cw_ref[0:1, :] + z * cw_ref[1:2, :] + zp1 * cw_ref[2:3, :])
    y = (jnp.dot(a_ref[...], w_ref[0:512, :], preferred_element_type=F32)
         + jnp.dot(conv.astype(BF16), w_ref[512:1024, :], preferred_element_type=F32))
    _out_tail(i, x_ref[...], y, mod_ref, g_ref, rwt_ref, rb_ref, carry_ref,
              xn_ref, h2_ref, te_ref, tw_ref, rk_ref, cnt_ref)


def _out_odd_kernel(grp_ref, x_ref, a_ref, f_ref, w_ref, mod_ref, g_ref, rwt_ref, rb_ref,
                    xn_ref, h2_ref, te_ref, tw_ref, rk_ref, cnt_ref, carry_ref):
    i = pl.program_id(0)
    y = (jnp.dot(a_ref[...], w_ref[0:512, :], preferred_element_type=F32)
         + jnp.dot(f_ref[...], w_ref[512:1024, :], preferred_element_type=F32))
    _out_tail(i, x_ref[...], y, mod_ref, g_ref, rwt_ref, rb_ref, carry_ref,
              xn_ref, h2_ref, te_ref, tw_ref, rk_ref, cnt_ref)


def _out_shapes(n_rows):
    return (jax.ShapeDtypeStruct((n_rows, D), F32), jax.ShapeDtypeStruct((n_rows, D), F32),
            jax.ShapeDtypeStruct((TOP_K, n_rows), I32), jax.ShapeDtypeStruct((TOP_K, n_rows), F32),
            jax.ShapeDtypeStruct((TOP_K, n_rows), I32), jax.ShapeDtypeStruct((N_EXPERTS, 128), F32))


def _out_even(xall, attn, gb, z, conv_w, w_bf, mod, g, rwt, rb):
    grp, _, first, last = _tile_tables()
    n_rows = N_ALL
    zblocks = n_rows // 8
    im = lambda f: (lambda i, grp, fi, la: f(i))
    tile = lambda n: pl.BlockSpec((TM, n), im(lambda i: (i, 0)))
    const = lambda shape: pl.BlockSpec(shape, im(lambda i: (0,) * len(shape)))
    tk = pl.BlockSpec((TOP_K, TM), im(lambda i: (0, i)))
    return pl.pallas_call(
        _out_even_kernel,
        out_shape=_out_shapes(n_rows),
        grid_spec=pltpu.PrefetchScalarGridSpec(
            num_scalar_prefetch=3,
            grid=(NT_ALL,),
            in_specs=[tile(D), tile(512), tile(512), tile(512),
                      pl.BlockSpec((8, 512), im(lambda i: (jnp.maximum(i * (TM // 8) - 1, 0), 0))),
                      pl.BlockSpec((8, 512), im(lambda i: (jnp.minimum((i + 1) * (TM // 8), zblocks - 1), 0))),
                      const((3, 512)), const((D, D)),
                      pl.BlockSpec((None, 6, D), lambda i, grp, fi, la: (grp[i], 0, 0)),
                      const((1, D)), const((N_EXPERTS, D)), const((N_EXPERTS, 1))],
            out_specs=(tile(D), tile(D), tk, tk, tk, const((N_EXPERTS, 128))),
            scratch_shapes=[pltpu.VMEM((N_EXPERTS, 128), F32)],
        ),
        compiler_params=_cparams(("arbitrary",)),
        name="out_proj_even",
    )(jnp.asarray(grp), jnp.asarray(first), jnp.asarray(last),
      xall, attn, gb, z, z, z, conv_w, w_bf, mod, g, rwt, rb)


def _out_odd(xall, attn, four, w_bf, mod, g, rwt, rb):
    grp, _, _, _ = _tile_tables()
    n_rows = N_MAIN
    im = lambda f: (lambda i, grp: f(i))
    tile = lambda n: pl.BlockSpec((TM, n), im(lambda i: (i, 0)))
    const = lambda shape: pl.BlockSpec(shape, im(lambda i: (0,) * len(shape)))
    tk = pl.BlockSpec((TOP_K, TM), im(lambda i: (0, i)))
    return pl.pallas_call(
        _out_odd_kernel,
        out_shape=_out_shapes(n_rows),
        grid_spec=pltpu.PrefetchScalarGridSpec(
            num_scalar_prefetch=1,
            grid=(NT_MAIN,),
            in_specs=[tile(D), tile(512), tile(512), const((D, D)),
                      pl.BlockSpec((None, 6, D), lambda i, grp: (grp[i], 0, 0)),
                      const((1, D)), const((N_EXPERTS, D)), const((N_EXPERTS, 1))],
            out_specs=(tile(D), tile(D), tk, tk, tk, const((N_EXPERTS, 128))),
            scratch_shapes=[pltpu.VMEM((N_EXPERTS, 128), F32)],
        ),
        compiler_params=_cparams(("arbitrary",)),
        name="out_proj_odd",
    )(jnp.asarray(grp), xall, attn, four, w_bf, mod, g, rwt, rb)


def _moe_plan(counts_f, top_e_t, rank_t, n_tok):
    counts = counts_f[:, 0].astype(I32)
    padded = (counts + TMM - 1) // TMM * TMM
    pad_end = jnp.cumsum(padded)
    pad_start = pad_end - padded
    dest = pad_start[top_e_t] + rank_t
    n_blocks = n_tok * TOP_K // TMM + N_EXPERTS
    cap = n_blocks * TMM
    blk_start = jnp.arange(n_blocks, dtype=I32) * TMM
    block_e = jnp.minimum(jnp.searchsorted(pad_end, blk_start, side='right'), N_EXPERTS - 1).astype(I32)
    n_used = (pad_end[-1] // TMM).astype(I32).reshape(1)
    r = jnp.arange(TMM, dtype=I32)[None, :]
    n_pad = (padded - counts)[:, None]
    n_spare = jnp.cumsum(TMM - n_pad[:, 0])
    spare_start = (pad_end[-1] + n_spare - (TMM - n_pad[:, 0]))[:, None]
    pad_slot = jnp.where(r < n_pad, (pad_start + counts)[:, None] + r, spare_start + r - n_pad)
    return dest, block_e, n_used, pad_slot, cap


def _dispatch_kernel(src_hbm, dst_hbm, h2_hbm, xs_hbm, src_s, dst_s, sem_i, sem):
    i = pl.program_id(0)
    c1 = pltpu.make_async_copy(src_hbm.at[i], src_s, sem_i.at[0])
    c2 = pltpu.make_async_copy(dst_hbm.at[i], dst_s, sem_i.at[1])
    c1.start()
    c2.start()
    c1.wait()
    c2.wait()
    n = src_s.shape[0]

    def row_copy(j):
        return pltpu.make_async_copy(h2_hbm.at[pl.ds(src_s[j], 1)], xs_hbm.at[pl.ds(dst_s[j], 1)], sem)

    def issue(j, c):
        row_copy(j).start()
        return c

    def drain(j, c):
        row_copy(j).wait()
        return c

    lax.fori_loop(0, n, issue, 0)
    lax.fori_loop(0, n, drain, 0)


DISPATCH_CHUNK = 1024


def _dispatch(h2, dest, pad_slot, cap, n_tok):
    tok = jnp.broadcast_to(jnp.arange(n_tok, dtype=I32)[None, :], (TOP_K, n_tok))
    src = jnp.concatenate([tok.reshape(-1), jnp.zeros((N_EXPERTS * TMM,), I32)])
    dst = jnp.concatenate([dest.reshape(-1), pad_slot.reshape(-1)])
    n_chunks = src.shape[0] // DISPATCH_CHUNK
    src = src.reshape(n_chunks, DISPATCH_CHUNK)
    dst = dst.reshape(n_chunks, DISPATCH_CHUNK)
    return pl.pallas_call(
        _dispatch_kernel,
        out_shape=jax.ShapeDtypeStruct((cap, D), F32),
        grid=(n_chunks,),
        in_specs=[pl.BlockSpec(memory_space=pl.ANY), pl.BlockSpec(memory_space=pl.ANY),
                  pl.BlockSpec(memory_space=pl.ANY)],
        out_specs=pl.BlockSpec(memory_space=pl.ANY),
        scratch_shapes=[pltpu.SMEM((DISPATCH_CHUNK,), I32), pltpu.SMEM((DISPATCH_CHUNK,), I32),
                        pltpu.SemaphoreType.DMA((2,)), pltpu.SemaphoreType.DMA],
        compiler_params=_cparams(("arbitrary",)),
        name="moe_dispatch",
    )(src, dst, h2)


def _moe_kernel(be_ref, nu_ref, x_ref, wgu_ref, bgu_ref, wdn_ref, bdn_ref, y_ref, wgu_bf, wdn_bf):
    i = pl.program_id(0)
    prev = be_ref[jnp.maximum(i - 1, 0)]

    @pl.when((i < nu_ref[0]) & ((i == 0) | (be_ref[i] != prev)))
    def _():
        wgu_bf[...] = wgu_ref[...].astype(BF16)
        wdn_bf[...] = wdn_ref[...].astype(BF16)

    @pl.when(i < nu_ref[0])
    def _():
        xb = x_ref[...].astype(BF16)
        gu = jnp.dot(xb, wgu_bf[...], preferred_element_type=F32) + bgu_ref[...]
        gate = jnp.minimum(gu[:, :D], SWIGLU_LIMIT)
        up = jnp.clip(gu[:, D:], -SWIGLU_LIMIT, SWIGLU_LIMIT)
        act = (up + 1.0) * (gate * (1.0 / (1.0 + jnp.exp(-SWIGLU_ALPHA * gate))))
        y_ref[...] = jnp.dot(act.astype(BF16), wdn_bf[...], preferred_element_type=F32) + bdn_ref[...]

    @pl.when(i >= nu_ref[0])
    def _():
        y_ref[...] = jnp.zeros_like(y_ref)


def _moe(xs, block_e, n_used, w_gu, b_gu, w_dn, b_dn):
    n_blocks = block_e.shape[0]
    blk = lambda i, be, nu: (jnp.minimum(i, nu[0] - 1), 0)
    out_blk = lambda i, be, nu: (i, 0)
    exp3 = lambda i, be, nu: (be[jnp.minimum(i, nu[0] - 1)], 0, 0)
    return pl.pallas_call(
        _moe_kernel,
        out_shape=jax.ShapeDtypeStruct((n_blocks * TMM, D), F32),
        grid_spec=pltpu.PrefetchScalarGridSpec(
            num_scalar_prefetch=2,
            grid=(n_blocks,),
            in_specs=[pl.BlockSpec((TMM, D), blk),
                      pl.BlockSpec((None, D, 2 * D), exp3),
                      pl.BlockSpec((None, 1, 2 * D), exp3),
                      pl.BlockSpec((None, D, D), exp3),
                      pl.BlockSpec((None, 1, D), exp3)],
            out_specs=pl.BlockSpec((TMM, D), out_blk),
            scratch_shapes=[pltpu.VMEM((D, 2 * D), BF16), pltpu.VMEM((D, D), BF16)],
        ),
        compiler_params=_cparams(("arbitrary",)),
        name="moe_experts",
    )(block_e, n_used, xs, w_gu, b_gu.reshape(N_EXPERTS, 1, 2 * D), w_dn, b_dn.reshape(N_EXPERTS, 1, D))


def _combine_kernel(final, grp_ref, dst_hbm, ys_hbm, x_ref, tw_ref, mod_ref, fn_ref, o_ref,
                    dst_s, buf, sem_i, sem):
    i = pl.program_id(0)
    ci = pltpu.make_async_copy(dst_hbm.at[i], dst_s, sem_i)
    ci.start()
    ci.wait()

    def row_copy(j, k):
        return pltpu.make_async_copy(ys_hbm.at[pl.ds(dst_s[k * TT + j], 1)], buf.at[k, pl.ds(j, 1)], sem)

    def issue(j, c):
        for k in range(TOP_K):
            row_copy(j, k).start()
        return c

    def drain(j, c):
        for k in range(TOP_K):
            row_copy(j, k).wait()
        return c

    lax.fori_loop(0, TT, issue, 0)
    lax.fori_loop(0, TT, drain, 0)
    tw = tw_ref[...]
    acc = tw[:, 0:1] * buf[0]
    for k in range(1, TOP_K):
        acc = acc + tw[:, k:k + 1] * buf[k]
    out = x_ref[...] + mod_ref[5:6, :] * acc
    if final:
        ms = jnp.mean(out * out, axis=-1, keepdims=True)
        out = out * lax.rsqrt(ms + EPS) * fn_ref[...]
    o_ref[...] = out


def _combine(xn, ys, dest, top_w_t, mod, final_norm, n_tok, final):
    grp, _, _, _ = _tile_tables()
    n_tiles = n_tok // TT
    dst = dest.reshape(TOP_K, n_tiles, TT).transpose(1, 0, 2).reshape(n_tiles, TOP_K * TT)
    tw = top_w_t.T
    return pl.pallas_call(
        functools.partial(_combine_kernel, final),
        out_shape=jax.ShapeDtypeStruct((n_tok, D), F32),
        grid_spec=pltpu.PrefetchScalarGridSpec(
            num_scalar_prefetch=1,
            grid=(n_tiles,),
            in_specs=[pl.BlockSpec(memory_space=pl.ANY), pl.BlockSpec(memory_space=pl.ANY),
                      pl.BlockSpec((TT, D), lambda i, grp: (i, 0)),
                      pl.BlockSpec((TT, TOP_K), lambda i, grp: (i, 0)),
                      pl.BlockSpec((None, 6, D), lambda i, grp: (grp[i], 0, 0)),
                      pl.BlockSpec((1, D), lambda i, grp: (0, 0))],
            out_specs=pl.BlockSpec((TT, D), lambda i, grp: (i, 0)),
            scratch_shapes=[pltpu.SMEM((TOP_K * TT,), I32), pltpu.VMEM((TOP_K, TT, D), F32),
                            pltpu.SemaphoreType.DMA, pltpu.SemaphoreType.DMA],
        ),
        compiler_params=_cparams(("arbitrary",)),
        name="moe_combine",
    )(jnp.asarray(grp), dst, ys, xn, tw, mod, final_norm.reshape(1, D))


def _moe_layer(xn, h2, top_e_t, top_w_t, rank_t, counts, mod, w_gu, b_gu, w_dn, b_dn, final_norm, final):
    n_tok = xn.shape[0]
    dest, block_e, n_used, pad_slot, cap = _moe_plan(counts, top_e_t, rank_t, n_tok)
    xs = _dispatch(h2, dest, pad_slot, cap, n_tok)
    ys = _moe(xs, block_e, n_used, w_gu, b_gu, w_dn, b_dn)
    return _combine(xn, ys, dest, top_w_t, mod, final_norm, n_tok, final)


def kernel(x, c, ctx, c_ctx, ada_w, ada_b, norm_mix, norm_ffn, even_w_in, even_w_out, even_conv_w, even_sink, odd_w_in, odd_w_out, odd_rpb, router_w, router_b, moe_w_gu, moe_b_gu, moe_w_dn, moe_b_dn, final_norm):
    xall = jnp.concatenate([x.reshape(N_MAIN, D), ctx.reshape(N_CTX, D)], axis=0)
    cc = jnp.concatenate([c, c_ctx[None, :], jnp.zeros((3, D), F32)], axis=0)
    mod = _ada(cc, ada_w, ada_b).reshape(2, 8, 6, D)
    cos_f, sin_f = _rope_tables()

    q, k, ks, v, vs, gb, z = _in_even(xall, mod[0], norm_mix[0:1], even_w_in[0].astype(BF16), cos_f, sin_f)
    attn = jnp.concatenate([_win_attn(even_sink[0], q, k, ks, v, vs),
                            _ctx_attn(even_sink[0], q, k, ks, v, vs)], axis=0)
    xn, h2, te, tw, rk, cnt = _out_even(xall, attn, gb, z, even_conv_w[0], even_w_out[0].astype(BF16),
                                        mod[0], norm_ffn[0:1], router_w[0].T, router_b[0][:, None])
    xall = _moe_layer(xn, h2, te, tw, rk, cnt, mod[0], moe_w_gu[0], moe_b_gu[0], moe_w_dn[0], moe_b_dn[0],
                      final_norm, False)

    q, k, v, f = _in_odd(xall, mod[1], norm_mix[1:2], odd_w_in[0].astype(BF16))
    attn = _na_attn(q, k, v, _na_bias(odd_rpb[0]))
    four = _fourier(f[:N_MAIN])
    xn, h2, te, tw, rk, cnt = _out_odd(xall, attn, four, odd_w_out[0].astype(BF16),
                                       mod[1], norm_ffn[1:2], router_w[1].T, router_b[1][:, None])
    out = _moe_layer(xn, h2, te, tw, rk, cnt, mod[1], moe_w_gu[1], moe_b_gu[1], moe_w_dn[1], moe_b_dn[1],
                     final_norm, True)
    return out.reshape(BATCH, SEQ, D)
```

```python
import functools

import numpy as np
import jax
import jax.numpy as jnp
from jax import lax
from jax.experimental import pallas as pl
from jax.experimental.pallas import tpu as pltpu

F32 = jnp.float32
BF16 = jnp.bfloat16
I32 = jnp.int32

D = 1024
BATCH = 4
SEQ = 4096
CTX = 256
GRID_W = 64
HEAD_DIM = 64
EPS = 1e-6
ROPE_THETA = 10000.0
N_EXPERTS = 32
TOP_K = 4
SWIGLU_LIMIT = 7.0
SWIGLU_ALPHA = 1.702
NA_ROWS = 8
NA_COLS = 16

N_MAIN = BATCH * SEQ
N_CTX = BATCH * CTX
N_ALL = N_MAIN + N_CTX
TM = 256
NT_MAIN = N_MAIN // TM
NT_ALL = N_ALL // TM
TILES_PER_SEQ = SEQ // TM
TMM = 256
TT = 256
VMEM_LIMIT = 56 * 1024 * 1024


def _cparams(sem, vmem=VMEM_LIMIT):
    return pltpu.CompilerParams(dimension_semantics=sem, vmem_limit_bytes=vmem)


def _rms_mod(x, g, sc, sh):
    ms = jnp.mean(x * x, axis=-1, keepdims=True)
    return (x * lax.rsqrt(ms + EPS) * g) * (1.0 + sc) + sh


def _ada_kernel(c_ref, w_ref, b_ref, o_ref):
    c = c_ref[...]
    s = c * (1.0 / (1.0 + jnp.exp(-c)))
    o_ref[...] = jnp.dot(s, w_ref[...], preferred_element_type=F32,
                         precision=lax.Precision.HIGHEST) + b_ref[...]


def _ada(cc, ada_w, ada_b):
    n_l = ada_w.shape[0]
    tn = 1024
    return pl.pallas_call(
        _ada_kernel,
        out_shape=jax.ShapeDtypeStruct((n_l, 8, 6 * D), F32),
        grid=(n_l, 6 * D // tn),
        in_specs=[pl.BlockSpec((8, D), lambda l, j: (0, 0)),
                  pl.BlockSpec((None, D, tn), lambda l, j: (l, 0, j)),
                  pl.BlockSpec((None, 1, tn), lambda l, j: (l, 0, j))],
        out_specs=pl.BlockSpec((None, 8, tn), lambda l, j: (l, 0, j)),
        compiler_params=_cparams(("arbitrary", "arbitrary")),
        name="ada_mod",
    )(cc, ada_w, ada_b.reshape(n_l, 1, 6 * D))


def _rope_apply(t, cos, sin):
    n = t.shape[1]
    lane = lax.broadcasted_iota(I32, t.shape, 1)
    fwd = pltpu.roll(t, n - 32, 1)
    bwd = pltpu.roll(t, 32, 1)
    rot = jnp.where((lane % 64) < 32, fwd, bwd)
    reps = n // 128
    cosf = jnp.concatenate([cos] * reps, axis=1) if reps > 1 else cos
    sinf = jnp.concatenate([sin] * reps, axis=1) if reps > 1 else sin
    return t * cosf + rot * sinf


def _in_even_kernel(grp_ref, rblk_ref, x_ref, mod_ref, g_ref, w_ref, cos_ref, sin_ref,
                    q_ref, k_ref, ks_ref, v_ref, vs_ref, gb_ref, z_ref):
    h = _rms_mod(x_ref[...], g_ref[...], mod_ref[1:2, :], mod_ref[0:1, :])
    p = jnp.dot(h.astype(BF16), w_ref[...], preferred_element_type=F32)
    cos = cos_ref[...]
    sin = sin_ref[...]
    q = _rope_apply(p[:, 0:512], cos, sin) * (HEAD_DIM ** -0.5)
    k = _rope_apply(p[:, 512:640], cos, sin)
    v = p[:, 640:768]
    q_ref[...] = q.astype(BF16)
    k_ref[...] = k.astype(BF16)
    ks_ref[...] = pltpu.roll(k, 64, 1).astype(BF16)
    v_ref[...] = v.astype(BF16)
    vs_ref[...] = pltpu.roll(v, 64, 1).astype(BF16)
    gb_ref[...] = p[:, 768:1280]
    z_ref[...] = p[:, 1280:1792] * p[:, 1792:2304]


def _in_odd_kernel(grp_ref, x_ref, mod_ref, g_ref, w_ref, q_ref, k_ref, v_ref, f_ref):
    h = _rms_mod(x_ref[...], g_ref[...], mod_ref[1:2, :], mod_ref[0:1, :])
    p = jnp.dot(h.astype(BF16), w_ref[...], preferred_element_type=F32)
    q_ref[...] = (p[:, 0:512] * (HEAD_DIM ** -0.5)).astype(BF16)
    k_ref[...] = p[:, 512:1024].astype(BF16)
    v_ref[...] = p[:, 1024:1536].astype(BF16)
    f_ref[...] = p[:, 1536:2048].astype(BF16)


def _tile_tables():
    t = np.arange(NT_ALL)
    main = t < NT_MAIN
    grp = np.where(main, t // TILES_PER_SEQ, BATCH).astype(np.int32)
    rblk = np.where(main, t % TILES_PER_SEQ, TILES_PER_SEQ).astype(np.int32)
    first = np.where(main, (t % TILES_PER_SEQ) == 0, True).astype(np.int32)
    last = np.where(main, (t % TILES_PER_SEQ) == TILES_PER_SEQ - 1, True).astype(np.int32)
    return grp, rblk, first, last


def _rope_tables():
    t = jnp.arange(SEQ, dtype=I32)
    row = (t // GRID_W).astype(F32)
    col = (t % GRID_W).astype(F32)
    n_freq = HEAD_DIM // 4
    inv_freq = jnp.power(ROPE_THETA, -jnp.arange(n_freq, dtype=F32) / n_freq)
    ang = jnp.concatenate([row[:, None] * inv_freq, col[:, None] * inv_freq], axis=-1)
    cos = jnp.cos(ang)
    sin = jnp.sin(ang)
    cos_f = jnp.concatenate([cos, cos, cos, cos], axis=1)
    sin_f = jnp.concatenate([-sin, sin, -sin, sin], axis=1)
    cos_f = jnp.concatenate([cos_f, jnp.ones((TM, 128), F32)], axis=0)
    sin_f = jnp.concatenate([sin_f, jnp.zeros((TM, 128), F32)], axis=0)
    return cos_f, sin_f


def _in_even(xall, mod, g, w_bf, cos_f, sin_f):
    grp, rblk, _, _ = _tile_tables()
    row = lambda n, dt: jax.ShapeDtypeStruct((N_ALL, n), dt)
    tile = lambda n: pl.BlockSpec((TM, n), lambda i, grp, rb: (i, 0))
    return pl.pallas_call(
        _in_even_kernel,
        out_shape=(row(512, BF16), row(128, BF16), row(128, BF16), row(128, BF16), row(128, BF16),
                   row(512, F32), row(512, F32)),
        grid_spec=pltpu.PrefetchScalarGridSpec(
            num_scalar_prefetch=2,
            grid=(NT_ALL,),
            in_specs=[tile(D),
                      pl.BlockSpec((None, 6, D), lambda i, grp, rb: (grp[i], 0, 0)),
                      pl.BlockSpec((1, D), lambda i, grp, rb: (0, 0)),
                      pl.BlockSpec((D, 2304), lambda i, grp, rb: (0, 0)),
                      pl.BlockSpec((TM, 128), lambda i, grp, rb: (rb[i], 0)),
                      pl.BlockSpec((TM, 128), lambda i, grp, rb: (rb[i], 0))],
            out_specs=(tile(512), tile(128), tile(128), tile(128), tile(128), tile(512), tile(512)),
        ),
        compiler_params=_cparams(("arbitrary",)),
        name="in_proj_even",
    )(jnp.asarray(grp), jnp.asarray(rblk), xall, mod, g, w_bf, cos_f, sin_f)


def _in_odd(xall, mod, g, w_bf):
    grp, _, _, _ = _tile_tables()
    row = lambda n, dt: jax.ShapeDtypeStruct((N_ALL, n), dt)
    tile = lambda n: pl.BlockSpec((TM, n), lambda i, grp: (i, 0))
    return pl.pallas_call(
        _in_odd_kernel,
        out_shape=(row(512, BF16), row(512, BF16), row(512, BF16), row(512, BF16)),
        grid_spec=pltpu.PrefetchScalarGridSpec(
            num_scalar_prefetch=1,
            grid=(NT_ALL,),
            in_specs=[tile(D),
                      pl.BlockSpec((None, 6, D), lambda i, grp: (grp[i], 0, 0)),
                      pl.BlockSpec((1, D), lambda i, grp: (0, 0)),
                      pl.BlockSpec((D, 2048), lambda i, grp: (0, 0))],
            out_specs=(tile(512), tile(512), tile(512), tile(512)),
        ),
        compiler_params=_cparams(("arbitrary",)),
        name="in_proj_odd",
    )(jnp.asarray(grp), xall, mod, g, w_bf)


def _nt(a, b):
    return lax.dot_general(a, b, (((1,), (1,)), ((), ())), preferred_element_type=F32)


def _half_mask(shape, half):
    lane = lax.broadcasted_iota(I32, shape, 1)
    return (lane < 64) if half == 0 else (lane >= 64)


def _win_kernel(sink_ref, q_ref, k_ref, ks_ref, v_ref, vs_ref, kc_ref, ksc_ref, vc_ref, vsc_ref, o_ref):
    n = pl.program_id(1)
    start = pl.multiple_of(jnp.clip((n - 1) * 128, 0, SEQ - 384), 128)
    win = pl.ds(start, 384)
    row = lax.broadcasted_iota(I32, (128, 384), 0)
    col = lax.broadcasted_iota(I32, (128, 384), 1)
    valid = jnp.abs((n * 128 + row) - (start + col)) <= 128
    kk = (jnp.concatenate([k_ref[win, :], kc_ref[...]], axis=0),
          jnp.concatenate([ks_ref[win, :], ksc_ref[...]], axis=0))
    vv = (jnp.concatenate([v_ref[win, :], vc_ref[...]], axis=0),
          jnp.concatenate([vs_ref[win, :], vsc_ref[...]], axis=0))
    for c in range(4):
        qc = q_ref[:, c * 128:(c + 1) * 128]
        halves = []
        for hf in range(2):
            h = 2 * c + hf
            swapped = 0 if (h // 4) == hf else 1
            qm = jnp.where(_half_mask(qc.shape, hf), qc, jnp.zeros_like(qc))
            s = _nt(qm, kk[swapped])
            s_loc = jnp.where(valid, s[:, :384], -jnp.inf)
            s_ctx = s[:, 384:]
            sink = sink_ref[h]
            m = jnp.maximum(jnp.maximum(jnp.max(s_loc, axis=1, keepdims=True),
                                        jnp.max(s_ctx, axis=1, keepdims=True)), sink)
            p_loc = jnp.exp(s_loc - m)
            p_ctx = jnp.exp(s_ctx - m)
            den = (jnp.sum(p_loc, axis=1, keepdims=True) + jnp.sum(p_ctx, axis=1, keepdims=True)
                   + jnp.exp(sink - m))
            p = jnp.concatenate([p_loc, p_ctx], axis=1).astype(BF16)
            halves.append(jnp.dot(p, vv[swapped], preferred_element_type=F32) / den)
        o_ref[:, c * 128:(c + 1) * 128] = jnp.where(_half_mask(halves[0].shape, 0),
                                                    halves[0], halves[1]).astype(BF16)


def _win_attn(sink, q, k, ks, v, vs):
    nb = SEQ // 128
    seq_spec = pl.BlockSpec((SEQ, 128), lambda b, n: (b, 0))
    ctx_spec = pl.BlockSpec((CTX, 128), lambda b, n: (N_MAIN // CTX + b, 0))
    return pl.pallas_call(
        _win_kernel,
        out_shape=jax.ShapeDtypeStruct((N_MAIN, 512), BF16),
        grid=(BATCH, nb),
        in_specs=[pl.BlockSpec(memory_space=pltpu.SMEM),
                  pl.BlockSpec((128, 512), lambda b, n: (b * (SEQ // 128) + n, 0)),
                  seq_spec, seq_spec, seq_spec, seq_spec,
                  ctx_spec, ctx_spec, ctx_spec, ctx_spec],
        out_specs=pl.BlockSpec((128, 512), lambda b, n: (b * (SEQ // 128) + n, 0)),
        compiler_params=_cparams(("arbitrary", "arbitrary")),
        name="window_attn",
    )(sink, q, k, ks, v, vs, k, ks, v, vs)


def _ctx_attn_kernel(sink_ref, q_ref, k_ref, ks_ref, v_ref, vs_ref, o_ref):
    kk = (k_ref[...], ks_ref[...])
    vv = (v_ref[...], vs_ref[...])
    for c in range(4):
        qc = q_ref[:, c * 128:(c + 1) * 128]
        halves = []
        for hf in range(2):
            h = 2 * c + hf
            swapped = 0 if (h // 4) == hf else 1
            qm = jnp.where(_half_mask(qc.shape, hf), qc, jnp.zeros_like(qc))
            s = _nt(qm, kk[swapped])
            sink = sink_ref[h]
            m = jnp.maximum(jnp.max(s, axis=1, keepdims=True), sink)
            p = jnp.exp(s - m)
            den = jnp.sum(p, axis=1, keepdims=True) + jnp.exp(sink - m)
            halves.append(jnp.dot(p.astype(BF16), vv[swapped], preferred_element_type=F32) / den)
        o_ref[:, c * 128:(c + 1) * 128] = jnp.where(_half_mask(halves[0].shape, 0),
                                                    halves[0], halves[1]).astype(BF16)


def _ctx_attn(sink, q, k, ks, v, vs):
    ctx_spec = lambda n: pl.BlockSpec((CTX, n), lambda b: (N_MAIN // CTX + b, 0))
    return pl.pallas_call(
        _ctx_attn_kernel,
        out_shape=jax.ShapeDtypeStruct((N_CTX, 512), BF16),
        grid=(BATCH,),
        in_specs=[pl.BlockSpec(memory_space=pltpu.SMEM),
                  ctx_spec(512), ctx_spec(128), ctx_spec(128), ctx_spec(128), ctx_spec(128)],
        out_specs=pl.BlockSpec((CTX, 512), lambda b: (b, 0)),
        compiler_params=_cparams(("arbitrary",)),
        name="context_attn",
    )(sink, q, k, ks, v, vs)


NA_GROUP = 8
N_GRID_ROWS = SEQ // GRID_W


def _na_kernel(q_ref, k_ref, v_ref, kc_ref, vc_ref, nb_ref, o_ref):
    g = pl.program_id(1)

    def body(i, carry):
        r = g * NA_GROUP + i
        r0 = jnp.clip(r - NA_ROWS // 2, 0, N_GRID_ROWS - NA_ROWS)
        shift = r0 - r + NA_ROWS - 1
        qrows = pl.ds(pl.multiple_of(i * GRID_W, GRID_W), GRID_W)
        krows = pl.ds(pl.multiple_of(r0 * GRID_W, GRID_W), NA_ROWS * GRID_W)
        for c in range(4):
            lanes = slice(c * 128, (c + 1) * 128)
            qc = q_ref[qrows, lanes]
            kl = k_ref[krows, lanes]
            vl = v_ref[krows, lanes]
            kx = kc_ref[:, lanes]
            vx = vc_ref[:, lanes]
            halves = []
            for hf in range(2):
                h = 2 * c + hf
                qm = jnp.where(_half_mask(qc.shape, hf), qc, jnp.zeros_like(qc))
                s_loc = _nt(qm, kl) + nb_ref[h, shift]
                s_ctx = _nt(qm, kx)
                m = jnp.maximum(jnp.max(s_loc, axis=1, keepdims=True),
                                jnp.max(s_ctx, axis=1, keepdims=True))
                p_loc = jnp.exp(s_loc - m)
                p_ctx = jnp.exp(s_ctx - m)
                den = jnp.sum(p_loc, axis=1, keepdims=True) + jnp.sum(p_ctx, axis=1, keepdims=True)
                o = (jnp.dot(p_loc.astype(BF16), vl, preferred_element_type=F32)
                     + jnp.dot(p_ctx.astype(BF16), vx, preferred_element_type=F32))
                halves.append(o / den)
            o_ref[qrows, lanes] = jnp.where(_half_mask(halves[0].shape, 0),
                                            halves[0], halves[1]).astype(BF16)
        return carry

    lax.fori_loop(0, NA_GROUP, body, 0)


def _na_bias(rpb):
    col = np.arange(GRID_W)
    c0 = np.clip(col - NA_COLS // 2, 0, GRID_W - NA_COLS)
    col_ok = (col[None, :] >= c0[:, None]) & (col[None, :] < c0[:, None] + NA_COLS)
    dc = np.clip(col[None, :] - col[:, None] + NA_COLS - 1, 0, 2 * NA_COLS - 2)
    onehot = (dc[None] == np.arange(2 * NA_COLS - 1)[:, None, None]).astype(np.float32)
    e = jnp.einsum('hrd,dqk->hrqk', rpb.astype(F32), jnp.asarray(onehot),
                   precision=lax.Precision.HIGHEST)
    e = jnp.where(col_ok[None, None], e, -jnp.inf)
    b = jnp.stack([e[:, s:s + NA_ROWS] for s in range(NA_ROWS)], axis=1)
    b = jnp.transpose(b, (0, 1, 3, 2, 4))
    return b.reshape(rpb.shape[0], NA_ROWS, GRID_W, NA_ROWS * GRID_W)


def _na_attn(q, k, v, nb):
    qrows = NA_GROUP * GRID_W
    n_g = SEQ // qrows
    seq_spec = pl.BlockSpec((SEQ, 512), lambda b, g: (b, 0))
    ctx_spec = pl.BlockSpec((CTX, 512), lambda b, g: (N_MAIN // CTX + b, 0))
    return pl.pallas_call(
        _na_kernel,
        out_shape=jax.ShapeDtypeStruct((N_MAIN, 512), BF16),
        grid=(BATCH, n_g),
        in_specs=[pl.BlockSpec((qrows, 512), lambda b, g: (b * n_g + g, 0)),
                  seq_spec, seq_spec, ctx_spec, ctx_spec,
                  pl.BlockSpec(nb.shape, lambda b, g: (0, 0, 0, 0))],
        out_specs=pl.BlockSpec((qrows, 512), lambda b, g: (b * n_g + g, 0)),
        compiler_params=_cparams(("arbitrary", "arbitrary")),
        name="neighborhood_attn",
    )(q, k, v, k, v, nb)


F_N2_CHUNK = 8
F_K1_CHUNK = 8


def _four1_kernel(x_ref, w_ref, t_ref):
    w = w_ref[...]
    for j in range(F_N2_CHUNK):
        res = jnp.dot(w, x_ref[:, j * 512:(j + 1) * 512], preferred_element_type=F32)
        t_ref[0, j] = res[:64].astype(BF16)
        t_ref[1, j] = res[64:].astype(BF16)


def _four2_kernel(t_ref, m_ref, cs_ref, y_ref):
    cs = cs_ref[...]
    for j in range(F_K1_CHUNK):
        lanes = slice(j * 512, (j + 1) * 512)
        tt = jnp.concatenate([t_ref[0, :, lanes], t_ref[1, :, lanes]], axis=0)
        pp = jnp.dot(m_ref[j], tt, preferred_element_type=F32)
        pc = jnp.concatenate([pp[:64], pp[64:]], axis=1).astype(BF16)
        y_ref[:, lanes] = jnp.dot(pc, cs, preferred_element_type=F32).astype(BF16)


def _fourier_tables():
    a = np.arange(64)
    ang1 = 2.0 * np.pi * np.outer(a, a) / 64.0
    w1 = np.concatenate([np.cos(ang1), -np.sin(ang1)], axis=0)
    k1 = a[:, None, None]
    k2 = a[None, :, None]
    n2 = a[None, None, :]
    theta = 2.0 * np.pi * (n2 * k2 / 64.0 + n2 * k1 / 4096.0)
    mr = np.cos(theta) / 64.0
    mi = -np.sin(theta) / 64.0
    m = np.concatenate([np.concatenate([mr, -mi], axis=2),
                        np.concatenate([mi, mr], axis=2)], axis=1)
    c = np.arange(128)
    angc = 2.0 * np.pi * np.outer(c, c) / 128.0
    eye4 = np.eye(4)
    cc = np.kron(eye4, np.cos(angc)) / np.sqrt(128.0)
    sc = np.kron(eye4, np.sin(angc)) / np.sqrt(128.0)
    cs = np.concatenate([cc, sc], axis=0)
    return tuple(jnp.asarray(t, F32).astype(BF16) for t in (w1, m, cs))


def _fourier(f):
    w1, m, cs = _fourier_tables()
    fv = f.reshape(N_ALL // 64, 64 * 512)
    n_c = 64 // F_N2_CHUNK
    t = pl.pallas_call(
        _four1_kernel,
        out_shape=jax.ShapeDtypeStruct((BATCH, 2, 64, 64, 512), BF16),
        grid=(BATCH, n_c),
        in_specs=[pl.BlockSpec((64, F_N2_CHUNK * 512), lambda b, c: (b, c)),
                  pl.BlockSpec((128, 64), lambda b, c: (0, 0))],
        out_specs=pl.BlockSpec((None, 2, F_N2_CHUNK, 64, 512), lambda b, c: (b, 0, c, 0, 0)),
        compiler_params=_cparams(("arbitrary", "arbitrary")),
        name="fourier_rows",
    )(fv, w1)
    n_k = 64 // F_K1_CHUNK
    t2 = t.reshape(BATCH, 2, 64, 64 * 512)
    y = pl.pallas_call(
        _four2_kernel,
        out_shape=jax.ShapeDtypeStruct((BATCH * 64, 64 * 512), BF16),
        grid=(BATCH, n_k),
        in_specs=[pl.BlockSpec((None, 2, 64, F_K1_CHUNK * 512), lambda b, c: (b, 0, 0, c)),
                  pl.BlockSpec((F_K1_CHUNK, 128, 128), lambda b, c: (c, 0, 0)),
                  pl.BlockSpec((1024, 512), lambda b, c: (0, 0))],
        out_specs=pl.BlockSpec((64, F_K1_CHUNK * 512), lambda b, c: (b, c)),
        compiler_params=_cparams(("arbitrary", "arbitrary")),
        name="fourier_cols",
    )(t2, m, cs)
    return y.reshape(N_MAIN, 512)


def _route(h2, rwt_ref, rb_ref, carry_ref, te_ref, tw_ref, rk_ref, cnt_ref):
    logits = lax.dot_general(rwt_ref[...], h2, (((1,), (1,)), ((), ())),
                             preferred_element_type=F32,
                             precision=lax.Precision.HIGHEST) + rb_ref[...]
    eidx = lax.broadcasted_iota(I32, logits.shape, 0)
    vals = logits
    sels, tops, idxs = [], [], []
    for _ in range(TOP_K):
        m = jnp.max(vals, axis=0, keepdims=True)
        idx = jnp.min(jnp.where(vals == m, eidx, N_EXPERTS), axis=0, keepdims=True)
        sel = eidx == idx
        sels.append(sel)
        tops.append(m)
        idxs.append(idx)
        vals = jnp.where(sel, -jnp.inf, vals)
    ex = [jnp.exp(t - tops[0]) for t in tops]
    den = ex[0] + ex[1] + ex[2] + ex[3]
    onehot = jnp.zeros(logits.shape, F32)
    for sel in sels:
        onehot = onehot + sel.astype(F32)
    r_i = lax.broadcasted_iota(I32, (TM, TM), 0)
    c_i = lax.broadcasted_iota(I32, (TM, TM), 1)
    upper = (r_i < c_i).astype(BF16)
    prefix = jnp.dot(onehot.astype(BF16), upper, preferred_element_type=F32)
    base = carry_ref[:, 0:1] + prefix
    for k in range(TOP_K):
        te_ref[k:k + 1, :] = idxs[k]
        tw_ref[k:k + 1, :] = ex[k] / den
        rk_ref[k:k + 1, :] = jnp.sum(jnp.where(sels[k], base, 0.0), axis=0, keepdims=True).astype(I32)
    new_carry = carry_ref[...] + jnp.sum(onehot, axis=1, keepdims=True)
    carry_ref[...] = new_carry
    cnt_ref[...] = new_carry


def _out_tail(i, x, y, mod_ref, g_ref, rwt_ref, rb_ref, carry_ref,
              xn_ref, h2_ref, te_ref, tw_ref, rk_ref, cnt_ref):
    @pl.when(i == 0)
    def _():
        carry_ref[...] = jnp.zeros_like(carry_ref)

    xn = x + mod_ref[2:3, :] * y
    xn_ref[...] = xn
    h2 = _rms_mod(xn, g_ref[...], mod_ref[4:5, :], mod_ref[3:4, :])
    h2_ref[...] = h2
    _route(h2, rwt_ref, rb_ref, carry_ref, te_ref, tw_ref, rk_ref, cnt_ref)


def _out_even_kernel(grp_ref, first_ref, last_ref,
                     x_ref, a_ref, gb_ref, z_ref, zp_ref, zn_ref, cw_ref, w_ref, mod_ref, g_ref,
                     rwt_ref, rb_ref,
                     xn_ref, h2_ref, te_ref, tw_ref, rk_ref, cnt_ref, carry_ref):
    i = pl.program_id(0)
    z = z_ref[...]
    zprev = jnp.where(first_ref[i] == 1, 0.0, zp_ref[7:8, :])
    znext = jnp.where(last_ref[i] == 1, 0.0, zn_ref[0:1, :])
    rid = lax.broadcasted_iota(I32, z.shape, 0)
    zm1 = jnp.where(rid == 0, zprev, pltpu.roll(z, 1, 0))
    zp1 = jnp.where(rid == TM - 1, znext, pltpu.roll(z, TM - 1, 0))
    conv = gb_ref[...] * (zm1 * cw_ref[0:1, :] + z * cw_ref[1:2, :] + zp1 * cw_ref[2:3, :])
    y = (jnp.dot(a_ref[...], w_ref[0:512, :], preferred_element_type=F32)
         + jnp.dot(conv.astype(BF16), w_ref[512:1024, :], preferred_element_type=F32))
    _out_tail(i, x_ref[...], y, mod_ref, g_ref, rwt_ref, rb_ref, carry_ref,
              xn_ref, h2_ref, te_ref, tw_ref, rk_ref, cnt_ref)


def _out_odd_kernel(grp_ref, x_ref, a_ref, f_ref, w_ref, mod_ref, g_ref, rwt_ref, rb_ref,
                    xn_ref, h2_ref, te_ref, tw_ref, rk_ref, cnt_ref, carry_ref):
    i = pl.program_id(0)
    y = (jnp.dot(a_ref[...], w_ref[0:512, :], preferred_element_type=F32)
         + jnp.dot(f_ref[...], w_ref[512:1024, :], preferred_element_type=F32))
    _out_tail(i, x_ref[...], y, mod_ref, g_ref, rwt_ref, rb_ref, carry_ref,
              xn_ref, h2_ref, te_ref, tw_ref, rk_ref, cnt_ref)


def _out_shapes(n_rows):
    return (jax.ShapeDtypeStruct((n_rows, D), F32), jax.ShapeDtypeStruct((n_rows, D), F32),
            jax.ShapeDtypeStruct((TOP_K, n_rows), I32), jax.ShapeDtypeStruct((TOP_K, n_rows), F32),
            jax.ShapeDtypeStruct((TOP_K, n_rows), I32), jax.ShapeDtypeStruct((N_EXPERTS, 128), F32))


def _out_even(xall, attn, gb, z, conv_w, w_bf, mod, g, rwt, rb):
    grp, _, first, last = _tile_tables()
    n_rows = N_ALL
    zblocks = n_rows // 8
    im = lambda f: (lambda i, grp, fi, la: f(i))
    tile = lambda n: pl.BlockSpec((TM, n), im(lambda i: (i, 0)))
    const = lambda shape: pl.BlockSpec(shape, im(lambda i: (0,) * len(shape)))
    tk = pl.BlockSpec((TOP_K, TM), im(lambda i: (0, i)))
    return pl.pallas_call(
        _out_even_kernel,
        out_shape=_out_shapes(n_rows),
        grid_spec=pltpu.PrefetchScalarGridSpec(
            num_scalar_prefetch=3,
            grid=(NT_ALL,),
            in_specs=[tile(D), tile(512), tile(512), tile(512),
                      pl.BlockSpec((8, 512), im(lambda i: (jnp.maximum(i * (TM // 8) - 1, 0), 0))),
                      pl.BlockSpec((8, 512), im(lambda i: (jnp.minimum((i + 1) * (TM // 8), zblocks - 1), 0))),
                      const((3, 512)), const((D, D)),
                      pl.BlockSpec((None, 6, D), lambda i, grp, fi, la: (grp[i], 0, 0)),
                      const((1, D)), const((N_EXPERTS, D)), const((N_EXPERTS, 1))],
            out_specs=(tile(D), tile(D), tk, tk, tk, const((N_EXPERTS, 128))),
            scratch_shapes=[pltpu.VMEM((N_EXPERTS, 128), F32)],
        ),
        compiler_params=_cparams(("arbitrary",)),
        name="out_proj_even",
    )(jnp.asarray(grp), jnp.asarray(first), jnp.asarray(last),
      xall, attn, gb, z, z, z, conv_w, w_bf, mod, g, rwt, rb)


def _out_odd(xall, attn, four, w_bf, mod, g, rwt, rb):
    grp, _, _, _ = _tile_tables()
    n_rows = N_MAIN
    im = lambda f: (lambda i, grp: f(i))
    tile = lambda n: pl.BlockSpec((TM, n), im(lambda i: (i, 0)))
    const = lambda shape: pl.BlockSpec(shape, im(lambda i: (0,) * len(shape)))
    tk = pl.BlockSpec((TOP_K, TM), im(lambda i: (0, i)))
    return pl.pallas_call(
        _out_odd_kernel,
        out_shape=_out_shapes(n_rows),
        grid_spec=pltpu.PrefetchScalarGridSpec(
            num_scalar_prefetch=1,
            grid=(NT_MAIN,),
            in_specs=[tile(D), tile(512), tile(512), const((D, D)),
                      pl.BlockSpec((None, 6, D), lambda i, grp: (grp[i], 0, 0)),
                      const((1, D)), const((N_EXPERTS, D)), const((N_EXPERTS, 1))],
            out_specs=(tile(D), tile(D), tk, tk, tk, const((N_EXPERTS, 128))),
            scratch_shapes=[pltpu.VMEM((N_EXPERTS, 128), F32)],
        ),
        compiler_params=_cparams(("arbitrary",)),
        name="out_proj_odd",
    )(jnp.asarray(grp), xall, attn, four, w_bf, mod, g, rwt, rb)


def _moe_plan(counts_f, top_e_t, rank_t, n_tok):
    counts = counts_f[:, 0].astype(I32)
    padded = (counts + TMM - 1) // TMM * TMM
    e_i = jnp.arange(N_EXPERTS, dtype=I32)
    incl = e_i[None, :] <= e_i[:, None]
    pad_end = jnp.sum(jnp.where(incl, padded[None, :], 0), axis=1)
    pad_start = pad_end - padded
    sel = top_e_t[None] == e_i[:, None, None]
    dest = jnp.sum(jnp.where(sel, pad_start[:, None, None], 0), axis=0) + rank_t
    n_blocks = n_tok * TOP_K // TMM + N_EXPERTS
    blk_start = jnp.arange(n_blocks, dtype=I32) * TMM
    block_e = jnp.minimum(jnp.sum((blk_start[:, None] >= pad_end[None, :]).astype(I32), axis=1),
                          N_EXPERTS - 1)
    n_used = (pad_end[-1] // TMM).reshape(1)
    r = jnp.arange(TMM, dtype=I32)[None, :]
    n_pad = (padded - counts)[:, None]
    n_spare = jnp.sum(jnp.where(incl, (TMM - n_pad[:, 0])[None, :], 0), axis=1)
    spare_start = (pad_end[-1] + n_spare - (TMM - n_pad[:, 0]))[:, None]
    pad_slot = jnp.where(r < n_pad, (pad_start + counts)[:, None] + r, spare_start + r - n_pad)
    n_tiles = n_tok // TT
    dest_tiles = dest.reshape(TOP_K, n_tiles, TT).transpose(1, 0, 2).reshape(n_tiles, TOP_K * TT)
    return dest_tiles, block_e, n_used, pad_slot.reshape(-1)


N_PAD_ROWS = N_EXPERTS * TMM


def _dispatch_kernel(dst_hbm, pad_hbm, h2_ref, row0_ref, xs_hbm, dst_s, pad_s, sem_i, sem, sem_p):
    i = pl.program_id(0)
    ci = pltpu.make_async_copy(dst_hbm.at[i], dst_s, sem_i)
    ci.start()
    ci.wait()

    def issue(j, c):
        for k in range(TOP_K):
            pltpu.make_async_copy(h2_ref.at[pl.ds(j, 1)], xs_hbm.at[pl.ds(dst_s[k * TT + j], 1)], sem).start()
        return c

    lax.fori_loop(0, TT, issue, 0, unroll=4)

    @pl.when(i == 0)
    def _():
        cp = pltpu.make_async_copy(pad_hbm, pad_s, sem_i)
        cp.start()
        cp.wait()

        def issue_pad(j, c):
            pltpu.make_async_copy(row0_ref.at[pl.ds(0, 1)], xs_hbm.at[pl.ds(pad_s[j], 1)], sem_p).start()
            return c

        lax.fori_loop(0, N_PAD_ROWS, issue_pad, 0, unroll=4)
        for _ in range(N_PAD_ROWS // TT):
            pltpu.make_async_copy(h2_ref, xs_hbm.at[pl.ds(0, TT)], sem_p).wait()

    for _ in range(TOP_K):
        pltpu.make_async_copy(h2_ref, xs_hbm.at[pl.ds(0, TT)], sem).wait()


def _dispatch(h2, dest_tiles, pad_slot, cap):
    n_tiles = dest_tiles.shape[0]
    return pl.pallas_call(
        _dispatch_kernel,
        out_shape=jax.ShapeDtypeStruct((cap, D), F32),
        grid=(n_tiles,),
        in_specs=[pl.BlockSpec(memory_space=pl.ANY), pl.BlockSpec(memory_space=pl.ANY),
                  pl.BlockSpec((TT, D), lambda i: (i, 0)),
                  pl.BlockSpec((8, D), lambda i: (0, 0))],
        out_specs=pl.BlockSpec(memory_space=pl.ANY),
        scratch_shapes=[pltpu.SMEM((TOP_K * TT,), I32), pltpu.SMEM((N_PAD_ROWS,), I32),
                        pltpu.SemaphoreType.DMA, pltpu.SemaphoreType.DMA, pltpu.SemaphoreType.DMA],
        compiler_params=_cparams(("arbitrary",)),
        name="moe_dispatch",
    )(dest_tiles, pad_slot, h2, h2)


def _moe_kernel(be_ref, nu_ref, x_ref, wgu_ref, bgu_ref, wdn_ref, bdn_ref, y_ref, wgu_bf, wdn_bf):
    i = pl.program_id(0)
    prev = be_ref[jnp.maximum(i - 1, 0)]

    @pl.when((i < nu_ref[0]) & ((i == 0) | (be_ref[i] != prev)))
    def _():
        wgu_bf[...] = wgu_ref[...].astype(BF16)
        wdn_bf[...] = wdn_ref[...].astype(BF16)

    @pl.when(i < nu_ref[0])
    def _():
        xb = x_ref[...].astype(BF16)
        gu = jnp.dot(xb, wgu_bf[...], preferred_element_type=F32) + bgu_ref[...]
        gate = jnp.minimum(gu[:, :D], SWIGLU_LIMIT)
        up = jnp.clip(gu[:, D:], -SWIGLU_LIMIT, SWIGLU_LIMIT)
        act = (up + 1.0) * (gate * (1.0 / (1.0 + jnp.exp(-SWIGLU_ALPHA * gate))))
        y_ref[...] = jnp.dot(act.astype(BF16), wdn_bf[...], preferred_element_type=F32) + bdn_ref[...]

    @pl.when(i >= nu_ref[0])
    def _():
        y_ref[...] = jnp.zeros_like(y_ref)


def _moe(layer, xs, block_e, n_used, w_gu, b_gu, w_dn, b_dn):
    n_blocks = block_e.shape[0]
    n_l = w_gu.shape[0]
    blk = lambda i, be, nu: (jnp.minimum(i, nu[0] - 1), 0)
    out_blk = lambda i, be, nu: (i, 0)
    exp4 = lambda i, be, nu: (layer, be[jnp.minimum(i, nu[0] - 1)], 0, 0)
    return pl.pallas_call(
        _moe_kernel,
        out_shape=jax.ShapeDtypeStruct((n_blocks * TMM, D), F32),
        grid_spec=pltpu.PrefetchScalarGridSpec(
            num_scalar_prefetch=2,
            grid=(n_blocks,),
            in_specs=[pl.BlockSpec((TMM, D), blk),
                      pl.BlockSpec((None, None, D, 2 * D), exp4),
                      pl.BlockSpec((None, None, 1, 2 * D), exp4),
                      pl.BlockSpec((None, None, D, D), exp4),
                      pl.BlockSpec((None, None, 1, D), exp4)],
            out_specs=pl.BlockSpec((TMM, D), out_blk),
            scratch_shapes=[pltpu.VMEM((D, 2 * D), BF16), pltpu.VMEM((D, D), BF16)],
        ),
        compiler_params=_cparams(("arbitrary",)),
        name="moe_experts",
    )(block_e, n_used, xs, w_gu, b_gu.reshape(n_l, N_EXPERTS, 1, 2 * D), w_dn,
      b_dn.reshape(n_l, N_EXPERTS, 1, D))


def _combine_kernel(final, grp_ref, dst_hbm, ys_hbm, x_ref, tw_ref, mod_ref, fn_ref, o_ref,
                    dst_s, buf, sem_i, sem):
    i = pl.program_id(0)
    ci = pltpu.make_async_copy(dst_hbm.at[i], dst_s, sem_i)
    ci.start()
    ci.wait()

    def issue(j, c):
        for k in range(TOP_K):
            pltpu.make_async_copy(ys_hbm.at[pl.ds(dst_s[k * TT + j], 1)], buf.at[k, pl.ds(j, 1)], sem).start()
        return c

    lax.fori_loop(0, TT, issue, 0, unroll=4)
    for k in range(TOP_K):
        pltpu.make_async_copy(ys_hbm.at[pl.ds(0, TT)], buf.at[k], sem).wait()
    tw = tw_ref[...]
    acc = tw[:, 0:1] * buf[0]
    for k in range(1, TOP_K):
        acc = acc + tw[:, k:k + 1] * buf[k]
    out = x_ref[...] + mod_ref[5:6, :] * acc
    if final:
        ms = jnp.mean(out * out, axis=-1, keepdims=True)
        out = out * lax.rsqrt(ms + EPS) * fn_ref[...]
    o_ref[...] = out


def _combine(xn, ys, dest_tiles, top_w_t, mod, final_norm, n_tok, final):
    grp, _, _, _ = _tile_tables()
    n_tiles = n_tok // TT
    tw = top_w_t.T
    return pl.pallas_call(
        functools.partial(_combine_kernel, final),
        out_shape=jax.ShapeDtypeStruct((n_tok, D), F32),
        grid_spec=pltpu.PrefetchScalarGridSpec(
            num_scalar_prefetch=1,
            grid=(n_tiles,),
            in_specs=[pl.BlockSpec(memory_space=pl.ANY), pl.BlockSpec(memory_space=pl.ANY),
                      pl.BlockSpec((TT, D), lambda i, grp: (i, 0)),
                      pl.BlockSpec((TT, TOP_K), lambda i, grp: (i, 0)),
                      pl.BlockSpec((None, 6, D), lambda i, grp: (grp[i], 0, 0)),
                      pl.BlockSpec((1, D), lambda i, grp: (0, 0))],
            out_specs=pl.BlockSpec((TT, D), lambda i, grp: (i, 0)),
            scratch_shapes=[pltpu.SMEM((TOP_K * TT,), I32), pltpu.VMEM((TOP_K, TT, D), F32),
                            pltpu.SemaphoreType.DMA, pltpu.SemaphoreType.DMA],
        ),
        compiler_params=_cparams(("arbitrary",)),
        name="moe_combine",
    )(jnp.asarray(grp), dest_tiles, ys, xn, tw, mod, final_norm.reshape(1, D))


def _moe_layer(layer, xn, h2, top_e_t, top_w_t, rank_t, counts, mod, w_gu, b_gu, w_dn, b_dn, final_norm, final):
    n_tok = xn.shape[0]
    dest_tiles, block_e, n_used, pad_slot = _moe_plan(counts, top_e_t, rank_t, n_tok)
    xs = _dispatch(h2, dest_tiles, pad_slot, block_e.shape[0] * TMM)
    ys = _moe(layer, xs, block_e, n_used, w_gu, b_gu, w_dn, b_dn)
    return _combine(xn, ys, dest_tiles, top_w_t, mod, final_norm, n_tok, final)


def kernel(x, c, ctx, c_ctx, ada_w, ada_b, norm_mix, norm_ffn, even_w_in, even_w_out, even_conv_w, even_sink, odd_w_in, odd_w_out, odd_rpb, router_w, router_b, moe_w_gu, moe_b_gu, moe_w_dn, moe_b_dn, final_norm):
    xall = jnp.concatenate([x.reshape(N_MAIN, D), ctx.reshape(N_CTX, D)], axis=0)
    cc = jnp.concatenate([c, c_ctx[None, :], jnp.zeros((3, D), F32)], axis=0)
    mod = _ada(cc, ada_w, ada_b).reshape(2, 8, 6, D)
    cos_f, sin_f = _rope_tables()

    q, k, ks, v, vs, gb, z = _in_even(xall, mod[0], norm_mix[0:1], even_w_in[0].astype(BF16), cos_f, sin_f)
    attn = jnp.concatenate([_win_attn(even_sink[0], q, k, ks, v, vs),
                            _ctx_attn(even_sink[0], q, k, ks, v, vs)], axis=0)
    xn, h2, te, tw, rk, cnt = _out_even(xall, attn, gb, z, even_conv_w[0], even_w_out[0].astype(BF16),
                                        mod[0], norm_ffn[0:1], router_w[0].T, router_b[0][:, None])
    xall = _moe_layer(0, xn, h2, te, tw, rk, cnt, mod[0], moe_w_gu, moe_b_gu, moe_w_dn, moe_b_dn,
                      final_norm, False)

    q, k, v, f = _in_odd(xall, mod[1], norm_mix[1:2], odd_w_in[0].astype(BF16))
    attn = _na_attn(q, k, v, _na_bias(odd_rpb[0]))
    four = _fourier(f)
    xn, h2, te, tw, rk, cnt = _out_odd(xall, attn, four, odd_w_out[0].astype(BF16),
                                       mod[1], norm_ffn[1:2], router_w[1].T, router_b[1][:, None])
    out = _moe_layer(1, xn, h2, te, tw, rk, cnt, mod[1], moe_w_gu, moe_b_gu, moe_w_dn, moe_b_dn,
                     final_norm, True)
    return out.reshape(BATCH, SEQ, D)
```

```python
import functools

import numpy as np
import jax
import jax.numpy as jnp
from jax import lax
from jax.experimental import pallas as pl
from jax.experimental.pallas import tpu as pltpu

F32 = jnp.float32
BF16 = jnp.bfloat16
I32 = jnp.int32

D = 1024
BATCH = 4
SEQ = 4096
CTX = 256
GRID_W = 64
HEAD_DIM = 64
EPS = 1e-6
ROPE_THETA = 10000.0
N_EXPERTS = 32
TOP_K = 4
SWIGLU_LIMIT = 7.0
SWIGLU_ALPHA = 1.702
NA_ROWS = 8
NA_COLS = 16

N_MAIN = BATCH * SEQ
N_CTX = BATCH * CTX
N_ALL = N_MAIN + N_CTX
TM = 256
NT_MAIN = N_MAIN // TM
NT_ALL = N_ALL // TM
TILES_PER_SEQ = SEQ // TM
TMM = 256
TT = 256
VMEM_LIMIT = 56 * 1024 * 1024


def _cparams(sem, vmem=VMEM_LIMIT):
    return pltpu.CompilerParams(dimension_semantics=sem, vmem_limit_bytes=vmem)


def _rms_mod(x, g, sc, sh):
    ms = jnp.mean(x * x, axis=-1, keepdims=True)
    return (x * lax.rsqrt(ms + EPS) * g) * (1.0 + sc) + sh


def _ada_kernel(c_ref, w_ref, b_ref, o_ref):
    c = c_ref[...]
    s = c * (1.0 / (1.0 + jnp.exp(-c)))
    o_ref[...] = jnp.dot(s, w_ref[...], preferred_element_type=F32,
                         precision=lax.Precision.HIGHEST) + b_ref[...]


def _ada(cc, ada_w, ada_b):
    n_l = ada_w.shape[0]
    tn = 1024
    return pl.pallas_call(
        _ada_kernel,
        out_shape=jax.ShapeDtypeStruct((n_l, 8, 6 * D), F32),
        grid=(n_l, 6 * D // tn),
        in_specs=[pl.BlockSpec((8, D), lambda l, j: (0, 0)),
                  pl.BlockSpec((None, D, tn), lambda l, j: (l, 0, j)),
                  pl.BlockSpec((None, 1, tn), lambda l, j: (l, 0, j))],
        out_specs=pl.BlockSpec((None, 8, tn), lambda l, j: (l, 0, j)),
        compiler_params=_cparams(("arbitrary", "arbitrary")),
        name="ada_mod",
    )(cc, ada_w, ada_b.reshape(n_l, 1, 6 * D))


def _rope_apply(t, cos, sin):
    n = t.shape[1]
    lane = lax.broadcasted_iota(I32, t.shape, 1)
    fwd = pltpu.roll(t, n - 32, 1)
    bwd = pltpu.roll(t, 32, 1)
    rot = jnp.where((lane % 64) < 32, fwd, bwd)
    reps = n // 128
    cosf = jnp.concatenate([cos] * reps, axis=1) if reps > 1 else cos
    sinf = jnp.concatenate([sin] * reps, axis=1) if reps > 1 else sin
    return t * cosf + rot * sinf


def _in_even_kernel(grp_ref, rblk_ref, x_ref, mod_ref, g_ref, w_ref, cos_ref, sin_ref,
                    q_ref, k_ref, ks_ref, v_ref, vs_ref, gb_ref, z_ref):
    h = _rms_mod(x_ref[...], g_ref[...], mod_ref[1:2, :], mod_ref[0:1, :])
    p = jnp.dot(h.astype(BF16), w_ref[...], preferred_element_type=F32)
    cos = cos_ref[...]
    sin = sin_ref[...]
    q = _rope_apply(p[:, 0:512], cos, sin) * (HEAD_DIM ** -0.5)
    k = _rope_apply(p[:, 512:640], cos, sin)
    v = p[:, 640:768]
    q_ref[...] = q.astype(BF16)
    k_ref[...] = k.astype(BF16)
    ks_ref[...] = pltpu.roll(k, 64, 1).astype(BF16)
    v_ref[...] = v.astype(BF16)
    vs_ref[...] = pltpu.roll(v, 64, 1).astype(BF16)
    gb_ref[...] = p[:, 768:1280]
    z_ref[...] = p[:, 1280:1792] * p[:, 1792:2304]


def _in_odd_kernel(grp_ref, x_ref, mod_ref, g_ref, w_ref, q_ref, k_ref, v_ref, f_ref):
    h = _rms_mod(x_ref[...], g_ref[...], mod_ref[1:2, :], mod_ref[0:1, :])
    p = jnp.dot(h.astype(BF16), w_ref[...], preferred_element_type=F32)
    q_ref[...] = (p[:, 0:512] * (HEAD_DIM ** -0.5)).astype(BF16)
    k_ref[...] = p[:, 512:1024].astype(BF16)
    v_ref[...] = p[:, 1024:1536].astype(BF16)
    f_ref[...] = p[:, 1536:2048].astype(BF16)


def _tile_tables():
    t = np.arange(NT_ALL)
    main = t < NT_MAIN
    grp = np.where(main, t // TILES_PER_SEQ, BATCH).astype(np.int32)
    rblk = np.where(main, t % TILES_PER_SEQ, TILES_PER_SEQ).astype(np.int32)
    first = np.where(main, (t % TILES_PER_SEQ) == 0, True).astype(np.int32)
    last = np.where(main, (t % TILES_PER_SEQ) == TILES_PER_SEQ - 1, True).astype(np.int32)
    return grp, rblk, first, last


def _rope_tables():
    t = jnp.arange(SEQ, dtype=I32)
    row = (t // GRID_W).astype(F32)
    col = (t % GRID_W).astype(F32)
    n_freq = HEAD_DIM // 4
    inv_freq = jnp.power(ROPE_THETA, -jnp.arange(n_freq, dtype=F32) / n_freq)
    ang = jnp.concatenate([row[:, None] * inv_freq, col[:, None] * inv_freq], axis=-1)
    cos = jnp.cos(ang)
    sin = jnp.sin(ang)
    cos_f = jnp.concatenate([cos, cos, cos, cos], axis=1)
    sin_f = jnp.concatenate([-sin, sin, -sin, sin], axis=1)
    cos_f = jnp.concatenate([cos_f, jnp.ones((TM, 128), F32)], axis=0)
    sin_f = jnp.concatenate([sin_f, jnp.zeros((TM, 128), F32)], axis=0)
    return cos_f, sin_f


def _in_even(xall, mod, g, w_bf, cos_f, sin_f):
    grp, rblk, _, _ = _tile_tables()
    row = lambda n, dt: jax.ShapeDtypeStruct((N_ALL, n), dt)
    tile = lambda n: pl.BlockSpec((TM, n), lambda i, grp, rb: (i, 0))
    return pl.pallas_call(
        _in_even_kernel,
        out_shape=(row(512, BF16), row(128, BF16), row(128, BF16), row(128, BF16), row(128, BF16),
                   row(512, F32), row(512, F32)),
        grid_spec=pltpu.PrefetchScalarGridSpec(
            num_scalar_prefetch=2,
            grid=(NT_ALL,),
            in_specs=[tile(D),
                      pl.BlockSpec((None, 6, D), lambda i, grp, rb: (grp[i], 0, 0)),
                      pl.BlockSpec((1, D), lambda i, grp, rb: (0, 0)),
                      pl.BlockSpec((D, 2304), lambda i, grp, rb: (0, 0)),
                      pl.BlockSpec((TM, 128), lambda i, grp, rb: (rb[i], 0)),
                      pl.BlockSpec((TM, 128), lambda i, grp, rb: (rb[i], 0))],
            out_specs=(tile(512), tile(128), tile(128), tile(128), tile(128), tile(512), tile(512)),
        ),
        compiler_params=_cparams(("arbitrary",)),
        name="in_proj_even",
    )(jnp.asarray(grp), jnp.asarray(rblk), xall, mod, g, w_bf, cos_f, sin_f)


def _in_odd(xall, mod, g, w_bf):
    grp, _, _, _ = _tile_tables()
    row = lambda n, dt: jax.ShapeDtypeStruct((N_ALL, n), dt)
    tile = lambda n: pl.BlockSpec((TM, n), lambda i, grp: (i, 0))
    return pl.pallas_call(
        _in_odd_kernel,
        out_shape=(row(512, BF16), row(512, BF16), row(512, BF16), row(512, BF16)),
        grid_spec=pltpu.PrefetchScalarGridSpec(
            num_scalar_prefetch=1,
            grid=(NT_ALL,),
            in_specs=[tile(D),
                      pl.BlockSpec((None, 6, D), lambda i, grp: (grp[i], 0, 0)),
                      pl.BlockSpec((1, D), lambda i, grp: (0, 0)),
                      pl.BlockSpec((D, 2048), lambda i, grp: (0, 0))],
            out_specs=(tile(512), tile(512), tile(512), tile(512)),
        ),
        compiler_params=_cparams(("arbitrary",)),
        name="in_proj_odd",
    )(jnp.asarray(grp), xall, mod, g, w_bf)


def _nt(a, b):
    return lax.dot_general(a, b, (((1,), (1,)), ((), ())), preferred_element_type=F32)


def _half_mask(shape, half):
    lane = lax.broadcasted_iota(I32, shape, 1)
    return (lane < 64) if half == 0 else (lane >= 64)


def _win_kernel(sink_ref, q_ref, k_ref, ks_ref, v_ref, vs_ref, kc_ref, ksc_ref, vc_ref, vsc_ref, o_ref):
    n = pl.program_id(1)
    start = pl.multiple_of(jnp.clip((n - 1) * 128, 0, SEQ - 384), 128)
    win = pl.ds(start, 384)
    row = lax.broadcasted_iota(I32, (128, 384), 0)
    col = lax.broadcasted_iota(I32, (128, 384), 1)
    valid = jnp.abs((n * 128 + row) - (start + col)) <= 128
    kk = (jnp.concatenate([k_ref[win, :], kc_ref[...]], axis=0),
          jnp.concatenate([ks_ref[win, :], ksc_ref[...]], axis=0))
    vv = (jnp.concatenate([v_ref[win, :], vc_ref[...]], axis=0),
          jnp.concatenate([vs_ref[win, :], vsc_ref[...]], axis=0))
    for c in range(4):
        qc = q_ref[:, c * 128:(c + 1) * 128]
        halves = []
        for hf in range(2):
            h = 2 * c + hf
            swapped = 0 if (h // 4) == hf else 1
            qm = jnp.where(_half_mask(qc.shape, hf), qc, jnp.zeros_like(qc))
            s = _nt(qm, kk[swapped])
            s_loc = jnp.where(valid, s[:, :384], -jnp.inf)
            s_ctx = s[:, 384:]
            sink = sink_ref[h]
            m = jnp.maximum(jnp.maximum(jnp.max(s_loc, axis=1, keepdims=True),
                                        jnp.max(s_ctx, axis=1, keepdims=True)), sink)
            p_loc = jnp.exp(s_loc - m)
            p_ctx = jnp.exp(s_ctx - m)
            den = (jnp.sum(p_loc, axis=1, keepdims=True) + jnp.sum(p_ctx, axis=1, keepdims=True)
                   + jnp.exp(sink - m))
            p = jnp.concatenate([p_loc, p_ctx], axis=1).astype(BF16)
            halves.append(jnp.dot(p, vv[swapped], preferred_element_type=F32) / den)
        o_ref[:, c * 128:(c + 1) * 128] = jnp.where(_half_mask(halves[0].shape, 0),
                                                    halves[0], halves[1]).astype(BF16)


def _win_attn(sink, q, k, ks, v, vs):
    nb = SEQ // 128
    seq_spec = pl.BlockSpec((SEQ, 128), lambda b, n: (b, 0))
    ctx_spec = pl.BlockSpec((CTX, 128), lambda b, n: (N_MAIN // CTX + b, 0))
    return pl.pallas_call(
        _win_kernel,
        out_shape=jax.ShapeDtypeStruct((N_MAIN, 512), BF16),
        grid=(BATCH, nb),
        in_specs=[pl.BlockSpec(memory_space=pltpu.SMEM),
                  pl.BlockSpec((128, 512), lambda b, n: (b * (SEQ // 128) + n, 0)),
                  seq_spec, seq_spec, seq_spec, seq_spec,
                  ctx_spec, ctx_spec, ctx_spec, ctx_spec],
        out_specs=pl.BlockSpec((128, 512), lambda b, n: (b * (SEQ // 128) + n, 0)),
        compiler_params=_cparams(("arbitrary", "arbitrary")),
        name="window_attn",
    )(sink, q, k, ks, v, vs, k, ks, v, vs)


def _ctx_attn_kernel(sink_ref, q_ref, k_ref, ks_ref, v_ref, vs_ref, o_ref):
    kk = (k_ref[...], ks_ref[...])
    vv = (v_ref[...], vs_ref[...])
    for c in range(4):
        qc = q_ref[:, c * 128:(c + 1) * 128]
        halves = []
        for hf in range(2):
            h = 2 * c + hf
            swapped = 0 if (h // 4) == hf else 1
            qm = jnp.where(_half_mask(qc.shape, hf), qc, jnp.zeros_like(qc))
            s = _nt(qm, kk[swapped])
            sink = sink_ref[h]
            m = jnp.maximum(jnp.max(s, axis=1, keepdims=True), sink)
            p = jnp.exp(s - m)
            den = jnp.sum(p, axis=1, keepdims=True) + jnp.exp(sink - m)
            halves.append(jnp.dot(p.astype(BF16), vv[swapped], preferred_element_type=F32) / den)
        o_ref[:, c * 128:(c + 1) * 128] = jnp.where(_half_mask(halves[0].shape, 0),
                                                    halves[0], halves[1]).astype(BF16)


def _ctx_attn(sink, q, k, ks, v, vs):
    ctx_spec = lambda n: pl.BlockSpec((CTX, n), lambda b: (N_MAIN // CTX + b, 0))
    return pl.pallas_call(
        _ctx_attn_kernel,
        out_shape=jax.ShapeDtypeStruct((N_CTX, 512), BF16),
        grid=(BATCH,),
        in_specs=[pl.BlockSpec(memory_space=pltpu.SMEM),
                  ctx_spec(512), ctx_spec(128), ctx_spec(128), ctx_spec(128), ctx_spec(128)],
        out_specs=pl.BlockSpec((CTX, 512), lambda b: (b, 0)),
        compiler_params=_cparams(("arbitrary",)),
        name="context_attn",
    )(sink, q, k, ks, v, vs)


NA_GROUP = 8
N_GRID_ROWS = SEQ // GRID_W


def _na_kernel(q_ref, k_ref, v_ref, kc_ref, vc_ref, nb_ref, o_ref):
    g = pl.program_id(1)

    def body(i, carry):
        r = g * NA_GROUP + i
        r0 = jnp.clip(r - NA_ROWS // 2, 0, N_GRID_ROWS - NA_ROWS)
        shift = r0 - r + NA_ROWS - 1
        qrows = pl.ds(pl.multiple_of(i * GRID_W, GRID_W), GRID_W)
        krows = pl.ds(pl.multiple_of(r0 * GRID_W, GRID_W), NA_ROWS * GRID_W)
        for c in range(4):
            lanes = slice(c * 128, (c + 1) * 128)
            qc = q_ref[qrows, lanes]
            kl = k_ref[krows, lanes]
            vl = v_ref[krows, lanes]
            kx = kc_ref[:, lanes]
            vx = vc_ref[:, lanes]
            halves = []
            for hf in range(2):
                h = 2 * c + hf
                qm = jnp.where(_half_mask(qc.shape, hf), qc, jnp.zeros_like(qc))
                s_loc = _nt(qm, kl) + nb_ref[h, shift]
                s_ctx = _nt(qm, kx)
                m = jnp.maximum(jnp.max(s_loc, axis=1, keepdims=True),
                                jnp.max(s_ctx, axis=1, keepdims=True))
                p_loc = jnp.exp(s_loc - m)
                p_ctx = jnp.exp(s_ctx - m)
                den = jnp.sum(p_loc, axis=1, keepdims=True) + jnp.sum(p_ctx, axis=1, keepdims=True)
                o = (jnp.dot(p_loc.astype(BF16), vl, preferred_element_type=F32)
                     + jnp.dot(p_ctx.astype(BF16), vx, preferred_element_type=F32))
                halves.append(o / den)
            o_ref[qrows, lanes] = jnp.where(_half_mask(halves[0].shape, 0),
                                            halves[0], halves[1]).astype(BF16)
        return carry

    lax.fori_loop(0, NA_GROUP, body, 0, unroll=2)


def _na_bias(rpb):
    col = np.arange(GRID_W)
    c0 = np.clip(col - NA_COLS // 2, 0, GRID_W - NA_COLS)
    col_ok = (col[None, :] >= c0[:, None]) & (col[None, :] < c0[:, None] + NA_COLS)
    dc = np.clip(col[None, :] - col[:, None] + NA_COLS - 1, 0, 2 * NA_COLS - 2)
    onehot = (dc[None] == np.arange(2 * NA_COLS - 1)[:, None, None]).astype(np.float32)
    e = jnp.einsum('hrd,dqk->hrqk', rpb.astype(F32), jnp.asarray(onehot),
                   precision=lax.Precision.HIGHEST)
    e = jnp.where(col_ok[None, None], e, -jnp.inf)
    b = jnp.stack([e[:, s:s + NA_ROWS] for s in range(NA_ROWS)], axis=1)
    b = jnp.transpose(b, (0, 1, 3, 2, 4))
    return b.reshape(rpb.shape[0], NA_ROWS, GRID_W, NA_ROWS * GRID_W)


def _na_attn(q, k, v, nb):
    qrows = NA_GROUP * GRID_W
    n_g = SEQ // qrows
    seq_spec = pl.BlockSpec((SEQ, 512), lambda b, g: (b, 0))
    ctx_spec = pl.BlockSpec((CTX, 512), lambda b, g: (N_MAIN // CTX + b, 0))
    return pl.pallas_call(
        _na_kernel,
        out_shape=jax.ShapeDtypeStruct((N_MAIN, 512), BF16),
        grid=(BATCH, n_g),
        in_specs=[pl.BlockSpec((qrows, 512), lambda b, g: (b * n_g + g, 0)),
                  seq_spec, seq_spec, ctx_spec, ctx_spec,
                  pl.BlockSpec(nb.shape, lambda b, g: (0, 0, 0, 0))],
        out_specs=pl.BlockSpec((qrows, 512), lambda b, g: (b * n_g + g, 0)),
        compiler_params=_cparams(("arbitrary", "arbitrary")),
        name="neighborhood_attn",
    )(q, k, v, k, v, nb)


F_N2_CHUNK = 8
F_K1_CHUNK = 8


def _four1_kernel(x_ref, w_ref, t_ref):
    w = w_ref[...]
    for j in range(F_N2_CHUNK):
        res = jnp.dot(w, x_ref[:, j * 512:(j + 1) * 512], preferred_element_type=F32)
        t_ref[0, j] = res[:64].astype(BF16)
        t_ref[1, j] = res[64:].astype(BF16)


def _four2_kernel(t_ref, m_ref, cs_ref, y_ref):
    cs = cs_ref[...]
    for j in range(F_K1_CHUNK):
        lanes = slice(j * 512, (j + 1) * 512)
        tt = jnp.concatenate([t_ref[0, :, lanes], t_ref[1, :, lanes]], axis=0)
        pp = jnp.dot(m_ref[j], tt, preferred_element_type=F32)
        pc = jnp.concatenate([pp[:64], pp[64:]], axis=1).astype(BF16)
        y_ref[:, lanes] = jnp.dot(pc, cs, preferred_element_type=F32).astype(BF16)


def _fourier_tables():
    a = np.arange(64)
    ang1 = 2.0 * np.pi * np.outer(a, a) / 64.0
    w1 = np.concatenate([np.cos(ang1), -np.sin(ang1)], axis=0)
    k1 = a[:, None, None]
    k2 = a[None, :, None]
    n2 = a[None, None, :]
    theta = 2.0 * np.pi * (n2 * k2 / 64.0 + n2 * k1 / 4096.0)
    mr = np.cos(theta) / 64.0
    mi = -np.sin(theta) / 64.0
    m = np.concatenate([np.concatenate([mr, -mi], axis=2),
                        np.concatenate([mi, mr], axis=2)], axis=1)
    c = np.arange(128)
    angc = 2.0 * np.pi * np.outer(c, c) / 128.0
    eye4 = np.eye(4)
    cc = np.kron(eye4, np.cos(angc)) / np.sqrt(128.0)
    sc = np.kron(eye4, np.sin(angc)) / np.sqrt(128.0)
    cs = np.concatenate([cc, sc], axis=0)
    return tuple(jnp.asarray(t, F32).astype(BF16) for t in (w1, m, cs))


def _fourier(f):
    w1, m, cs = _fourier_tables()
    fv = f.reshape(N_ALL // 64, 64 * 512)
    n_c = 64 // F_N2_CHUNK
    t = pl.pallas_call(
        _four1_kernel,
        out_shape=jax.ShapeDtypeStruct((BATCH, 2, 64, 64, 512), BF16),
        grid=(BATCH, n_c),
        in_specs=[pl.BlockSpec((64, F_N2_CHUNK * 512), lambda b, c: (b, c)),
                  pl.BlockSpec((128, 64), lambda b, c: (0, 0))],
        out_specs=pl.BlockSpec((None, 2, F_N2_CHUNK, 64, 512), lambda b, c: (b, 0, c, 0, 0)),
        compiler_params=_cparams(("arbitrary", "arbitrary")),
        name="fourier_rows",
    )(fv, w1)
    n_k = 64 // F_K1_CHUNK
    t2 = t.reshape(BATCH, 2, 64, 64 * 512)
    y = pl.pallas_call(
        _four2_kernel,
        out_shape=jax.ShapeDtypeStruct((BATCH * 64, 64 * 512), BF16),
        grid=(BATCH, n_k),
        in_specs=[pl.BlockSpec((None, 2, 64, F_K1_CHUNK * 512), lambda b, c: (b, 0, 0, c)),
                  pl.BlockSpec((F_K1_CHUNK, 128, 128), lambda b, c: (c, 0, 0)),
                  pl.BlockSpec((1024, 512), lambda b, c: (0, 0))],
        out_specs=pl.BlockSpec((64, F_K1_CHUNK * 512), lambda b, c: (b, c)),
        compiler_params=_cparams(("arbitrary", "arbitrary")),
        name="fourier_cols",
    )(t2, m, cs)
    return y.reshape(N_MAIN, 512)


def _route(h2, rwt_ref, rb_ref, carry_ref, te_ref, tw_ref, rk_ref, cnt_ref):
    logits = lax.dot_general(rwt_ref[...], h2, (((1,), (1,)), ((), ())),
                             preferred_element_type=F32,
                             precision=lax.Precision.HIGHEST) + rb_ref[...]
    eidx = lax.broadcasted_iota(I32, logits.shape, 0)
    vals = logits
    sels, tops, idxs = [], [], []
    for _ in range(TOP_K):
        m = jnp.max(vals, axis=0, keepdims=True)
        idx = jnp.min(jnp.where(vals == m, eidx, N_EXPERTS), axis=0, keepdims=True)
        sel = eidx == idx
        sels.append(sel)
        tops.append(m)
        idxs.append(idx)
        vals = jnp.where(sel, -jnp.inf, vals)
    ex = [jnp.exp(t - tops[0]) for t in tops]
    den = ex[0] + ex[1] + ex[2] + ex[3]
    onehot = jnp.zeros(logits.shape, F32)
    for sel in sels:
        onehot = onehot + sel.astype(F32)
    r_i = lax.broadcasted_iota(I32, (TM, TM), 0)
    c_i = lax.broadcasted_iota(I32, (TM, TM), 1)
    upper = (r_i < c_i).astype(BF16)
    prefix = jnp.dot(onehot.astype(BF16), upper, preferred_element_type=F32)
    base = carry_ref[:, 0:1] + prefix
    for k in range(TOP_K):
        te_ref[k:k + 1, :] = idxs[k]
        tw_ref[k:k + 1, :] = ex[k] / den
        rk_ref[k:k + 1, :] = jnp.sum(jnp.where(sels[k], base, 0.0), axis=0, keepdims=True).astype(I32)
    new_carry = carry_ref[...] + jnp.sum(onehot, axis=1, keepdims=True)
    carry_ref[...] = new_carry
    cnt_ref[...] = new_carry


def _out_tail(i, x, y, mod_ref, g_ref, rwt_ref, rb_ref, carry_ref,
              xn_ref, h2_ref, te_ref, tw_ref, rk_ref, cnt_ref):
    @pl.when(i == 0)
    def _():
        carry_ref[...] = jnp.zeros_like(carry_ref)

    xn = x + mod_ref[2:3, :] * y
    xn_ref[...] = xn
    h2 = _rms_mod(xn, g_ref[...], mod_ref[4:5, :], mod_ref[3:4, :])
    h2_ref[...] = h2
    _route(h2, rwt_ref, rb_ref, carry_ref, te_ref, tw_ref, rk_ref, cnt_ref)


def _out_even_kernel(grp_ref, first_ref, last_ref,
                     x_ref, a_ref, gb_ref, z_ref, zp_ref, zn_ref, cw_ref, w_ref, mod_ref, g_ref,
                     rwt_ref, rb_ref,
                     xn_ref, h2_ref, te_ref, tw_ref, rk_ref, cnt_ref, carry_ref):
    i = pl.program_id(0)
    z = z_ref[...]
    zprev = jnp.where(first_ref[i] == 1, 0.0, zp_ref[7:8, :])
    znext = jnp.where(last_ref[i] == 1, 0.0, zn_ref[0:1, :])
    rid = lax.broadcasted_iota(I32, z.shape, 0)
    zm1 = jnp.where(rid == 0, zprev, pltpu.roll(z, 1, 0))
    zp1 = jnp.where(rid == TM - 1, znext, pltpu.roll(z, TM - 1, 0))
    conv = gb_ref[...] * (zm1 * cw_ref[0:1, :] + z * cw_ref[1:2, :] + zp1 * cw_ref[2:3, :])
    y = (jnp.dot(a_ref[...], w_ref[0:512, :], preferred_element_type=F32)
         + jnp.dot(conv.astype(BF16), w_ref[512:1024, :], preferred_element_type=F32))
    _out_tail(i, x_ref[...], y, mod_ref, g_ref, rwt_ref, rb_ref, carry_ref,
              xn_ref, h2_ref, te_ref, tw_ref, rk_ref, cnt_ref)


def _out_odd_kernel(grp_ref, x_ref, a_ref, f_ref, w_ref, mod_ref, g_ref, rwt_ref, rb_ref,
                    xn_ref, h2_ref, te_ref, tw_ref, rk_ref, cnt_ref, carry_ref):
    i = pl.program_id(0)
    y = (jnp.dot(a_ref[...], w_ref[0:512, :], preferred_element_type=F32)
         + jnp.dot(f_ref[...], w_ref[512:1024, :], preferred_element_type=F32))
    _out_tail(i, x_ref[...], y, mod_ref, g_ref, rwt_ref, rb_ref, carry_ref,
              xn_ref, h2_ref, te_ref, tw_ref, rk_ref, cnt_ref)


def _out_shapes(n_rows):
    return (jax.ShapeDtypeStruct((n_rows, D), F32), jax.ShapeDtypeStruct((n_rows, D), F32),
            jax.ShapeDtypeStruct((TOP_K, n_rows), I32), jax.ShapeDtypeStruct((TOP_K, n_rows), F32),
            jax.ShapeDtypeStruct((TOP_K, n_rows), I32), jax.ShapeDtypeStruct((N_EXPERTS, 128), F32))


def _out_even(xall, attn, gb, z, conv_w, w_bf, mod, g, rwt, rb):
    grp, _, first, last = _tile_tables()
    n_rows = N_ALL
    zblocks = n_rows // 8
    im = lambda f: (lambda i, grp, fi, la: f(i))
    tile = lambda n: pl.BlockSpec((TM, n), im(lambda i: (i, 0)))
    const = lambda shape: pl.BlockSpec(shape, im(lambda i: (0,) * len(shape)))
    tk = pl.BlockSpec((TOP_K, TM), im(lambda i: (0, i)))
    return pl.pallas_call(
        _out_even_kernel,
        out_shape=_out_shapes(n_rows),
        grid_spec=pltpu.PrefetchScalarGridSpec(
            num_scalar_prefetch=3,
            grid=(NT_ALL,),
            in_specs=[tile(D), tile(512), tile(512), tile(512),
                      pl.BlockSpec((8, 512), im(lambda i: (jnp.maximum(i * (TM // 8) - 1, 0), 0))),
                      pl.BlockSpec((8, 512), im(lambda i: (jnp.minimum((i + 1) * (TM // 8), zblocks - 1), 0))),
                      const((3, 512)), const((D, D)),
                      pl.BlockSpec((None, 6, D), lambda i, grp, fi, la: (grp[i], 0, 0)),
                      const((1, D)), const((N_EXPERTS, D)), const((N_EXPERTS, 1))],
            out_specs=(tile(D), tile(D), tk, tk, tk, const((N_EXPERTS, 128))),
            scratch_shapes=[pltpu.VMEM((N_EXPERTS, 128), F32)],
        ),
        compiler_params=_cparams(("arbitrary",)),
        name="out_proj_even",
    )(jnp.asarray(grp), jnp.asarray(first), jnp.asarray(last),
      xall, attn, gb, z, z, z, conv_w, w_bf, mod, g, rwt, rb)


def _out_odd(xall, attn, four, w_bf, mod, g, rwt, rb):
    grp, _, _, _ = _tile_tables()
    n_rows = N_MAIN
    im = lambda f: (lambda i, grp: f(i))
    tile = lambda n: pl.BlockSpec((TM, n), im(lambda i: (i, 0)))
    const = lambda shape: pl.BlockSpec(shape, im(lambda i: (0,) * len(shape)))
    tk = pl.BlockSpec((TOP_K, TM), im(lambda i: (0, i)))
    return pl.pallas_call(
        _out_odd_kernel,
        out_shape=_out_shapes(n_rows),
        grid_spec=pltpu.PrefetchScalarGridSpec(
            num_scalar_prefetch=1,
            grid=(NT_MAIN,),
            in_specs=[tile(D), tile(512), tile(512), const((D, D)),
                      pl.BlockSpec((None, 6, D), lambda i, grp: (grp[i], 0, 0)),
                      const((1, D)), const((N_EXPERTS, D)), const((N_EXPERTS, 1))],
            out_specs=(tile(D), tile(D), tk, tk, tk, const((N_EXPERTS, 128))),
            scratch_shapes=[pltpu.VMEM((N_EXPERTS, 128), F32)],
        ),
        compiler_params=_cparams(("arbitrary",)),
        name="out_proj_odd",
    )(jnp.asarray(grp), xall, attn, four, w_bf, mod, g, rwt, rb)


def _moe_plan(counts_f, top_e_t, rank_t, n_tok):
    counts = counts_f[:, 0].astype(I32)
    padded = (counts + TMM - 1) // TMM * TMM
    e_i = jnp.arange(N_EXPERTS, dtype=I32)
    incl = e_i[None, :] <= e_i[:, None]
    pad_end = jnp.sum(jnp.where(incl, padded[None, :], 0), axis=1)
    pad_start = pad_end - padded
    sel = top_e_t[None] == e_i[:, None, None]
    dest = jnp.sum(jnp.where(sel, pad_start[:, None, None], 0), axis=0) + rank_t
    n_blocks = n_tok * TOP_K // TMM + N_EXPERTS
    blk_start = jnp.arange(n_blocks, dtype=I32) * TMM
    block_e = jnp.minimum(jnp.sum((blk_start[:, None] >= pad_end[None, :]).astype(I32), axis=1),
                          N_EXPERTS - 1)
    r = jnp.arange(TMM, dtype=I32)[None, :]
    cend = pad_start + counts
    cend_b = jnp.sum(jnp.where(block_e[:, None] == e_i[None, :], cend[None, :], 0), axis=1)
    nvalid = jnp.clip(cend_b - blk_start, 0, TMM)
    b_i = jnp.arange(n_blocks, dtype=I32)
    spare_base = jnp.sum(jnp.where(b_i[None, :] < b_i[:, None], (TMM - nvalid)[None, :], 0), axis=1)
    n_rows = n_tok * TOP_K
    init = n_rows + spare_base[:, None] + r - nvalid[:, None]
    lead = n_rows + N_SPARE_ROWS + r
    table_init = jnp.concatenate([lead.reshape(-1), init.reshape(-1)])
    return dest, block_e, table_init


N_SPARE_ROWS = N_EXPERTS * TMM
INV_CHUNKS = 8


def _slot_table_kernel(init_hbm, dest_hbm, out_hbm, tab_s, d_s, sem):
    n_chunks, ch = dest_hbm.shape
    c0 = pltpu.make_async_copy(init_hbm, tab_s, sem)
    c0.start()
    c0.wait()

    def chunk(kc, carry):
        cp = pltpu.make_async_copy(dest_hbm.at[kc], d_s, sem)
        cp.start()
        cp.wait()
        k = kc // INV_CHUNKS
        t0 = (kc % INV_CHUNKS) * ch

        def body(j, c):
            tab_s[d_s[j] + TMM] = (t0 + j) * TOP_K + k
            return c

        lax.fori_loop(0, ch, body, 0, unroll=8)
        return carry

    lax.fori_loop(0, n_chunks, chunk, 0)
    c1 = pltpu.make_async_copy(tab_s, out_hbm, sem)
    c1.start()
    c1.wait()


def _slot_table(dest, table_init):
    n_tok = dest.shape[1]
    ch = n_tok // INV_CHUNKS
    n = table_init.shape[0]
    return pl.pallas_call(
        _slot_table_kernel,
        out_shape=jax.ShapeDtypeStruct((n,), I32),
        in_specs=[pl.BlockSpec(memory_space=pl.ANY), pl.BlockSpec(memory_space=pl.ANY)],
        out_specs=pl.BlockSpec(memory_space=pl.ANY),
        scratch_shapes=[pltpu.SMEM((n,), I32), pltpu.SMEM((ch,), I32), pltpu.SemaphoreType.DMA],
        name="moe_slot_table",
    )(table_init, dest.reshape(TOP_K * INV_CHUNKS, ch))


def _moe_kernel(n_rows, be_ref, sv_ref, h2_hbm, wgu_ref, bgu_ref, wdn_ref, bdn_ref, yk_hbm,
                wgu_bf, wdn_bf, xb0, xb1, yb0, yb1, sem_g, sem_s):
    i = pl.program_id(0)
    nb = pl.num_programs(0)

    def gather_row(base, r, xb, sem):
        v = sv_ref[base + r]
        tok = jnp.where(v < n_rows, v >> 2, 0)
        return pltpu.make_async_copy(h2_hbm.at[pl.ds(tok, 1)], xb.at[pl.ds(r, 1)], sem)

    def scatter_row(base, r, yb, sem):
        return pltpu.make_async_copy(yb.at[pl.ds(r, 1)], yk_hbm.at[pl.ds(sv_ref[base + r], 1)], sem)

    def wait_gather(sem):
        pltpu.make_async_copy(h2_hbm.at[pl.ds(0, TMM)], xb0, sem).wait()

    def wait_scatter(sem):
        pltpu.make_async_copy(yb0, yk_hbm.at[pl.ds(0, TMM)], sem).wait()

    @pl.when(i == 0)
    def _():
        yb1[...] = jnp.zeros_like(yb1)

        def first(r, c):
            gather_row(TMM, r, xb0, sem_g.at[0]).start()
            return c

        lax.fori_loop(0, TMM, first, 0, unroll=8)

    @pl.when(i >= 1)
    def _():
        wait_scatter(sem_s.at[i % 2])

    @pl.when((i == 0) | (be_ref[i] != be_ref[jnp.maximum(i - 1, 0)]))
    def _():
        wgu_bf[...] = wgu_ref[...].astype(BF16)
        wdn_bf[...] = wdn_ref[...].astype(BF16)

    def step(p, xb_cur, xb_nxt, yb_cur, yb_prv):
        wait_gather(sem_g.at[p])
        xb = xb_cur[...].astype(BF16)
        nxt = (jnp.minimum(i + 1, nb - 1) + 1) * TMM
        prv = i * TMM
        for r in range(TMM):
            gather_row(nxt, r, xb_nxt, sem_g.at[1 - p]).start()
            scatter_row(prv, r, yb_prv, sem_s.at[1 - p]).start(priority=r % 2)
        gu = jnp.dot(xb, wgu_bf[...], preferred_element_type=F32) + bgu_ref[...]
        gate = jnp.minimum(gu[:, :D], SWIGLU_LIMIT)
        up = jnp.clip(gu[:, D:], -SWIGLU_LIMIT, SWIGLU_LIMIT)
        act = (up + 1.0) * (gate * (1.0 / (1.0 + jnp.exp(-SWIGLU_ALPHA * gate))))
        yb_cur[...] = jnp.dot(act.astype(BF16), wdn_bf[...], preferred_element_type=F32) + bdn_ref[...]

    @pl.when(i % 2 == 0)
    def _():
        step(0, xb0, xb1, yb0, yb1)

    @pl.when(i % 2 == 1)
    def _():
        step(1, xb1, xb0, yb1, yb0)

    n_blocks = be_ref.shape[0]
    p_last = (n_blocks - 1) % 2

    @pl.when(i == nb - 1)
    def _():
        yb_last = yb1 if p_last else yb0

        def last(r, c):
            scatter_row(n_blocks * TMM, r, yb_last, sem_s.at[p_last]).start()
            return c

        lax.fori_loop(0, TMM, last, 0, unroll=8)
        wait_scatter(sem_s.at[1 - p_last])
        wait_scatter(sem_s.at[p_last])
        wait_gather(sem_g.at[1 - p_last])


def _moe(layer, h2, block_e, slot_tab, w_gu, b_gu, w_dn, b_dn):
    n_blocks = block_e.shape[0]
    n_l = w_gu.shape[0]
    n_rows = h2.shape[0] * TOP_K
    exp4 = lambda i, be, sv: (layer, be[i], 0, 0)
    return pl.pallas_call(
        functools.partial(_moe_kernel, n_rows),
        out_shape=jax.ShapeDtypeStruct((n_rows + N_SPARE_ROWS + TMM, D), F32),
        grid_spec=pltpu.PrefetchScalarGridSpec(
            num_scalar_prefetch=2,
            grid=(n_blocks,),
            in_specs=[pl.BlockSpec(memory_space=pl.ANY),
                      pl.BlockSpec((None, None, D, 2 * D), exp4),
                      pl.BlockSpec((None, None, 1, 2 * D), exp4),
                      pl.BlockSpec((None, None, D, D), exp4),
                      pl.BlockSpec((None, None, 1, D), exp4)],
            out_specs=pl.BlockSpec(memory_space=pl.ANY),
            scratch_shapes=[pltpu.VMEM((D, 2 * D), BF16), pltpu.VMEM((D, D), BF16),
                            pltpu.VMEM((TMM, D), F32), pltpu.VMEM((TMM, D), F32),
                            pltpu.VMEM((TMM, D), F32), pltpu.VMEM((TMM, D), F32),
                            pltpu.SemaphoreType.DMA((2,)), pltpu.SemaphoreType.DMA((2,))],
        ),
        compiler_params=_cparams(("arbitrary",)),
        name="moe_experts",
    )(block_e, slot_tab, h2, w_gu, b_gu.reshape(n_l, N_EXPERTS, 1, 2 * D), w_dn,
      b_dn.reshape(n_l, N_EXPERTS, 1, D))


N_LANE_CHUNKS = D // 128


def _combine_kernel(final, grp_ref, *refs):
    yk_refs = refs[:N_LANE_CHUNKS]
    x_ref, tw_ref, mod_ref, fn_ref, o_ref = refs[N_LANE_CHUNKS:]
    tw = tw_ref[...]
    w = [tw[:, k:k + 1] for k in range(TOP_K)]
    ssq = jnp.zeros((TT, 1), F32)
    for c in range(N_LANE_CHUNKS):
        lanes = slice(c * 128, (c + 1) * 128)
        acc = w[0] * yk_refs[c][pl.ds(0, TT, stride=TOP_K), :]
        for k in range(1, TOP_K):
            acc = acc + w[k] * yk_refs[c][pl.ds(k, TT, stride=TOP_K), :]
        out = x_ref[:, lanes] + mod_ref[5:6, lanes] * acc
        o_ref[:, lanes] = out
        ssq = ssq + jnp.sum(out * out, axis=-1, keepdims=True)
    if final:
        o_ref[...] = o_ref[...] * lax.rsqrt(ssq * (1.0 / D) + EPS) * fn_ref[...]


def _combine(xn, yk, top_w_t, mod, final_norm, n_tok, final):
    grp, _, _, _ = _tile_tables()
    n_tiles = n_tok // TT
    tw = top_w_t.T
    return pl.pallas_call(
        functools.partial(_combine_kernel, final),
        out_shape=jax.ShapeDtypeStruct((n_tok, D), F32),
        grid_spec=pltpu.PrefetchScalarGridSpec(
            num_scalar_prefetch=1,
            grid=(n_tiles,),
            in_specs=[pl.BlockSpec((TOP_K * TT, 128), functools.partial(lambda c, i, grp: (i, c), c))
                      for c in range(N_LANE_CHUNKS)] + [
                      pl.BlockSpec((TT, D), lambda i, grp: (i, 0)),
                      pl.BlockSpec((TT, TOP_K), lambda i, grp: (i, 0)),
                      pl.BlockSpec((None, 6, D), lambda i, grp: (grp[i], 0, 0)),
                      pl.BlockSpec((1, D), lambda i, grp: (0, 0))],
            out_specs=pl.BlockSpec((TT, D), lambda i, grp: (i, 0)),
        ),
        compiler_params=_cparams(("arbitrary",)),
        name="moe_combine",
    )(jnp.asarray(grp), *([yk] * N_LANE_CHUNKS), xn, tw, mod, final_norm.reshape(1, D))


def _moe_layer(layer, xn, h2, top_e_t, top_w_t, rank_t, counts, mod, w_gu, b_gu, w_dn, b_dn, final_norm, final):
    n_tok = xn.shape[0]
    dest, block_e, table_init = _moe_plan(counts, top_e_t, rank_t, n_tok)
    slot_tab = _slot_table(dest, table_init)
    yk = _moe(layer, h2, block_e, slot_tab, w_gu, b_gu, w_dn, b_dn)
    return _combine(xn, yk, top_w_t, mod, final_norm, n_tok, final)


def kernel(x, c, ctx, c_ctx, ada_w, ada_b, norm_mix, norm_ffn, even_w_in, even_w_out, even_conv_w, even_sink, odd_w_in, odd_w_out, odd_rpb, router_w, router_b, moe_w_gu, moe_b_gu, moe_w_dn, moe_b_dn, final_norm):
    xall = jnp.concatenate([x.reshape(N_MAIN, D), ctx.reshape(N_CTX, D)], axis=0)
    cc = jnp.concatenate([c, c_ctx[None, :], jnp.zeros((3, D), F32)], axis=0)
    mod = _ada(cc, ada_w, ada_b).reshape(2, 8, 6, D)
    cos_f, sin_f = _rope_tables()

    q, k, ks, v, vs, gb, z = _in_even(xall, mod[0], norm_mix[0:1], even_w_in[0].astype(BF16), cos_f, sin_f)
    attn = jnp.concatenate([_win_attn(even_sink[0], q, k, ks, v, vs),
                            _ctx_attn(even_sink[0], q, k, ks, v, vs)], axis=0)
    xn, h2, te, tw, rk, cnt = _out_even(xall, attn, gb, z, even_conv_w[0], even_w_out[0].astype(BF16),
                                        mod[0], norm_ffn[0:1], router_w[0].T, router_b[0][:, None])
    xall = _moe_layer(0, xn, h2, te, tw, rk, cnt, mod[0], moe_w_gu, moe_b_gu, moe_w_dn, moe_b_dn,
                      final_norm, False)

    q, k, v, f = _in_odd(xall, mod[1], norm_mix[1:2], odd_w_in[0].astype(BF16))
    attn = _na_attn(q, k, v, _na_bias(odd_rpb[0]))
    four = _fourier(f)
    xn, h2, te, tw, rk, cnt = _out_odd(xall, attn, four, odd_w_out[0].astype(BF16),
                                       mod[1], norm_ffn[1:2], router_w[1].T, router_b[1][:, None])
    out = _moe_layer(1, xn, h2, te, tw, rk, cnt, mod[1], moe_w_gu, moe_b_gu, moe_w_dn, moe_b_dn,
                     final_norm, True)
    return out.reshape(BATCH, SEQ, D)
```

```python
import functools

import numpy as np
import jax
import jax.numpy as jnp
from jax import lax
from jax.experimental import pallas as pl
from jax.experimental.pallas import tpu as pltpu

F32 = jnp.float32
BF16 = jnp.bfloat16
I32 = jnp.int32

D = 1024
BATCH = 4
SEQ = 4096
CTX = 256
GRID_W = 64
HEAD_DIM = 64
EPS = 1e-6
ROPE_THETA = 10000.0
N_EXPERTS = 32
TOP_K = 4
SWIGLU_LIMIT = 7.0
SWIGLU_ALPHA = 1.702
NA_ROWS = 8
NA_COLS = 16

N_MAIN = BATCH * SEQ
N_CTX = BATCH * CTX
N_ALL = N_MAIN + N_CTX
TM = 256
NT_MAIN = N_MAIN // TM
NT_ALL = N_ALL // TM
TILES_PER_SEQ = SEQ // TM
TMM = 256
TT = 256
VMEM_LIMIT = 56 * 1024 * 1024


def _cparams(sem, vmem=VMEM_LIMIT):
    return pltpu.CompilerParams(dimension_semantics=sem, vmem_limit_bytes=vmem)


def _rms_mod(x, g, sc, sh):
    ms = jnp.mean(x * x, axis=-1, keepdims=True)
    return (x * lax.rsqrt(ms + EPS) * g) * (1.0 + sc) + sh


def _ada_kernel(c_ref, w_ref, b_ref, o_ref):
    c = c_ref[...]
    s = c * (1.0 / (1.0 + jnp.exp(-c)))
    o_ref[...] = jnp.dot(s, w_ref[...], preferred_element_type=F32,
                         precision=lax.Precision.HIGHEST) + b_ref[...]


def _ada(cc, ada_w, ada_b):
    n_l = ada_w.shape[0]
    tn = 1024
    return pl.pallas_call(
        _ada_kernel,
        out_shape=jax.ShapeDtypeStruct((n_l, 8, 6 * D), F32),
        grid=(n_l, 6 * D // tn),
        in_specs=[pl.BlockSpec((8, D), lambda l, j: (0, 0)),
                  pl.BlockSpec((None, D, tn), lambda l, j: (l, 0, j)),
                  pl.BlockSpec((None, 1, tn), lambda l, j: (l, 0, j))],
        out_specs=pl.BlockSpec((None, 8, tn), lambda l, j: (l, 0, j)),
        compiler_params=_cparams(("arbitrary", "arbitrary")),
        name="ada_mod",
    )(cc, ada_w, ada_b.reshape(n_l, 1, 6 * D))


def _rope_apply(t, cos, sin):
    n = t.shape[1]
    lane = lax.broadcasted_iota(I32, t.shape, 1)
    fwd = pltpu.roll(t, n - 32, 1)
    bwd = pltpu.roll(t, 32, 1)
    rot = jnp.where((lane % 64) < 32, fwd, bwd)
    reps = n // 128
    cosf = jnp.concatenate([cos] * reps, axis=1) if reps > 1 else cos
    sinf = jnp.concatenate([sin] * reps, axis=1) if reps > 1 else sin
    return t * cosf + rot * sinf


def _in_even_kernel(grp_ref, rblk_ref, x_ref, mod_ref, g_ref, w_ref, cos_ref, sin_ref,
                    q_ref, k_ref, ks_ref, v_ref, vs_ref, gb_ref, z_ref):
    h = _rms_mod(x_ref[...], g_ref[...], mod_ref[1:2, :], mod_ref[0:1, :])
    p = jnp.dot(h.astype(BF16), w_ref[...], preferred_element_type=F32)
    cos = cos_ref[...]
    sin = sin_ref[...]
    q = _rope_apply(p[:, 0:512], cos, sin) * (HEAD_DIM ** -0.5)
    k = _rope_apply(p[:, 512:640], cos, sin)
    v = p[:, 640:768]
    q_ref[...] = q.astype(BF16)
    k_ref[...] = k.astype(BF16)
    ks_ref[...] = pltpu.roll(k, 64, 1).astype(BF16)
    v_ref[...] = v.astype(BF16)
    vs_ref[...] = pltpu.roll(v, 64, 1).astype(BF16)
    gb_ref[...] = p[:, 768:1280]
    z_ref[...] = p[:, 1280:1792] * p[:, 1792:2304]


def _in_odd_kernel(grp_ref, x_ref, mod_ref, g_ref, w_ref, q_ref, k_ref, v_ref, f_ref):
    h = _rms_mod(x_ref[...], g_ref[...], mod_ref[1:2, :], mod_ref[0:1, :])
    p = jnp.dot(h.astype(BF16), w_ref[...], preferred_element_type=F32)
    q_ref[...] = (p[:, 0:512] * (HEAD_DIM ** -0.5)).astype(BF16)
    k_ref[...] = p[:, 512:1024].astype(BF16)
    v_ref[...] = p[:, 1024:1536].astype(BF16)
    f_ref[...] = p[:, 1536:2048].astype(BF16)


def _tile_tables():
    t = np.arange(NT_ALL)
    main = t < NT_MAIN
    grp = np.where(main, t // TILES_PER_SEQ, BATCH).astype(np.int32)
    rblk = np.where(main, t % TILES_PER_SEQ, TILES_PER_SEQ).astype(np.int32)
    first = np.where(main, (t % TILES_PER_SEQ) == 0, True).astype(np.int32)
    last = np.where(main, (t % TILES_PER_SEQ) == TILES_PER_SEQ - 1, True).astype(np.int32)
    return grp, rblk, first, last


def _rope_tables():
    t = jnp.arange(SEQ, dtype=I32)
    row = (t // GRID_W).astype(F32)
    col = (t % GRID_W).astype(F32)
    n_freq = HEAD_DIM // 4
    inv_freq = jnp.power(ROPE_THETA, -jnp.arange(n_freq, dtype=F32) / n_freq)
    ang = jnp.concatenate([row[:, None] * inv_freq, col[:, None] * inv_freq], axis=-1)
    cos = jnp.cos(ang)
    sin = jnp.sin(ang)
    cos_f = jnp.concatenate([cos, cos, cos, cos], axis=1)
    sin_f = jnp.concatenate([-sin, sin, -sin, sin], axis=1)
    cos_f = jnp.concatenate([cos_f, jnp.ones((TM, 128), F32)], axis=0)
    sin_f = jnp.concatenate([sin_f, jnp.zeros((TM, 128), F32)], axis=0)
    return cos_f, sin_f


def _in_even(xall, mod, g, w_bf, cos_f, sin_f):
    grp, rblk, _, _ = _tile_tables()
    row = lambda n, dt: jax.ShapeDtypeStruct((N_ALL, n), dt)
    tile = lambda n: pl.BlockSpec((TM, n), lambda i, grp, rb: (i, 0))
    return pl.pallas_call(
        _in_even_kernel,
        out_shape=(row(512, BF16), row(128, BF16), row(128, BF16), row(128, BF16), row(128, BF16),
                   row(512, F32), row(512, F32)),
        grid_spec=pltpu.PrefetchScalarGridSpec(
            num_scalar_prefetch=2,
            grid=(NT_ALL,),
            in_specs=[tile(D),
                      pl.BlockSpec((None, 6, D), lambda i, grp, rb: (grp[i], 0, 0)),
                      pl.BlockSpec((1, D), lambda i, grp, rb: (0, 0)),
                      pl.BlockSpec((D, 2304), lambda i, grp, rb: (0, 0)),
                      pl.BlockSpec((TM, 128), lambda i, grp, rb: (rb[i], 0)),
                      pl.BlockSpec((TM, 128), lambda i, grp, rb: (rb[i], 0))],
            out_specs=(tile(512), tile(128), tile(128), tile(128), tile(128), tile(512), tile(512)),
        ),
        compiler_params=_cparams(("arbitrary",)),
        name="in_proj_even",
    )(jnp.asarray(grp), jnp.asarray(rblk), xall, mod, g, w_bf, cos_f, sin_f)


def _in_odd(xall, mod, g, w_bf):
    grp, _, _, _ = _tile_tables()
    row = lambda n, dt: jax.ShapeDtypeStruct((N_ALL, n), dt)
    tile = lambda n: pl.BlockSpec((TM, n), lambda i, grp: (i, 0))
    return pl.pallas_call(
        _in_odd_kernel,
        out_shape=(row(512, BF16), row(512, BF16), row(512, BF16), row(512, BF16)),
        grid_spec=pltpu.PrefetchScalarGridSpec(
            num_scalar_prefetch=1,
            grid=(NT_ALL,),
            in_specs=[tile(D),
                      pl.BlockSpec((None, 6, D), lambda i, grp: (grp[i], 0, 0)),
                      pl.BlockSpec((1, D), lambda i, grp: (0, 0)),
                      pl.BlockSpec((D, 2048), lambda i, grp: (0, 0))],
            out_specs=(tile(512), tile(512), tile(512), tile(512)),
        ),
        compiler_params=_cparams(("arbitrary",)),
        name="in_proj_odd",
    )(jnp.asarray(grp), xall, mod, g, w_bf)


def _nt(a, b):
    return lax.dot_general(a, b, (((1,), (1,)), ((), ())), preferred_element_type=F32)


def _half_mask(shape, half):
    lane = lax.broadcasted_iota(I32, shape, 1)
    return (lane < 64) if half == 0 else (lane >= 64)


def _win_kernel(sink_ref, q_ref, k_ref, ks_ref, v_ref, vs_ref, kc_ref, ksc_ref, vc_ref, vsc_ref, o_ref):
    n = pl.program_id(1)
    start = pl.multiple_of(jnp.clip((n - 1) * 128, 0, SEQ - 384), 128)
    win = pl.ds(start, 384)
    row = lax.broadcasted_iota(I32, (128, 384), 0)
    col = lax.broadcasted_iota(I32, (128, 384), 1)
    valid = jnp.abs((n * 128 + row) - (start + col)) <= 128
    kk = (jnp.concatenate([k_ref[win, :], kc_ref[...]], axis=0),
          jnp.concatenate([ks_ref[win, :], ksc_ref[...]], axis=0))
    vv = (jnp.concatenate([v_ref[win, :], vc_ref[...]], axis=0),
          jnp.concatenate([vs_ref[win, :], vsc_ref[...]], axis=0))
    for c in range(4):
        qc = q_ref[:, c * 128:(c + 1) * 128]
        halves = []
        for hf in range(2):
            h = 2 * c + hf
            swapped = 0 if (h // 4) == hf else 1
            qm = jnp.where(_half_mask(qc.shape, hf), qc, jnp.zeros_like(qc))
            s = _nt(qm, kk[swapped])
            s_loc = jnp.where(valid, s[:, :384], -jnp.inf)
            s_ctx = s[:, 384:]
            sink = sink_ref[h]
            m = jnp.maximum(jnp.maximum(jnp.max(s_loc, axis=1, keepdims=True),
                                        jnp.max(s_ctx, axis=1, keepdims=True)), sink)
            p_loc = jnp.exp(s_loc - m)
            p_ctx = jnp.exp(s_ctx - m)
            den = (jnp.sum(p_loc, axis=1, keepdims=True) + jnp.sum(p_ctx, axis=1, keepdims=True)
                   + jnp.exp(sink - m))
            p = jnp.concatenate([p_loc, p_ctx], axis=1).astype(BF16)
            halves.append(jnp.dot(p, vv[swapped], preferred_element_type=F32) / den)
        o_ref[:, c * 128:(c + 1) * 128] = jnp.where(_half_mask(halves[0].shape, 0),
                                                    halves[0], halves[1]).astype(BF16)


def _win_attn(sink, q, k, ks, v, vs):
    nb = SEQ // 128
    seq_spec = pl.BlockSpec((SEQ, 128), lambda b, n: (b, 0))
    ctx_spec = pl.BlockSpec((CTX, 128), lambda b, n: (N_MAIN // CTX + b, 0))
    return pl.pallas_call(
        _win_kernel,
        out_shape=jax.ShapeDtypeStruct((N_MAIN, 512), BF16),
        grid=(BATCH, nb),
        in_specs=[pl.BlockSpec(memory_space=pltpu.SMEM),
                  pl.BlockSpec((128, 512), lambda b, n: (b * (SEQ // 128) + n, 0)),
                  seq_spec, seq_spec, seq_spec, seq_spec,
                  ctx_spec, ctx_spec, ctx_spec, ctx_spec],
        out_specs=pl.BlockSpec((128, 512), lambda b, n: (b * (SEQ // 128) + n, 0)),
        compiler_params=_cparams(("arbitrary", "arbitrary")),
        name="window_attn",
    )(sink, q, k, ks, v, vs, k, ks, v, vs)


def _ctx_attn_kernel(sink_ref, q_ref, k_ref, ks_ref, v_ref, vs_ref, o_ref):
    kk = (k_ref[...], ks_ref[...])
    vv = (v_ref[...], vs_ref[...])
    for c in range(4):
        qc = q_ref[:, c * 128:(c + 1) * 128]
        halves = []
        for hf in range(2):
            h = 2 * c + hf
            swapped = 0 if (h // 4) == hf else 1
            qm = jnp.where(_half_mask(qc.shape, hf), qc, jnp.zeros_like(qc))
            s = _nt(qm, kk[swapped])
            sink = sink_ref[h]
            m = jnp.maximum(jnp.max(s, axis=1, keepdims=True), sink)
            p = jnp.exp(s - m)
            den = jnp.sum(p, axis=1, keepdims=True) + jnp.exp(sink - m)
            halves.append(jnp.dot(p.astype(BF16), vv[swapped], preferred_element_type=F32) / den)
        o_ref[:, c * 128:(c + 1) * 128] = jnp.where(_half_mask(halves[0].shape, 0),
                                                    halves[0], halves[1]).astype(BF16)


def _ctx_attn(sink, q, k, ks, v, vs):
    ctx_spec = lambda n: pl.BlockSpec((CTX, n), lambda b: (N_MAIN // CTX + b, 0))
    return pl.pallas_call(
        _ctx_attn_kernel,
        out_shape=jax.ShapeDtypeStruct((N_CTX, 512), BF16),
        grid=(BATCH,),
        in_specs=[pl.BlockSpec(memory_space=pltpu.SMEM),
                  ctx_spec(512), ctx_spec(128), ctx_spec(128), ctx_spec(128), ctx_spec(128)],
        out_specs=pl.BlockSpec((CTX, 512), lambda b: (b, 0)),
        compiler_params=_cparams(("arbitrary",)),
        name="context_attn",
    )(sink, q, k, ks, v, vs)


NA_GROUP = 8
N_GRID_ROWS = SEQ // GRID_W


def _na_kernel(q_ref, k_ref, v_ref, kc_ref, vc_ref, nb_ref, o_ref):
    g = pl.program_id(1)

    def body(i, carry):
        r = g * NA_GROUP + i
        r0 = jnp.clip(r - NA_ROWS // 2, 0, N_GRID_ROWS - NA_ROWS)
        shift = r0 - r + NA_ROWS - 1
        qrows = pl.ds(pl.multiple_of(i * GRID_W, GRID_W), GRID_W)
        krows = pl.ds(pl.multiple_of(r0 * GRID_W, GRID_W), NA_ROWS * GRID_W)
        for c in range(4):
            lanes = slice(c * 128, (c + 1) * 128)
            qc = q_ref[qrows, lanes]
            kl = k_ref[krows, lanes]
            vl = v_ref[krows, lanes]
            kx = kc_ref[:, lanes]
            vx = vc_ref[:, lanes]
            halves = []
            for hf in range(2):
                h = 2 * c + hf
                qm = jnp.where(_half_mask(qc.shape, hf), qc, jnp.zeros_like(qc))
                s_loc = _nt(qm, kl) + nb_ref[h, shift]
                s_ctx = _nt(qm, kx)
                m = jnp.maximum(jnp.max(s_loc, axis=1, keepdims=True),
                                jnp.max(s_ctx, axis=1, keepdims=True))
                p_loc = jnp.exp(s_loc - m)
                p_ctx = jnp.exp(s_ctx - m)
                den = jnp.sum(p_loc, axis=1, keepdims=True) + jnp.sum(p_ctx, axis=1, keepdims=True)
                o = (jnp.dot(p_loc.astype(BF16), vl, preferred_element_type=F32)
                     + jnp.dot(p_ctx.astype(BF16), vx, preferred_element_type=F32))
                halves.append(o / den)
            o_ref[qrows, lanes] = jnp.where(_half_mask(halves[0].shape, 0),
                                            halves[0], halves[1]).astype(BF16)
        return carry

    lax.fori_loop(0, NA_GROUP, body, 0, unroll=2)


def _na_bias(rpb):
    col = np.arange(GRID_W)
    c0 = np.clip(col - NA_COLS // 2, 0, GRID_W - NA_COLS)
    col_ok = (col[None, :] >= c0[:, None]) & (col[None, :] < c0[:, None] + NA_COLS)
    dc = np.clip(col[None, :] - col[:, None] + NA_COLS - 1, 0, 2 * NA_COLS - 2)
    onehot = (dc[None] == np.arange(2 * NA_COLS - 1)[:, None, None]).astype(np.float32)
    e = jnp.einsum('hrd,dqk->hrqk', rpb.astype(F32), jnp.asarray(onehot),
                   precision=lax.Precision.HIGHEST)
    e = jnp.where(col_ok[None, None], e, -jnp.inf)
    b = jnp.stack([e[:, s:s + NA_ROWS] for s in range(NA_ROWS)], axis=1)
    b = jnp.transpose(b, (0, 1, 3, 2, 4))
    return b.reshape(rpb.shape[0], NA_ROWS, GRID_W, NA_ROWS * GRID_W)


def _na_attn(q, k, v, nb):
    qrows = NA_GROUP * GRID_W
    n_g = SEQ // qrows
    seq_spec = pl.BlockSpec((SEQ, 512), lambda b, g: (b, 0))
    ctx_spec = pl.BlockSpec((CTX, 512), lambda b, g: (N_MAIN // CTX + b, 0))
    return pl.pallas_call(
        _na_kernel,
        out_shape=jax.ShapeDtypeStruct((N_MAIN, 512), BF16),
        grid=(BATCH, n_g),
        in_specs=[pl.BlockSpec((qrows, 512), lambda b, g: (b * n_g + g, 0)),
                  seq_spec, seq_spec, ctx_spec, ctx_spec,
                  pl.BlockSpec(nb.shape, lambda b, g: (0, 0, 0, 0))],
        out_specs=pl.BlockSpec((qrows, 512), lambda b, g: (b * n_g + g, 0)),
        compiler_params=_cparams(("arbitrary", "arbitrary")),
        name="neighborhood_attn",
    )(q, k, v, k, v, nb)


F_N2_CHUNK = 8
F_K1_CHUNK = 8


def _four1_kernel(x_ref, w_ref, t_ref):
    w = w_ref[...]
    for j in range(F_N2_CHUNK):
        res = jnp.dot(w, x_ref[:, j * 512:(j + 1) * 512], preferred_element_type=F32)
        t_ref[0, j] = res[:64].astype(BF16)
        t_ref[1, j] = res[64:].astype(BF16)


def _four2_kernel(t_ref, m_ref, cs_ref, y_ref):
    cs = cs_ref[...]
    for j in range(F_K1_CHUNK):
        lanes = slice(j * 512, (j + 1) * 512)
        tt = jnp.concatenate([t_ref[0, :, lanes], t_ref[1, :, lanes]], axis=0)
        pp = jnp.dot(m_ref[j], tt, preferred_element_type=F32)
        pc = jnp.concatenate([pp[:64], pp[64:]], axis=1).astype(BF16)
        y_ref[:, lanes] = jnp.dot(pc, cs, preferred_element_type=F32).astype(BF16)


def _fourier_tables():
    a = np.arange(64)
    ang1 = 2.0 * np.pi * np.outer(a, a) / 64.0
    w1 = np.concatenate([np.cos(ang1), -np.sin(ang1)], axis=0)
    k1 = a[:, None, None]
    k2 = a[None, :, None]
    n2 = a[None, None, :]
    theta = 2.0 * np.pi * (n2 * k2 / 64.0 + n2 * k1 / 4096.0)
    mr = np.cos(theta) / 64.0
    mi = -np.sin(theta) / 64.0
    m = np.concatenate([np.concatenate([mr, -mi], axis=2),
                        np.concatenate([mi, mr], axis=2)], axis=1)
    c = np.arange(128)
    angc = 2.0 * np.pi * np.outer(c, c) / 128.0
    eye4 = np.eye(4)
    cc = np.kron(eye4, np.cos(angc)) / np.sqrt(128.0)
    sc = np.kron(eye4, np.sin(angc)) / np.sqrt(128.0)
    cs = np.concatenate([cc, sc], axis=0)
    return tuple(jnp.asarray(t, F32).astype(BF16) for t in (w1, m, cs))


def _fourier(f):
    w1, m, cs = _fourier_tables()
    fv = f.reshape(N_ALL // 64, 64 * 512)
    n_c = 64 // F_N2_CHUNK
    t = pl.pallas_call(
        _four1_kernel,
        out_shape=jax.ShapeDtypeStruct((BATCH, 2, 64, 64, 512), BF16),
        grid=(BATCH, n_c),
        in_specs=[pl.BlockSpec((64, F_N2_CHUNK * 512), lambda b, c: (b, c)),
                  pl.BlockSpec((128, 64), lambda b, c: (0, 0))],
        out_specs=pl.BlockSpec((None, 2, F_N2_CHUNK, 64, 512), lambda b, c: (b, 0, c, 0, 0)),
        compiler_params=_cparams(("arbitrary", "arbitrary")),
        name="fourier_rows",
    )(fv, w1)
    n_k = 64 // F_K1_CHUNK
    t2 = t.reshape(BATCH, 2, 64, 64 * 512)
    y = pl.pallas_call(
        _four2_kernel,
        out_shape=jax.ShapeDtypeStruct((BATCH * 64, 64 * 512), BF16),
        grid=(BATCH, n_k),
        in_specs=[pl.BlockSpec((None, 2, 64, F_K1_CHUNK * 512), lambda b, c: (b, 0, 0, c)),
                  pl.BlockSpec((F_K1_CHUNK, 128, 128), lambda b, c: (c, 0, 0)),
                  pl.BlockSpec((1024, 512), lambda b, c: (0, 0))],
        out_specs=pl.BlockSpec((64, F_K1_CHUNK * 512), lambda b, c: (b, c)),
        compiler_params=_cparams(("arbitrary", "arbitrary")),
        name="fourier_cols",
    )(t2, m, cs)
    return y.reshape(N_MAIN, 512)


def _route(h2, rwt_ref, rb_ref, carry_ref, te_ref, tw_ref, rk_ref, cnt_ref):
    logits = lax.dot_general(rwt_ref[...], h2, (((1,), (1,)), ((), ())),
                             preferred_element_type=F32,
                             precision=lax.Precision.HIGHEST) + rb_ref[...]
    eidx = lax.broadcasted_iota(I32, logits.shape, 0)
    vals = logits
    sels, tops, idxs = [], [], []
    for _ in range(TOP_K):
        m = jnp.max(vals, axis=0, keepdims=True)
        idx = jnp.min(jnp.where(vals == m, eidx, N_EXPERTS), axis=0, keepdims=True)
        sel = eidx == idx
        sels.append(sel)
        tops.append(m)
        idxs.append(idx)
        vals = jnp.where(sel, -jnp.inf, vals)
    ex = [jnp.exp(t - tops[0]) for t in tops]
    den = ex[0] + ex[1] + ex[2] + ex[3]
    onehot = jnp.zeros(logits.shape, F32)
    for sel in sels:
        onehot = onehot + sel.astype(F32)
    r_i = lax.broadcasted_iota(I32, (TM, TM), 0)
    c_i = lax.broadcasted_iota(I32, (TM, TM), 1)
    upper = (r_i < c_i).astype(BF16)
    prefix = jnp.dot(onehot.astype(BF16), upper, preferred_element_type=F32)
    base = carry_ref[:, 0:1] + prefix
    for k in range(TOP_K):
        te_ref[k:k + 1, :] = idxs[k]
        tw_ref[k:k + 1, :] = ex[k] / den
        rk_ref[k:k + 1, :] = jnp.sum(jnp.where(sels[k], base, 0.0), axis=0, keepdims=True).astype(I32)
    new_carry = carry_ref[...] + jnp.sum(onehot, axis=1, keepdims=True)
    carry_ref[...] = new_carry
    cnt_ref[...] = new_carry


def _out_tail(i, x, y, mod_ref, g_ref, rwt_ref, rb_ref, carry_ref,
              xn_ref, h2_ref, te_ref, tw_ref, rk_ref, cnt_ref):
    @pl.when(i == 0)
    def _():
        carry_ref[...] = jnp.zeros_like(carry_ref)

    xn = x + mod_ref[2:3, :] * y
    xn_ref[...] = xn
    h2 = _rms_mod(xn, g_ref[...], mod_ref[4:5, :], mod_ref[3:4, :])
    h2_ref[...] = h2
    _route(h2, rwt_ref, rb_ref, carry_ref, te_ref, tw_ref, rk_ref, cnt_ref)


def _out_even_kernel(grp_ref, first_ref, last_ref,
                     x_ref, a_ref, gb_ref, z_ref, zp_ref, zn_ref, cw_ref, w_ref, mod_ref, g_ref,
                     rwt_ref, rb_ref,
                     xn_ref, h2_ref, te_ref, tw_ref, rk_ref, cnt_ref, carry_ref):
    i = pl.program_id(0)
    z = z_ref[...]
    zprev = jnp.where(first_ref[i] == 1, 0.0, zp_ref[7:8, :])
    znext = jnp.where(last_ref[i] == 1, 0.0, zn_ref[0:1, :])
    rid = lax.broadcasted_iota(I32, z.shape, 0)
    zm1 = jnp.where(rid == 0, zprev, pltpu.roll(z, 1, 0))
    zp1 = jnp.where(rid == TM - 1, znext, pltpu.roll(z, TM - 1, 0))
    conv = gb_ref[...] * (zm1 * cw_ref[0:1, :] + z * cw_ref[1:2, :] + zp1 * cw_ref[2:3, :])
    y = (jnp.dot(a_ref[...], w_ref[0:512, :], preferred_element_type=F32)
         + jnp.dot(conv.astype(BF16), w_ref[512:1024, :], preferred_element_type=F32))
    _out_tail(i, x_ref[...], y, mod_ref, g_ref, rwt_ref, rb_ref, carry_ref,
              xn_ref, h2_ref, te_ref, tw_ref, rk_ref, cnt_ref)


def _out_odd_kernel(grp_ref, x_ref, a_ref, f_ref, w_ref, mod_ref, g_ref, rwt_ref, rb_ref,
                    xn_ref, h2_ref, te_ref, tw_ref, rk_ref, cnt_ref, carry_ref):
    i = pl.program_id(0)
    y = (jnp.dot(a_ref[...], w_ref[0:512, :], preferred_element_type=F32)
         + jnp.dot(f_ref[...], w_ref[512:1024, :], preferred_element_type=F32))
    _out_tail(i, x_ref[...], y, mod_ref, g_ref, rwt_ref, rb_ref, carry_ref,
              xn_ref, h2_ref, te_ref, tw_ref, rk_ref, cnt_ref)


def _out_shapes(n_rows):
    return (jax.ShapeDtypeStruct((n_rows, D), F32), jax.ShapeDtypeStruct((n_rows, D), F32),
            jax.ShapeDtypeStruct((TOP_K, n_rows), I32), jax.ShapeDtypeStruct((TOP_K, n_rows), F32),
            jax.ShapeDtypeStruct((TOP_K, n_rows), I32), jax.ShapeDtypeStruct((N_EXPERTS, 128), F32))


def _out_even(xall, attn, gb, z, conv_w, w_bf, mod, g, rwt, rb):
    grp, _, first, last = _tile_tables()
    n_rows = N_ALL
    zblocks = n_rows // 8
    im = lambda f: (lambda i, grp, fi, la: f(i))
    tile = lambda n: pl.BlockSpec((TM, n), im(lambda i: (i, 0)))
    const = lambda shape: pl.BlockSpec(shape, im(lambda i: (0,) * len(shape)))
    tk = pl.BlockSpec((TOP_K, TM), im(lambda i: (0, i)))
    return pl.pallas_call(
        _out_even_kernel,
        out_shape=_out_shapes(n_rows),
        grid_spec=pltpu.PrefetchScalarGridSpec(
            num_scalar_prefetch=3,
            grid=(NT_ALL,),
            in_specs=[tile(D), tile(512), tile(512), tile(512),
                      pl.BlockSpec((8, 512), im(lambda i: (jnp.maximum(i * (TM // 8) - 1, 0), 0))),
                      pl.BlockSpec((8, 512), im(lambda i: (jnp.minimum((i + 1) * (TM // 8), zblocks - 1), 0))),
                      const((3, 512)), const((D, D)),
                      pl.BlockSpec((None, 6, D), lambda i, grp, fi, la: (grp[i], 0, 0)),
                      const((1, D)), const((N_EXPERTS, D)), const((N_EXPERTS, 1))],
            out_specs=(tile(D), tile(D), tk, tk, tk, const((N_EXPERTS, 128))),
            scratch_shapes=[pltpu.VMEM((N_EXPERTS, 128), F32)],
        ),
        compiler_params=_cparams(("arbitrary",)),
        name="out_proj_even",
    )(jnp.asarray(grp), jnp.asarray(first), jnp.asarray(last),
      xall, attn, gb, z, z, z, conv_w, w_bf, mod, g, rwt, rb)


def _out_odd(xall, attn, four, w_bf, mod, g, rwt, rb):
    grp, _, _, _ = _tile_tables()
    n_rows = N_MAIN
    im = lambda f: (lambda i, grp: f(i))
    tile = lambda n: pl.BlockSpec((TM, n), im(lambda i: (i, 0)))
    const = lambda shape: pl.BlockSpec(shape, im(lambda i: (0,) * len(shape)))
    tk = pl.BlockSpec((TOP_K, TM), im(lambda i: (0, i)))
    return pl.pallas_call(
        _out_odd_kernel,
        out_shape=_out_shapes(n_rows),
        grid_spec=pltpu.PrefetchScalarGridSpec(
            num_scalar_prefetch=1,
            grid=(NT_MAIN,),
            in_specs=[tile(D), tile(512), tile(512), const((D, D)),
                      pl.BlockSpec((None, 6, D), lambda i, grp: (grp[i], 0, 0)),
                      const((1, D)), const((N_EXPERTS, D)), const((N_EXPERTS, 1))],
            out_specs=(tile(D), tile(D), tk, tk, tk, const((N_EXPERTS, 128))),
            scratch_shapes=[pltpu.VMEM((N_EXPERTS, 128), F32)],
        ),
        compiler_params=_cparams(("arbitrary",)),
        name="out_proj_odd",
    )(jnp.asarray(grp), xall, attn, four, w_bf, mod, g, rwt, rb)


def _moe_plan(counts_f, top_e_t, rank_t, n_tok):
    counts = counts_f[:, 0].astype(I32)
    padded = (counts + TMM - 1) // TMM * TMM
    e_i = jnp.arange(N_EXPERTS, dtype=I32)
    incl = e_i[None, :] <= e_i[:, None]
    pad_end = jnp.sum(jnp.where(incl, padded[None, :], 0), axis=1)
    pad_start = pad_end - padded
    sel = top_e_t[None] == e_i[:, None, None]
    dest = jnp.sum(jnp.where(sel, pad_start[:, None, None], 0), axis=0) + rank_t
    n_blocks = n_tok * TOP_K // TMM + N_EXPERTS
    blk_start = jnp.arange(n_blocks, dtype=I32) * TMM
    block_e = jnp.minimum(jnp.sum((blk_start[:, None] >= pad_end[None, :]).astype(I32), axis=1),
                          N_EXPERTS - 1)
    r = jnp.arange(TMM, dtype=I32)[None, :]
    cend = pad_start + counts
    cend_b = jnp.sum(jnp.where(block_e[:, None] == e_i[None, :], cend[None, :], 0), axis=1)
    nvalid = jnp.clip(cend_b - blk_start, 0, TMM)
    b_i = jnp.arange(n_blocks, dtype=I32)
    spare_base = jnp.sum(jnp.where(b_i[None, :] < b_i[:, None], (TMM - nvalid)[None, :], 0), axis=1)
    n_rows = n_tok * TOP_K
    init = n_rows + spare_base[:, None] + r - nvalid[:, None]
    lead = n_rows + N_SPARE_ROWS + r
    table_init = jnp.concatenate([lead.reshape(-1), init.reshape(-1)])
    return dest, block_e, table_init


N_SPARE_ROWS = N_EXPERTS * TMM
INV_CHUNKS = 8


def _slot_table_kernel(init_hbm, dest_hbm, out_hbm, tab_s, d_s, sem):
    n_chunks, ch = dest_hbm.shape
    c0 = pltpu.make_async_copy(init_hbm, tab_s, sem)
    c0.start()
    c0.wait()

    def chunk(kc, carry):
        cp = pltpu.make_async_copy(dest_hbm.at[kc], d_s, sem)
        cp.start()
        cp.wait()
        k = kc // INV_CHUNKS
        t0 = (kc % INV_CHUNKS) * ch

        def body(j, c):
            tab_s[d_s[j] + TMM] = (t0 + j) * TOP_K + k
            return c

        lax.fori_loop(0, ch, body, 0, unroll=8)
        return carry

    lax.fori_loop(0, n_chunks, chunk, 0)
    c1 = pltpu.make_async_copy(tab_s, out_hbm, sem)
    c1.start()
    c1.wait()


def _slot_table(dest, table_init):
    n_tok = dest.shape[1]
    ch = n_tok // INV_CHUNKS
    n = table_init.shape[0]
    return pl.pallas_call(
        _slot_table_kernel,
        out_shape=jax.ShapeDtypeStruct((n,), I32),
        in_specs=[pl.BlockSpec(memory_space=pl.ANY), pl.BlockSpec(memory_space=pl.ANY)],
        out_specs=pl.BlockSpec(memory_space=pl.ANY),
        scratch_shapes=[pltpu.SMEM((n,), I32), pltpu.SMEM((ch,), I32), pltpu.SemaphoreType.DMA],
        name="moe_slot_table",
    )(table_init, dest.reshape(TOP_K * INV_CHUNKS, ch))


def _moe_kernel(n_rows, be_ref, sv_ref, h2_hbm, wgu_ref, bgu_ref, wdn_ref, bdn_ref, yk_hbm,
                wgu_bf, wdn_bf, xb0, xb1, yb0, yb1, sem_g, sem_s):
    i = pl.program_id(0)
    nb = pl.num_programs(0)

    def gather_row(base, r, xb, sem):
        v = sv_ref[base + r]
        tok = jnp.where(v < n_rows, v >> 2, 0)
        return pltpu.make_async_copy(h2_hbm.at[pl.ds(tok, 1)], xb.at[pl.ds(r, 1)], sem)

    def scatter_row(base, r, yb, sem):
        return pltpu.make_async_copy(yb.at[pl.ds(r, 1)], yk_hbm.at[pl.ds(sv_ref[base + r], 1)], sem)

    def wait_gather(sem):
        pltpu.make_async_copy(h2_hbm.at[pl.ds(0, TMM)], xb0, sem).wait()

    def wait_scatter(sem):
        pltpu.make_async_copy(yb0, yk_hbm.at[pl.ds(0, TMM)], sem).wait()

    @pl.when(i == 0)
    def _():
        yb1[...] = jnp.zeros_like(yb1)

        def first(r, c):
            gather_row(TMM, r, xb0, sem_g.at[0]).start()
            return c

        lax.fori_loop(0, TMM, first, 0, unroll=8)

    @pl.when(i >= 1)
    def _():
        wait_scatter(sem_s.at[i % 2])

    @pl.when((i == 0) | (be_ref[i] != be_ref[jnp.maximum(i - 1, 0)]))
    def _():
        wgu_bf[...] = wgu_ref[...].astype(BF16)
        wdn_bf[...] = wdn_ref[...].astype(BF16)

    def step(p, xb_cur, xb_nxt, yb_cur, yb_prv):
        wait_gather(sem_g.at[p])
        xb = xb_cur[...].astype(BF16)
        nxt = (jnp.minimum(i + 1, nb - 1) + 1) * TMM
        prv = i * TMM
        for r in range(TMM):
            gather_row(nxt, r, xb_nxt, sem_g.at[1 - p]).start(priority=r % 2)
            scatter_row(prv, r, yb_prv, sem_s.at[1 - p]).start(priority=(r + 1) % 2)
        gu = jnp.dot(xb, wgu_bf[...], preferred_element_type=F32) + bgu_ref[...]
        gate = jnp.minimum(gu[:, :D], SWIGLU_LIMIT)
        up = jnp.clip(gu[:, D:], -SWIGLU_LIMIT, SWIGLU_LIMIT)
        act = (up + 1.0) * (gate * (1.0 / (1.0 + jnp.exp(-SWIGLU_ALPHA * gate))))
        yb_cur[...] = jnp.dot(act.astype(BF16), wdn_bf[...], preferred_element_type=F32) + bdn_ref[...]

    @pl.when(i % 2 == 0)
    def _():
        step(0, xb0, xb1, yb0, yb1)

    @pl.when(i % 2 == 1)
    def _():
        step(1, xb1, xb0, yb1, yb0)

    n_blocks = be_ref.shape[0]
    p_last = (n_blocks - 1) % 2

    @pl.when(i == nb - 1)
    def _():
        yb_last = yb1 if p_last else yb0

        def last(r, c):
            scatter_row(n_blocks * TMM, r, yb_last, sem_s.at[p_last]).start()
            return c

        lax.fori_loop(0, TMM, last, 0, unroll=8)
        wait_scatter(sem_s.at[1 - p_last])
        wait_scatter(sem_s.at[p_last])
        wait_gather(sem_g.at[1 - p_last])


def _moe(layer, h2, block_e, slot_tab, w_gu, b_gu, w_dn, b_dn):
    n_blocks = block_e.shape[0]
    n_l = w_gu.shape[0]
    n_rows = h2.shape[0] * TOP_K
    exp4 = lambda i, be, sv: (layer, be[i], 0, 0)
    return pl.pallas_call(
        functools.partial(_moe_kernel, n_rows),
        out_shape=jax.ShapeDtypeStruct((n_rows + N_SPARE_ROWS + TMM, D), F32),
        grid_spec=pltpu.PrefetchScalarGridSpec(
            num_scalar_prefetch=2,
            grid=(n_blocks,),
            in_specs=[pl.BlockSpec(memory_space=pl.ANY),
                      pl.BlockSpec((None, None, D, 2 * D), exp4),
                      pl.BlockSpec((None, None, 1, 2 * D), exp4),
                      pl.BlockSpec((None, None, D, D), exp4),
                      pl.BlockSpec((None, None, 1, D), exp4)],
            out_specs=pl.BlockSpec(memory_space=pl.ANY),
            scratch_shapes=[pltpu.VMEM((D, 2 * D), BF16), pltpu.VMEM((D, D), BF16),
                            pltpu.VMEM((TMM, D), F32), pltpu.VMEM((TMM, D), F32),
                            pltpu.VMEM((TMM, D), F32), pltpu.VMEM((TMM, D), F32),
                            pltpu.SemaphoreType.DMA((2,)), pltpu.SemaphoreType.DMA((2,))],
        ),
        compiler_params=_cparams(("arbitrary",)),
        name="moe_experts",
    )(block_e, slot_tab, h2, w_gu, b_gu.reshape(n_l, N_EXPERTS, 1, 2 * D), w_dn,
      b_dn.reshape(n_l, N_EXPERTS, 1, D))


N_LANE_CHUNKS = D // 128


def _combine_kernel(final, grp_ref, *refs):
    yk_refs = refs[:N_LANE_CHUNKS]
    x_ref, tw_ref, mod_ref, fn_ref, o_ref = refs[N_LANE_CHUNKS:]
    tw = tw_ref[...]
    w = [tw[:, k:k + 1] for k in range(TOP_K)]
    ssq = jnp.zeros((TT, 1), F32)
    for c in range(N_LANE_CHUNKS):
        lanes = slice(c * 128, (c + 1) * 128)
        acc = w[0] * yk_refs[c][pl.ds(0, TT, stride=TOP_K), :]
        for k in range(1, TOP_K):
            acc = acc + w[k] * yk_refs[c][pl.ds(k, TT, stride=TOP_K), :]
        out = x_ref[:, lanes] + mod_ref[5:6, lanes] * acc
        o_ref[:, lanes] = out
        ssq = ssq + jnp.sum(out * out, axis=-1, keepdims=True)
    if final:
        o_ref[...] = o_ref[...] * lax.rsqrt(ssq * (1.0 / D) + EPS) * fn_ref[...]


def _combine(xn, yk, top_w_t, mod, final_norm, n_tok, final):
    grp, _, _, _ = _tile_tables()
    n_tiles = n_tok // TT
    tw = top_w_t.T
    return pl.pallas_call(
        functools.partial(_combine_kernel, final),
        out_shape=jax.ShapeDtypeStruct((n_tok, D), F32),
        grid_spec=pltpu.PrefetchScalarGridSpec(
            num_scalar_prefetch=1,
            grid=(n_tiles,),
            in_specs=[pl.BlockSpec((TOP_K * TT, 128), functools.partial(lambda c, i, grp: (i, c), c))
                      for c in range(N_LANE_CHUNKS)] + [
                      pl.BlockSpec((TT, D), lambda i, grp: (i, 0)),
                      pl.BlockSpec((TT, TOP_K), lambda i, grp: (i, 0)),
                      pl.BlockSpec((None, 6, D), lambda i, grp: (grp[i], 0, 0)),
                      pl.BlockSpec((1, D), lambda i, grp: (0, 0))],
            out_specs=pl.BlockSpec((TT, D), lambda i, grp: (i, 0)),
        ),
        compiler_params=_cparams(("arbitrary",)),
        name="moe_combine",
    )(jnp.asarray(grp), *([yk] * N_LANE_CHUNKS), xn, tw, mod, final_norm.reshape(1, D))


def _moe_layer(layer, xn, h2, top_e_t, top_w_t, rank_t, counts, mod, w_gu, b_gu, w_dn, b_dn, final_norm, final):
    n_tok = xn.shape[0]
    dest, block_e, table_init = _moe_plan(counts, top_e_t, rank_t, n_tok)
    slot_tab = _slot_table(dest, table_init)
    yk = _moe(layer, h2, block_e, slot_tab, w_gu, b_gu, w_dn, b_dn)
    return _combine(xn, yk, top_w_t, mod, final_norm, n_tok, final)


def kernel(x, c, ctx, c_ctx, ada_w, ada_b, norm_mix, norm_ffn, even_w_in, even_w_out, even_conv_w, even_sink, odd_w_in, odd_w_out, odd_rpb, router_w, router_b, moe_w_gu, moe_b_gu, moe_w_dn, moe_b_dn, final_norm):
    xall = jnp.concatenate([x.reshape(N_MAIN, D), ctx.reshape(N_CTX, D)], axis=0)
    cc = jnp.concatenate([c, c_ctx[None, :], jnp.zeros((3, D), F32)], axis=0)
    mod = _ada(cc, ada_w, ada_b).reshape(2, 8, 6, D)
    cos_f, sin_f = _rope_tables()

    q, k, ks, v, vs, gb, z = _in_even(xall, mod[0], norm_mix[0:1], even_w_in[0].astype(BF16), cos_f, sin_f)
    attn = jnp.concatenate([_win_attn(even_sink[0], q, k, ks, v, vs),
                            _ctx_attn(even_sink[0], q, k, ks, v, vs)], axis=0)
    xn, h2, te, tw, rk, cnt = _out_even(xall, attn, gb, z, even_conv_w[0], even_w_out[0].astype(BF16),
                                        mod[0], norm_ffn[0:1], router_w[0].T, router_b[0][:, None])
    xall = _moe_layer(0, xn, h2, te, tw, rk, cnt, mod[0], moe_w_gu, moe_b_gu, moe_w_dn, moe_b_dn,
                      final_norm, False)

    q, k, v, f = _in_odd(xall, mod[1], norm_mix[1:2], odd_w_in[0].astype(BF16))
    attn = _na_attn(q, k, v, _na_bias(odd_rpb[0]))
    four = _fourier(f)
    xn, h2, te, tw, rk, cnt = _out_odd(xall, attn, four, odd_w_out[0].astype(BF16),
                                       mod[1], norm_ffn[1:2], router_w[1].T, router_b[1][:, None])
    out = _moe_layer(1, xn, h2, te, tw, rk, cnt, mod[1], moe_w_gu, moe_b_gu, moe_w_dn, moe_b_dn,
                     final_norm, True)
    return out.reshape(BATCH, SEQ, D)
```

```python
import functools

import numpy as np
import jax
import jax.numpy as jnp
from jax import lax
from jax.experimental import pallas as pl
from jax.experimental.pallas import tpu as pltpu

F32 = jnp.float32
BF16 = jnp.bfloat16
I32 = jnp.int32

D = 1024
BATCH = 4
SEQ = 4096
CTX = 256
GRID_W = 64
HEAD_DIM = 64
EPS = 1e-6
ROPE_THETA = 10000.0
N_EXPERTS = 32
TOP_K = 4
SWIGLU_LIMIT = 7.0
SWIGLU_ALPHA = 1.702
NA_ROWS = 8
NA_COLS = 16

N_MAIN = BATCH * SEQ
N_CTX = BATCH * CTX
N_ALL = N_MAIN + N_CTX
TM = 256
NT_MAIN = N_MAIN // TM
NT_ALL = N_ALL // TM
TILES_PER_SEQ = SEQ // TM
TMM = 256
TT = 256
N_LANE_CHUNKS = D // 128
VMEM_LIMIT = 56 * 1024 * 1024


def _cparams(sem, vmem=VMEM_LIMIT):
    return pltpu.CompilerParams(dimension_semantics=sem, vmem_limit_bytes=vmem)


def _rms_mod(x, g, sc, sh):
    ms = jnp.mean(x * x, axis=-1, keepdims=True)
    return (x * lax.rsqrt(ms + EPS) * g) * (1.0 + sc) + sh


def _ada_kernel(c_ref, w_ref, b_ref, o_ref):
    c = c_ref[...]
    s = c * (1.0 / (1.0 + jnp.exp(-c)))
    o_ref[...] = jnp.dot(s, w_ref[...], preferred_element_type=F32,
                         precision=lax.Precision.HIGHEST) + b_ref[...]


def _ada(cc, ada_w, ada_b):
    n_l = ada_w.shape[0]
    tn = 1024
    return pl.pallas_call(
        _ada_kernel,
        out_shape=jax.ShapeDtypeStruct((n_l, 8, 6 * D), F32),
        grid=(n_l, 6 * D // tn),
        in_specs=[pl.BlockSpec((8, D), lambda l, j: (0, 0)),
                  pl.BlockSpec((None, D, tn), lambda l, j: (l, 0, j)),
                  pl.BlockSpec((None, 1, tn), lambda l, j: (l, 0, j))],
        out_specs=pl.BlockSpec((None, 8, tn), lambda l, j: (l, 0, j)),
        compiler_params=_cparams(("arbitrary", "arbitrary")),
        name="ada_mod",
    )(cc, ada_w, ada_b.reshape(n_l, 1, 6 * D))


def _rope_apply(t, cos, sin):
    n = t.shape[1]
    lane = lax.broadcasted_iota(I32, t.shape, 1)
    fwd = pltpu.roll(t, n - 32, 1)
    bwd = pltpu.roll(t, 32, 1)
    rot = jnp.where((lane % 64) < 32, fwd, bwd)
    reps = n // 128
    cosf = jnp.concatenate([cos] * reps, axis=1) if reps > 1 else cos
    sinf = jnp.concatenate([sin] * reps, axis=1) if reps > 1 else sin
    return t * cosf + rot * sinf


def _in_even_kernel(grp_ref, rblk_ref, x_ref, mod_ref, g_ref, w_ref, cos_ref, sin_ref,
                    q_ref, k_ref, ks_ref, v_ref, vs_ref, gb_ref, z_ref):
    h = _rms_mod(x_ref[...], g_ref[...], mod_ref[1:2, :], mod_ref[0:1, :])
    p = jnp.dot(h.astype(BF16), w_ref[...], preferred_element_type=F32)
    cos = cos_ref[...]
    sin = sin_ref[...]
    q = _rope_apply(p[:, 0:512], cos, sin) * (HEAD_DIM ** -0.5)
    k = _rope_apply(p[:, 512:640], cos, sin)
    v = p[:, 640:768]
    q_ref[...] = q.astype(BF16)
    k_ref[...] = k.astype(BF16)
    ks_ref[...] = pltpu.roll(k, 64, 1).astype(BF16)
    v_ref[...] = v.astype(BF16)
    vs_ref[...] = pltpu.roll(v, 64, 1).astype(BF16)
    gb_ref[...] = p[:, 768:1280]
    z_ref[...] = p[:, 1280:1792] * p[:, 1792:2304]


def _in_odd_kernel(grp_ref, x_ref, mod_ref, g_ref, w_ref, q_ref, k_ref, v_ref, f_ref):
    h = _rms_mod(x_ref[...], g_ref[...], mod_ref[1:2, :], mod_ref[0:1, :])
    p = jnp.dot(h.astype(BF16), w_ref[...], preferred_element_type=F32)
    q_ref[...] = (p[:, 0:512] * (HEAD_DIM ** -0.5)).astype(BF16)
    k_ref[...] = p[:, 512:1024].astype(BF16)
    v_ref[...] = p[:, 1024:1536].astype(BF16)
    f_ref[...] = p[:, 1536:2048].astype(BF16)


def _tile_tables():
    t = np.arange(NT_ALL)
    main = t < NT_MAIN
    grp = np.where(main, t // TILES_PER_SEQ, BATCH).astype(np.int32)
    rblk = np.where(main, t % TILES_PER_SEQ, TILES_PER_SEQ).astype(np.int32)
    first = np.where(main, (t % TILES_PER_SEQ) == 0, True).astype(np.int32)
    last = np.where(main, (t % TILES_PER_SEQ) == TILES_PER_SEQ - 1, True).astype(np.int32)
    return grp, rblk, first, last


def _rope_tables():
    t = jnp.arange(SEQ, dtype=I32)
    row = (t // GRID_W).astype(F32)
    col = (t % GRID_W).astype(F32)
    n_freq = HEAD_DIM // 4
    inv_freq = jnp.power(ROPE_THETA, -jnp.arange(n_freq, dtype=F32) / n_freq)
    ang = jnp.concatenate([row[:, None] * inv_freq, col[:, None] * inv_freq], axis=-1)
    cos = jnp.cos(ang)
    sin = jnp.sin(ang)
    cos_f = jnp.concatenate([cos, cos, cos, cos], axis=1)
    sin_f = jnp.concatenate([-sin, sin, -sin, sin], axis=1)
    cos_f = jnp.concatenate([cos_f, jnp.ones((TM, 128), F32)], axis=0)
    sin_f = jnp.concatenate([sin_f, jnp.zeros((TM, 128), F32)], axis=0)
    return cos_f, sin_f


def _in_even(xall, mod, g, w_bf, cos_f, sin_f):
    grp, rblk, _, _ = _tile_tables()
    row = lambda n, dt: jax.ShapeDtypeStruct((N_ALL, n), dt)
    tile = lambda n: pl.BlockSpec((TM, n), lambda i, grp, rb: (i, 0))
    return pl.pallas_call(
        _in_even_kernel,
        out_shape=(row(512, BF16), row(128, BF16), row(128, BF16), row(128, BF16), row(128, BF16),
                   row(512, F32), row(512, F32)),
        grid_spec=pltpu.PrefetchScalarGridSpec(
            num_scalar_prefetch=2,
            grid=(NT_ALL,),
            in_specs=[tile(D),
                      pl.BlockSpec((None, 6, D), lambda i, grp, rb: (grp[i], 0, 0)),
                      pl.BlockSpec((1, D), lambda i, grp, rb: (0, 0)),
                      pl.BlockSpec((D, 2304), lambda i, grp, rb: (0, 0)),
                      pl.BlockSpec((TM, 128), lambda i, grp, rb: (rb[i], 0)),
                      pl.BlockSpec((TM, 128), lambda i, grp, rb: (rb[i], 0))],
            out_specs=(tile(512), tile(128), tile(128), tile(128), tile(128), tile(512), tile(512)),
        ),
        compiler_params=_cparams(("arbitrary",)),
        name="in_proj_even",
    )(jnp.asarray(grp), jnp.asarray(rblk), xall, mod, g, w_bf, cos_f, sin_f)


def _in_odd(xall, mod, g, w_bf):
    grp, _, _, _ = _tile_tables()
    row = lambda n, dt: jax.ShapeDtypeStruct((N_ALL, n), dt)
    tile = lambda n: pl.BlockSpec((TM, n), lambda i, grp: (i, 0))
    return pl.pallas_call(
        _in_odd_kernel,
        out_shape=(row(512, BF16), row(512, BF16), row(512, BF16), row(512, BF16)),
        grid_spec=pltpu.PrefetchScalarGridSpec(
            num_scalar_prefetch=1,
            grid=(NT_ALL,),
            in_specs=[tile(D),
                      pl.BlockSpec((None, 6, D), lambda i, grp: (grp[i], 0, 0)),
                      pl.BlockSpec((1, D), lambda i, grp: (0, 0)),
                      pl.BlockSpec((D, 2048), lambda i, grp: (0, 0))],
            out_specs=(tile(512), tile(512), tile(512), tile(512)),
        ),
        compiler_params=_cparams(("arbitrary",)),
        name="in_proj_odd",
    )(jnp.asarray(grp), xall, mod, g, w_bf)


def _nt(a, b):
    return lax.dot_general(a, b, (((1,), (1,)), ((), ())), preferred_element_type=F32)


def _half_mask(shape, half):
    lane = lax.broadcasted_iota(I32, shape, 1)
    return (lane < 64) if half == 0 else (lane >= 64)


def _win_kernel(sink_ref, q_ref, k_ref, ks_ref, v_ref, vs_ref, kc_ref, ksc_ref, vc_ref, vsc_ref, o_ref):
    n = pl.program_id(1)
    start = pl.multiple_of(jnp.clip((n - 1) * 128, 0, SEQ - 384), 128)
    win = pl.ds(start, 384)
    row = lax.broadcasted_iota(I32, (128, 384), 0)
    col = lax.broadcasted_iota(I32, (128, 384), 1)
    valid = jnp.abs((n * 128 + row) - (start + col)) <= 128
    kk = (jnp.concatenate([k_ref[win, :], kc_ref[...]], axis=0),
          jnp.concatenate([ks_ref[win, :], ksc_ref[...]], axis=0))
    vv = (jnp.concatenate([v_ref[win, :], vc_ref[...]], axis=0),
          jnp.concatenate([vs_ref[win, :], vsc_ref[...]], axis=0))
    for c in range(4):
        qc = q_ref[:, c * 128:(c + 1) * 128]
        halves = []
        for hf in range(2):
            h = 2 * c + hf
            swapped = 0 if (h // 4) == hf else 1
            qm = jnp.where(_half_mask(qc.shape, hf), qc, jnp.zeros_like(qc))
            s = _nt(qm, kk[swapped])
            s_loc = jnp.where(valid, s[:, :384], -jnp.inf)
            s_ctx = s[:, 384:]
            sink = sink_ref[h]
            m = jnp.maximum(jnp.maximum(jnp.max(s_loc, axis=1, keepdims=True),
                                        jnp.max(s_ctx, axis=1, keepdims=True)), sink)
            p_loc = jnp.exp(s_loc - m)
            p_ctx = jnp.exp(s_ctx - m)
            den = (jnp.sum(p_loc, axis=1, keepdims=True) + jnp.sum(p_ctx, axis=1, keepdims=True)
                   + jnp.exp(sink - m))
            p = jnp.concatenate([p_loc, p_ctx], axis=1).astype(BF16)
            halves.append(jnp.dot(p, vv[swapped], preferred_element_type=F32) / den)
        o_ref[:, c * 128:(c + 1) * 128] = jnp.where(_half_mask(halves[0].shape, 0),
                                                    halves[0], halves[1]).astype(BF16)


def _win_attn(sink, q, k, ks, v, vs):
    nb = SEQ // 128
    seq_spec = pl.BlockSpec((SEQ, 128), lambda b, n: (b, 0))
    ctx_spec = pl.BlockSpec((CTX, 128), lambda b, n: (N_MAIN // CTX + b, 0))
    return pl.pallas_call(
        _win_kernel,
        out_shape=jax.ShapeDtypeStruct((N_MAIN, 512), BF16),
        grid=(BATCH, nb),
        in_specs=[pl.BlockSpec(memory_space=pltpu.SMEM),
                  pl.BlockSpec((128, 512), lambda b, n: (b * (SEQ // 128) + n, 0)),
                  seq_spec, seq_spec, seq_spec, seq_spec,
                  ctx_spec, ctx_spec, ctx_spec, ctx_spec],
        out_specs=pl.BlockSpec((128, 512), lambda b, n: (b * (SEQ // 128) + n, 0)),
        compiler_params=_cparams(("arbitrary", "arbitrary")),
        name="window_attn",
    )(sink, q, k, ks, v, vs, k, ks, v, vs)


def _ctx_attn_kernel(sink_ref, q_ref, k_ref, ks_ref, v_ref, vs_ref, o_ref):
    kk = (k_ref[...], ks_ref[...])
    vv = (v_ref[...], vs_ref[...])
    for c in range(4):
        qc = q_ref[:, c * 128:(c + 1) * 128]
        halves = []
        for hf in range(2):
            h = 2 * c + hf
            swapped = 0 if (h // 4) == hf else 1
            qm = jnp.where(_half_mask(qc.shape, hf), qc, jnp.zeros_like(qc))
            s = _nt(qm, kk[swapped])
            sink = sink_ref[h]
            m = jnp.maximum(jnp.max(s, axis=1, keepdims=True), sink)
            p = jnp.exp(s - m)
            den = jnp.sum(p, axis=1, keepdims=True) + jnp.exp(sink - m)
            halves.append(jnp.dot(p.astype(BF16), vv[swapped], preferred_element_type=F32) / den)
        o_ref[:, c * 128:(c + 1) * 128] = jnp.where(_half_mask(halves[0].shape, 0),
                                                    halves[0], halves[1]).astype(BF16)


def _ctx_attn(sink, q, k, ks, v, vs):
    ctx_spec = lambda n: pl.BlockSpec((CTX, n), lambda b: (N_MAIN // CTX + b, 0))
    return pl.pallas_call(
        _ctx_attn_kernel,
        out_shape=jax.ShapeDtypeStruct((N_CTX, 512), BF16),
        grid=(BATCH,),
        in_specs=[pl.BlockSpec(memory_space=pltpu.SMEM),
                  ctx_spec(512), ctx_spec(128), ctx_spec(128), ctx_spec(128), ctx_spec(128)],
        out_specs=pl.BlockSpec((CTX, 512), lambda b: (b, 0)),
        compiler_params=_cparams(("arbitrary",)),
        name="context_attn",
    )(sink, q, k, ks, v, vs)


NA_GROUP = 8
N_GRID_ROWS = SEQ // GRID_W


def _na_kernel(q_ref, k_ref, v_ref, kc_ref, vc_ref, nb_ref, o_ref):
    g = pl.program_id(1)

    def body(i, carry):
        r = g * NA_GROUP + i
        r0 = jnp.clip(r - NA_ROWS // 2, 0, N_GRID_ROWS - NA_ROWS)
        shift = r0 - r + NA_ROWS - 1
        qrows = pl.ds(pl.multiple_of(i * GRID_W, GRID_W), GRID_W)
        krows = pl.ds(pl.multiple_of(r0 * GRID_W, GRID_W), NA_ROWS * GRID_W)
        for c in range(4):
            lanes = slice(c * 128, (c + 1) * 128)
            qc = q_ref[qrows, lanes]
            kl = k_ref[krows, lanes]
            vl = v_ref[krows, lanes]
            kx = kc_ref[:, lanes]
            vx = vc_ref[:, lanes]
            halves = []
            for hf in range(2):
                h = 2 * c + hf
                qm = jnp.where(_half_mask(qc.shape, hf), qc, jnp.zeros_like(qc))
                s_loc = _nt(qm, kl) + nb_ref[h, shift]
                s_ctx = _nt(qm, kx)
                m = jnp.maximum(jnp.max(s_loc, axis=1, keepdims=True),
                                jnp.max(s_ctx, axis=1, keepdims=True))
                p_loc = jnp.exp(s_loc - m)
                p_ctx = jnp.exp(s_ctx - m)
                den = jnp.sum(p_loc, axis=1, keepdims=True) + jnp.sum(p_ctx, axis=1, keepdims=True)
                o = (jnp.dot(p_loc.astype(BF16), vl, preferred_element_type=F32)
                     + jnp.dot(p_ctx.astype(BF16), vx, preferred_element_type=F32))
                halves.append(o / den)
            o_ref[qrows, lanes] = jnp.where(_half_mask(halves[0].shape, 0),
                                            halves[0], halves[1]).astype(BF16)
        return carry

    lax.fori_loop(0, NA_GROUP, body, 0, unroll=2)


def _na_bias(rpb):
    col = np.arange(GRID_W)
    c0 = np.clip(col - NA_COLS // 2, 0, GRID_W - NA_COLS)
    col_ok = (col[None, :] >= c0[:, None]) & (col[None, :] < c0[:, None] + NA_COLS)
    dc = np.clip(col[None, :] - col[:, None] + NA_COLS - 1, 0, 2 * NA_COLS - 2)
    onehot = (dc[None] == np.arange(2 * NA_COLS - 1)[:, None, None]).astype(np.float32)
    e = jnp.einsum('hrd,dqk->hrqk', rpb.astype(F32), jnp.asarray(onehot),
                   precision=lax.Precision.HIGHEST)
    e = jnp.where(col_ok[None, None], e, -jnp.inf)
    b = jnp.stack([e[:, s:s + NA_ROWS] for s in range(NA_ROWS)], axis=1)
    b = jnp.transpose(b, (0, 1, 3, 2, 4))
    return b.reshape(rpb.shape[0], NA_ROWS, GRID_W, NA_ROWS * GRID_W)


def _na_attn(q, k, v, nb):
    qrows = NA_GROUP * GRID_W
    n_g = SEQ // qrows
    seq_spec = pl.BlockSpec((SEQ, 512), lambda b, g: (b, 0))
    ctx_spec = pl.BlockSpec((CTX, 512), lambda b, g: (N_MAIN // CTX + b, 0))
    return pl.pallas_call(
        _na_kernel,
        out_shape=jax.ShapeDtypeStruct((N_MAIN, 512), BF16),
        grid=(BATCH, n_g),
        in_specs=[pl.BlockSpec((qrows, 512), lambda b, g: (b * n_g + g, 0)),
                  seq_spec, seq_spec, ctx_spec, ctx_spec,
                  pl.BlockSpec(nb.shape, lambda b, g: (0, 0, 0, 0))],
        out_specs=pl.BlockSpec((qrows, 512), lambda b, g: (b * n_g + g, 0)),
        compiler_params=_cparams(("arbitrary", "arbitrary")),
        name="neighborhood_attn",
    )(q, k, v, k, v, nb)


F_N2_CHUNK = 8
F_K1_CHUNK = 8


def _four1_kernel(x_ref, w_ref, t_ref):
    w = w_ref[...]
    for j in range(F_N2_CHUNK):
        res = jnp.dot(w, x_ref[:, j * 512:(j + 1) * 512], preferred_element_type=F32)
        t_ref[0, j] = res[:64].astype(BF16)
        t_ref[1, j] = res[64:].astype(BF16)


def _four2_kernel(t_ref, m_ref, cs_ref, y_ref):
    cs = cs_ref[...]
    for j in range(F_K1_CHUNK):
        lanes = slice(j * 512, (j + 1) * 512)
        tt = jnp.concatenate([t_ref[0, :, lanes], t_ref[1, :, lanes]], axis=0)
        pp = jnp.dot(m_ref[j], tt, preferred_element_type=F32)
        pc = jnp.concatenate([pp[:64], pp[64:]], axis=1).astype(BF16)
        y_ref[:, lanes] = jnp.dot(pc, cs, preferred_element_type=F32).astype(BF16)


def _fourier_tables():
    a = np.arange(64)
    ang1 = 2.0 * np.pi * np.outer(a, a) / 64.0
    w1 = np.concatenate([np.cos(ang1), -np.sin(ang1)], axis=0)
    k1 = a[:, None, None]
    k2 = a[None, :, None]
    n2 = a[None, None, :]
    theta = 2.0 * np.pi * (n2 * k2 / 64.0 + n2 * k1 / 4096.0)
    mr = np.cos(theta) / 64.0
    mi = -np.sin(theta) / 64.0
    m = np.concatenate([np.concatenate([mr, -mi], axis=2),
                        np.concatenate([mi, mr], axis=2)], axis=1)
    c = np.arange(128)
    angc = 2.0 * np.pi * np.outer(c, c) / 128.0
    eye4 = np.eye(4)
    cc = np.kron(eye4, np.cos(angc)) / np.sqrt(128.0)
    sc = np.kron(eye4, np.sin(angc)) / np.sqrt(128.0)
    cs = np.concatenate([cc, sc], axis=0)
    return tuple(jnp.asarray(t, F32).astype(BF16) for t in (w1, m, cs))


def _fourier(f):
    w1, m, cs = _fourier_tables()
    fv = f.reshape(N_ALL // 64, 64 * 512)
    n_c = 64 // F_N2_CHUNK
    t = pl.pallas_call(
        _four1_kernel,
        out_shape=jax.ShapeDtypeStruct((BATCH, 2, 64, 64, 512), BF16),
        grid=(BATCH, n_c),
        in_specs=[pl.BlockSpec((64, F_N2_CHUNK * 512), lambda b, c: (b, c)),
                  pl.BlockSpec((128, 64), lambda b, c: (0, 0))],
        out_specs=pl.BlockSpec((None, 2, F_N2_CHUNK, 64, 512), lambda b, c: (b, 0, c, 0, 0)),
        compiler_params=_cparams(("arbitrary", "arbitrary")),
        name="fourier_rows",
    )(fv, w1)
    n_k = 64 // F_K1_CHUNK
    t2 = t.reshape(BATCH, 2, 64, 64 * 512)
    y = pl.pallas_call(
        _four2_kernel,
        out_shape=jax.ShapeDtypeStruct((BATCH * 64, 64 * 512), BF16),
        grid=(BATCH, n_k),
        in_specs=[pl.BlockSpec((None, 2, 64, F_K1_CHUNK * 512), lambda b, c: (b, 0, 0, c)),
                  pl.BlockSpec((F_K1_CHUNK, 128, 128), lambda b, c: (c, 0, 0)),
                  pl.BlockSpec((1024, 512), lambda b, c: (0, 0))],
        out_specs=pl.BlockSpec((64, F_K1_CHUNK * 512), lambda b, c: (b, c)),
        compiler_params=_cparams(("arbitrary", "arbitrary")),
        name="fourier_cols",
    )(t2, m, cs)
    return y.reshape(N_MAIN, 512)


def _route(h2, rwt_ref, rb_ref, carry_ref, te_ref, tw_ref, rk_ref, cnt_ref):
    logits = lax.dot_general(rwt_ref[...], h2, (((1,), (1,)), ((), ())),
                             preferred_element_type=F32,
                             precision=lax.Precision.HIGHEST) + rb_ref[...]
    eidx = lax.broadcasted_iota(I32, logits.shape, 0)
    vals = logits
    sels, tops, idxs = [], [], []
    for _ in range(TOP_K):
        m = jnp.max(vals, axis=0, keepdims=True)
        idx = jnp.min(jnp.where(vals == m, eidx, N_EXPERTS), axis=0, keepdims=True)
        sel = eidx == idx
        sels.append(sel)
        tops.append(m)
        idxs.append(idx)
        vals = jnp.where(sel, -jnp.inf, vals)
    ex = [jnp.exp(t - tops[0]) for t in tops]
    den = ex[0] + ex[1] + ex[2] + ex[3]
    onehot = jnp.zeros(logits.shape, F32)
    for sel in sels:
        onehot = onehot + sel.astype(F32)
    r_i = lax.broadcasted_iota(I32, (TM, TM), 0)
    c_i = lax.broadcasted_iota(I32, (TM, TM), 1)
    upper = (r_i < c_i).astype(BF16)
    prefix = jnp.dot(onehot.astype(BF16), upper, preferred_element_type=F32)
    base = carry_ref[:, 0:1] + prefix
    for k in range(TOP_K):
        te_ref[k:k + 1, :] = idxs[k]
        tw_ref[k:k + 1, :] = ex[k] / den
        rk_ref[k:k + 1, :] = jnp.sum(jnp.where(sels[k], base, 0.0), axis=0, keepdims=True).astype(I32)
    new_carry = carry_ref[...] + jnp.sum(onehot, axis=1, keepdims=True)
    carry_ref[...] = new_carry
    cnt_ref[...] = new_carry


def _out_tail(i, x, y, mod_ref, g_ref, rwt_ref, rb_ref, carry_ref,
              xn_ref, h2_ref, te_ref, tw_ref, rk_ref, cnt_ref):
    @pl.when(i == 0)
    def _():
        carry_ref[...] = jnp.zeros_like(carry_ref)

    xn = x + mod_ref[2:3, :] * y
    xn_ref[...] = xn
    h2 = _rms_mod(xn, g_ref[...], mod_ref[4:5, :], mod_ref[3:4, :])
    for c in range(N_LANE_CHUNKS):
        h2_ref[pl.ds(c, TM, stride=N_LANE_CHUNKS), :] = h2[:, c * 128:(c + 1) * 128]
    _route(h2, rwt_ref, rb_ref, carry_ref, te_ref, tw_ref, rk_ref, cnt_ref)


def _out_even_kernel(grp_ref, first_ref, last_ref,
                     x_ref, a_ref, gb_ref, z_ref, zp_ref, zn_ref, cw_ref, w_ref, mod_ref, g_ref,
                     rwt_ref, rb_ref,
                     xn_ref, h2_ref, te_ref, tw_ref, rk_ref, cnt_ref, carry_ref):
    i = pl.program_id(0)
    z = z_ref[...]
    zprev = jnp.where(first_ref[i] == 1, 0.0, zp_ref[7:8, :])
    znext = jnp.where(last_ref[i] == 1, 0.0, zn_ref[0:1, :])
    rid = lax.broadcasted_iota(I32, z.shape, 0)
    zm1 = jnp.where(rid == 0, zprev, pltpu.roll(z, 1, 0))
    zp1 = jnp.where(rid == TM - 1, znext, pltpu.roll(z, TM - 1, 0))
    conv = gb_ref[...] * (zm1 * cw_ref[0:1, :] + z * cw_ref[1:2, :] + zp1 * cw_ref[2:3, :])
    y = (jnp.dot(a_ref[...], w_ref[0:512, :], preferred_element_type=F32)
         + jnp.dot(conv.astype(BF16), w_ref[512:1024, :], preferred_element_type=F32))
    _out_tail(i, x_ref[...], y, mod_ref, g_ref, rwt_ref, rb_ref, carry_ref,
              xn_ref, h2_ref, te_ref, tw_ref, rk_ref, cnt_ref)


def _out_odd_kernel(grp_ref, x_ref, a_ref, f_ref, w_ref, mod_ref, g_ref, rwt_ref, rb_ref,
                    xn_ref, h2_ref, te_ref, tw_ref, rk_ref, cnt_ref, carry_ref):
    i = pl.program_id(0)
    y = (jnp.dot(a_ref[...], w_ref[0:512, :], preferred_element_type=F32)
         + jnp.dot(f_ref[...], w_ref[512:1024, :], preferred_element_type=F32))
    _out_tail(i, x_ref[...], y, mod_ref, g_ref, rwt_ref, rb_ref, carry_ref,
              xn_ref, h2_ref, te_ref, tw_ref, rk_ref, cnt_ref)


def _out_shapes(n_rows):
    return (jax.ShapeDtypeStruct((n_rows, D), F32), jax.ShapeDtypeStruct((n_rows * N_LANE_CHUNKS, 128), F32),
            jax.ShapeDtypeStruct((TOP_K, n_rows), I32), jax.ShapeDtypeStruct((TOP_K, n_rows), F32),
            jax.ShapeDtypeStruct((TOP_K, n_rows), I32), jax.ShapeDtypeStruct((N_EXPERTS, 128), F32))


def _out_even(xall, attn, gb, z, conv_w, w_bf, mod, g, rwt, rb):
    grp, _, first, last = _tile_tables()
    n_rows = N_ALL
    zblocks = n_rows // 8
    im = lambda f: (lambda i, grp, fi, la: f(i))
    tile = lambda n: pl.BlockSpec((TM, n), im(lambda i: (i, 0)))
    const = lambda shape: pl.BlockSpec(shape, im(lambda i: (0,) * len(shape)))
    tk = pl.BlockSpec((TOP_K, TM), im(lambda i: (0, i)))
    return pl.pallas_call(
        _out_even_kernel,
        out_shape=_out_shapes(n_rows),
        grid_spec=pltpu.PrefetchScalarGridSpec(
            num_scalar_prefetch=3,
            grid=(NT_ALL,),
            in_specs=[tile(D), tile(512), tile(512), tile(512),
                      pl.BlockSpec((8, 512), im(lambda i: (jnp.maximum(i * (TM // 8) - 1, 0), 0))),
                      pl.BlockSpec((8, 512), im(lambda i: (jnp.minimum((i + 1) * (TM // 8), zblocks - 1), 0))),
                      const((3, 512)), const((D, D)),
                      pl.BlockSpec((None, 6, D), lambda i, grp, fi, la: (grp[i], 0, 0)),
                      const((1, D)), const((N_EXPERTS, D)), const((N_EXPERTS, 1))],
            out_specs=(tile(D), pl.BlockSpec((TM * N_LANE_CHUNKS, 128), im(lambda i: (i, 0))),
                       tk, tk, tk, const((N_EXPERTS, 128))),
            scratch_shapes=[pltpu.VMEM((N_EXPERTS, 128), F32)],
        ),
        compiler_params=_cparams(("arbitrary",)),
        name="out_proj_even",
    )(jnp.asarray(grp), jnp.asarray(first), jnp.asarray(last),
      xall, attn, gb, z, z, z, conv_w, w_bf, mod, g, rwt, rb)


def _out_odd(xall, attn, four, w_bf, mod, g, rwt, rb):
    grp, _, _, _ = _tile_tables()
    n_rows = N_MAIN
    im = lambda f: (lambda i, grp: f(i))
    tile = lambda n: pl.BlockSpec((TM, n), im(lambda i: (i, 0)))
    const = lambda shape: pl.BlockSpec(shape, im(lambda i: (0,) * len(shape)))
    tk = pl.BlockSpec((TOP_K, TM), im(lambda i: (0, i)))
    return pl.pallas_call(
        _out_odd_kernel,
        out_shape=_out_shapes(n_rows),
        grid_spec=pltpu.PrefetchScalarGridSpec(
            num_scalar_prefetch=1,
            grid=(NT_MAIN,),
            in_specs=[tile(D), tile(512), tile(512), const((D, D)),
                      pl.BlockSpec((None, 6, D), lambda i, grp: (grp[i], 0, 0)),
                      const((1, D)), const((N_EXPERTS, D)), const((N_EXPERTS, 1))],
            out_specs=(tile(D), pl.BlockSpec((TM * N_LANE_CHUNKS, 128), im(lambda i: (i, 0))),
                       tk, tk, tk, const((N_EXPERTS, 128))),
            scratch_shapes=[pltpu.VMEM((N_EXPERTS, 128), F32)],
        ),
        compiler_params=_cparams(("arbitrary",)),
        name="out_proj_odd",
    )(jnp.asarray(grp), xall, attn, four, w_bf, mod, g, rwt, rb)


def _moe_plan(counts_f, top_e_t, rank_t, n_tok):
    counts = counts_f[:, 0].astype(I32)
    padded = (counts + TMM - 1) // TMM * TMM
    e_i = jnp.arange(N_EXPERTS, dtype=I32)
    incl = e_i[None, :] <= e_i[:, None]
    pad_end = jnp.sum(jnp.where(incl, padded[None, :], 0), axis=1)
    pad_start = pad_end - padded
    sel = top_e_t[None] == e_i[:, None, None]
    dest = jnp.sum(jnp.where(sel, pad_start[:, None, None], 0), axis=0) + rank_t
    n_blocks = n_tok * TOP_K // TMM + N_EXPERTS
    blk_start = jnp.arange(n_blocks, dtype=I32) * TMM
    block_e = jnp.minimum(jnp.sum((blk_start[:, None] >= pad_end[None, :]).astype(I32), axis=1),
                          N_EXPERTS - 1)
    r = jnp.arange(TMM, dtype=I32)[None, :]
    cend = pad_start + counts
    cend_b = jnp.sum(jnp.where(block_e[:, None] == e_i[None, :], cend[None, :], 0), axis=1)
    nvalid = jnp.clip(cend_b - blk_start, 0, TMM)
    b_i = jnp.arange(n_blocks, dtype=I32)
    spare_base = jnp.sum(jnp.where(b_i[None, :] < b_i[:, None], (TMM - nvalid)[None, :], 0), axis=1)
    n_rows = n_tok * TOP_K
    init = n_rows + spare_base[:, None] + r - nvalid[:, None]
    lead = n_rows + N_SPARE_ROWS + r
    table_init = jnp.concatenate([lead.reshape(-1), init.reshape(-1)])
    return dest, block_e, table_init


N_SPARE_ROWS = N_EXPERTS * TMM
INV_CHUNKS = 8


def _slot_table_kernel(init_hbm, dest_hbm, out_hbm, tab_s, d_s, sem):
    n_chunks, ch = dest_hbm.shape
    c0 = pltpu.make_async_copy(init_hbm, tab_s, sem)
    c0.start()
    c0.wait()

    def chunk(kc, carry):
        cp = pltpu.make_async_copy(dest_hbm.at[kc], d_s, sem)
        cp.start()
        cp.wait()
        k = kc // INV_CHUNKS
        t0 = (kc % INV_CHUNKS) * ch

        def body(j, c):
            tab_s[d_s[j] + TMM] = (t0 + j) * TOP_K + k
            return c

        lax.fori_loop(0, ch, body, 0, unroll=8)
        return carry

    lax.fori_loop(0, n_chunks, chunk, 0)
    c1 = pltpu.make_async_copy(tab_s, out_hbm, sem)
    c1.start()
    c1.wait()


def _slot_table(dest, table_init):
    n_tok = dest.shape[1]
    ch = n_tok // INV_CHUNKS
    n = table_init.shape[0]
    return pl.pallas_call(
        _slot_table_kernel,
        out_shape=jax.ShapeDtypeStruct((n,), I32),
        in_specs=[pl.BlockSpec(memory_space=pl.ANY), pl.BlockSpec(memory_space=pl.ANY)],
        out_specs=pl.BlockSpec(memory_space=pl.ANY),
        scratch_shapes=[pltpu.SMEM((n,), I32), pltpu.SMEM((ch,), I32), pltpu.SemaphoreType.DMA],
        name="moe_slot_table",
    )(table_init, dest.reshape(TOP_K * INV_CHUNKS, ch))


def _moe_kernel(n_rows, be_ref, sv_ref, h2_hbm, wgu_ref, bgu_ref, wdn_ref, bdn_ref, yk_hbm,
                wgu_bf, wdn_bf, xb0, xb1, yb0, yb1, sem_g, sem_s):
    i = pl.program_id(0)
    nb = pl.num_programs(0)

    nc = N_LANE_CHUNKS

    def tile(ref, row):
        return ref.at[pl.ds(pl.multiple_of(row * nc, nc), nc)]

    def gather_row(base, r, xb, sem):
        v = sv_ref[base + r]
        tok = jnp.where(v < n_rows, v >> 2, 0)
        return pltpu.make_async_copy(tile(h2_hbm, tok), xb.at[pl.ds(r * nc, nc)], sem)

    def scatter_row(base, r, yb, sem):
        return pltpu.make_async_copy(yb.at[pl.ds(r * nc, nc)], tile(yk_hbm, sv_ref[base + r]), sem)

    def wait_gather(sem):
        pltpu.make_async_copy(h2_hbm.at[pl.ds(0, TMM * nc)], xb0, sem).wait()

    def wait_scatter(sem):
        pltpu.make_async_copy(yb0, yk_hbm.at[pl.ds(0, TMM * nc)], sem).wait()

    @pl.when(i == 0)
    def _():
        yb1[...] = jnp.zeros_like(yb1)

        def first(r, c):
            gather_row(TMM, r, xb0, sem_g.at[0]).start()
            return c

        lax.fori_loop(0, TMM, first, 0, unroll=8)

    @pl.when(i >= 1)
    def _():
        wait_scatter(sem_s.at[i % 2])

    @pl.when((i == 0) | (be_ref[i] != be_ref[jnp.maximum(i - 1, 0)]))
    def _():
        wgu_bf[...] = wgu_ref[...].astype(BF16)
        wdn_bf[...] = wdn_ref[...].astype(BF16)

    def step(p, xb_cur, xb_nxt, yb_cur, yb_prv):
        wait_gather(sem_g.at[p])
        xb = jnp.concatenate([xb_cur[pl.ds(c, TMM, stride=nc), :] for c in range(nc)], axis=1).astype(BF16)
        nxt = (jnp.minimum(i + 1, nb - 1) + 1) * TMM
        prv = i * TMM
        for r in range(TMM):
            gather_row(nxt, r, xb_nxt, sem_g.at[1 - p]).start(priority=r % 2)
            scatter_row(prv, r, yb_prv, sem_s.at[1 - p]).start(priority=(r + 1) % 2)
        gu = jnp.dot(xb, wgu_bf[...], preferred_element_type=F32) + bgu_ref[...]
        gate = jnp.minimum(gu[:, :D], SWIGLU_LIMIT)
        up = jnp.clip(gu[:, D:], -SWIGLU_LIMIT, SWIGLU_LIMIT)
        act = (up + 1.0) * (gate * (1.0 / (1.0 + jnp.exp(-SWIGLU_ALPHA * gate))))
        y = jnp.dot(act.astype(BF16), wdn_bf[...], preferred_element_type=F32) + bdn_ref[...]
        for c in range(nc):
            yb_cur[pl.ds(c, TMM, stride=nc), :] = y[:, c * 128:(c + 1) * 128]

    @pl.when(i % 2 == 0)
    def _():
        step(0, xb0, xb1, yb0, yb1)

    @pl.when(i % 2 == 1)
    def _():
        step(1, xb1, xb0, yb1, yb0)

    n_blocks = be_ref.shape[0]
    p_last = (n_blocks - 1) % 2

    @pl.when(i == nb - 1)
    def _():
        yb_last = yb1 if p_last else yb0

        def last(r, c):
            scatter_row(n_blocks * TMM, r, yb_last, sem_s.at[p_last]).start()
            return c

        lax.fori_loop(0, TMM, last, 0, unroll=8)
        wait_scatter(sem_s.at[1 - p_last])
        wait_scatter(sem_s.at[p_last])
        wait_gather(sem_g.at[1 - p_last])


def _moe(layer, h2, block_e, slot_tab, w_gu, b_gu, w_dn, b_dn):
    n_blocks = block_e.shape[0]
    n_l = w_gu.shape[0]
    n_rows = h2.shape[0] // N_LANE_CHUNKS * TOP_K
    exp4 = lambda i, be, sv: (layer, be[i], 0, 0)
    return pl.pallas_call(
        functools.partial(_moe_kernel, n_rows),
        out_shape=jax.ShapeDtypeStruct(((n_rows + N_SPARE_ROWS + TMM) * N_LANE_CHUNKS, 128), F32),
        grid_spec=pltpu.PrefetchScalarGridSpec(
            num_scalar_prefetch=2,
            grid=(n_blocks,),
            in_specs=[pl.BlockSpec(memory_space=pl.ANY),
                      pl.BlockSpec((None, None, D, 2 * D), exp4),
                      pl.BlockSpec((None, None, 1, 2 * D), exp4),
                      pl.BlockSpec((None, None, D, D), exp4),
                      pl.BlockSpec((None, None, 1, D), exp4)],
            out_specs=pl.BlockSpec(memory_space=pl.ANY),
            scratch_shapes=[pltpu.VMEM((D, 2 * D), BF16), pltpu.VMEM((D, D), BF16),
                            pltpu.VMEM((TMM * N_LANE_CHUNKS, 128), F32), pltpu.VMEM((TMM * N_LANE_CHUNKS, 128), F32),
                            pltpu.VMEM((TMM * N_LANE_CHUNKS, 128), F32), pltpu.VMEM((TMM * N_LANE_CHUNKS, 128), F32),
                            pltpu.SemaphoreType.DMA((2,)), pltpu.SemaphoreType.DMA((2,))],
        ),
        compiler_params=_cparams(("arbitrary",)),
        name="moe_experts",
    )(block_e, slot_tab, h2, w_gu, b_gu.reshape(n_l, N_EXPERTS, 1, 2 * D), w_dn,
      b_dn.reshape(n_l, N_EXPERTS, 1, D))


def _combine_kernel(final, grp_ref, yk_ref, x_ref, tw_ref, mod_ref, fn_ref, o_ref):
    tw = tw_ref[...]
    w = [tw[:, k:k + 1] for k in range(TOP_K)]
    ssq = jnp.zeros((TT, 1), F32)
    stride = TOP_K * N_LANE_CHUNKS
    for c in range(N_LANE_CHUNKS):
        lanes = slice(c * 128, (c + 1) * 128)
        acc = w[0] * yk_ref[pl.ds(c, TT, stride=stride), :]
        for k in range(1, TOP_K):
            acc = acc + w[k] * yk_ref[pl.ds(k * N_LANE_CHUNKS + c, TT, stride=stride), :]
        out = x_ref[:, lanes] + mod_ref[5:6, lanes] * acc
        o_ref[:, lanes] = out
        ssq = ssq + jnp.sum(out * out, axis=-1, keepdims=True)
    if final:
        o_ref[...] = o_ref[...] * lax.rsqrt(ssq * (1.0 / D) + EPS) * fn_ref[...]


def _combine(xn, yk, top_w_t, mod, final_norm, n_tok, final):
    grp, _, _, _ = _tile_tables()
    n_tiles = n_tok // TT
    tw = top_w_t.T
    return pl.pallas_call(
        functools.partial(_combine_kernel, final),
        out_shape=jax.ShapeDtypeStruct((n_tok, D), F32),
        grid_spec=pltpu.PrefetchScalarGridSpec(
            num_scalar_prefetch=1,
            grid=(n_tiles,),
            in_specs=[pl.BlockSpec((TOP_K * TT * N_LANE_CHUNKS, 128), lambda i, grp: (i, 0)),
                      pl.BlockSpec((TT, D), lambda i, grp: (i, 0)),
                      pl.BlockSpec((TT, TOP_K), lambda i, grp: (i, 0)),
                      pl.BlockSpec((None, 6, D), lambda i, grp: (grp[i], 0, 0)),
                      pl.BlockSpec((1, D), lambda i, grp: (0, 0))],
            out_specs=pl.BlockSpec((TT, D), lambda i, grp: (i, 0)),
        ),
        compiler_params=_cparams(("arbitrary",)),
        name="moe_combine",
    )(jnp.asarray(grp), yk, xn, tw, mod, final_norm.reshape(1, D))


def _moe_layer(layer, xn, h2, top_e_t, top_w_t, rank_t, counts, mod, w_gu, b_gu, w_dn, b_dn, final_norm, final):
    n_tok = xn.shape[0]
    dest, block_e, table_init = _moe_plan(counts, top_e_t, rank_t, n_tok)
    slot_tab = _slot_table(dest, table_init)
    yk = _moe(layer, h2, block_e, slot_tab, w_gu, b_gu, w_dn, b_dn)
    return _combine(xn, yk, top_w_t, mod, final_norm, n_tok, final)


def kernel(x, c, ctx, c_ctx, ada_w, ada_b, norm_mix, norm_ffn, even_w_in, even_w_out, even_conv_w, even_sink, odd_w_in, odd_w_out, odd_rpb, router_w, router_b, moe_w_gu, moe_b_gu, moe_w_dn, moe_b_dn, final_norm):
    xall = jnp.concatenate([x.reshape(N_MAIN, D), ctx.reshape(N_CTX, D)], axis=0)
    cc = jnp.concatenate([c, c_ctx[None, :], jnp.zeros((3, D), F32)], axis=0)
    mod = _ada(cc, ada_w, ada_b).reshape(2, 8, 6, D)
    cos_f, sin_f = _rope_tables()

    q, k, ks, v, vs, gb, z = _in_even(xall, mod[0], norm_mix[0:1], even_w_in[0].astype(BF16), cos_f, sin_f)
    attn = jnp.concatenate([_win_attn(even_sink[0], q, k, ks, v, vs),
                            _ctx_attn(even_sink[0], q, k, ks, v, vs)], axis=0)
    xn, h2, te, tw, rk, cnt = _out_even(xall, attn, gb, z, even_conv_w[0], even_w_out[0].astype(BF16),
                                        mod[0], norm_ffn[0:1], router_w[0].T, router_b[0][:, None])
    xall = _moe_layer(0, xn, h2, te, tw, rk, cnt, mod[0], moe_w_gu, moe_b_gu, moe_w_dn, moe_b_dn,
                      final_norm, False)

    q, k, v, f = _in_odd(xall, mod[1], norm_mix[1:2], odd_w_in[0].astype(BF16))
    attn = _na_attn(q, k, v, _na_bias(odd_rpb[0]))
    four = _fourier(f)
    xn, h2, te, tw, rk, cnt = _out_odd(xall, attn, four, odd_w_out[0].astype(BF16),
                                       mod[1], norm_ffn[1:2], router_w[1].T, router_b[1][:, None])
    out = _moe_layer(1, xn, h2, te, tw, rk, cnt, mod[1], moe_w_gu, moe_b_gu, moe_w_dn, moe_b_dn,
                     final_norm, True)
    return out.reshape(BATCH, SEQ, D)
```

```python
import functools

import numpy as np
import jax
import jax.numpy as jnp
from jax import lax
from jax.experimental import pallas as pl
from jax.experimental.pallas import tpu as pltpu

F32 = jnp.float32
BF16 = jnp.bfloat16
I32 = jnp.int32

D = 1024
BATCH = 4
SEQ = 4096
CTX = 256
GRID_W = 64
HEAD_DIM = 64
EPS = 1e-6
ROPE_THETA = 10000.0
N_EXPERTS = 32
TOP_K = 4
SWIGLU_LIMIT = 7.0
SWIGLU_ALPHA = 1.702
NA_ROWS = 8
NA_COLS = 16

N_MAIN = BATCH * SEQ
N_CTX = BATCH * CTX
N_ALL = N_MAIN + N_CTX
TM = 256
NT_MAIN = N_MAIN // TM
NT_ALL = N_ALL // TM
TILES_PER_SEQ = SEQ // TM
TMM = 512
TT = 256
VMEM_LIMIT = 56 * 1024 * 1024


def _cparams(sem, vmem=VMEM_LIMIT):
    return pltpu.CompilerParams(dimension_semantics=sem, vmem_limit_bytes=vmem)


def _rms_mod(x, g, sc, sh):
    ms = jnp.mean(x * x, axis=-1, keepdims=True)
    return (x * lax.rsqrt(ms + EPS) * g) * (1.0 + sc) + sh


def _ada_kernel(c_ref, w_ref, b_ref, o_ref):
    c = c_ref[...]
    s = c * (1.0 / (1.0 + jnp.exp(-c)))
    o_ref[...] = jnp.dot(s, w_ref[...], preferred_element_type=F32,
                         precision=lax.Precision.HIGHEST) + b_ref[...]


def _ada(cc, ada_w, ada_b):
    n_l = ada_w.shape[0]
    tn = 1024
    return pl.pallas_call(
        _ada_kernel,
        out_shape=jax.ShapeDtypeStruct((n_l, 8, 6 * D), F32),
        grid=(n_l, 6 * D // tn),
        in_specs=[pl.BlockSpec((8, D), lambda l, j: (0, 0)),
                  pl.BlockSpec((None, D, tn), lambda l, j: (l, 0, j)),
                  pl.BlockSpec((None, 1, tn), lambda l, j: (l, 0, j))],
        out_specs=pl.BlockSpec((None, 8, tn), lambda l, j: (l, 0, j)),
        compiler_params=_cparams(("arbitrary", "arbitrary")),
        name="ada_mod",
    )(cc, ada_w, ada_b.reshape(n_l, 1, 6 * D))


def _rope_apply(t, cos, sin):
    n = t.shape[1]
    lane = lax.broadcasted_iota(I32, t.shape, 1)
    fwd = pltpu.roll(t, n - 32, 1)
    bwd = pltpu.roll(t, 32, 1)
    rot = jnp.where((lane % 64) < 32, fwd, bwd)
    reps = n // 128
    cosf = jnp.concatenate([cos] * reps, axis=1) if reps > 1 else cos
    sinf = jnp.concatenate([sin] * reps, axis=1) if reps > 1 else sin
    return t * cosf + rot * sinf


def _in_even_kernel(grp_ref, rblk_ref, x_ref, mod_ref, g_ref, w_ref, cos_ref, sin_ref,
                    q_ref, k_ref, ks_ref, v_ref, vs_ref, gb_ref, z_ref):
    h = _rms_mod(x_ref[...], g_ref[...], mod_ref[1:2, :], mod_ref[0:1, :])
    p = jnp.dot(h.astype(BF16), w_ref[...], preferred_element_type=F32)
    cos = cos_ref[...]
    sin = sin_ref[...]
    q = _rope_apply(p[:, 0:512], cos, sin) * (HEAD_DIM ** -0.5)
    k = _rope_apply(p[:, 512:640], cos, sin)
    v = p[:, 640:768]
    q_ref[...] = q.astype(BF16)
    k_ref[...] = k.astype(BF16)
    ks_ref[...] = pltpu.roll(k, 64, 1).astype(BF16)
    v_ref[...] = v.astype(BF16)
    vs_ref[...] = pltpu.roll(v, 64, 1).astype(BF16)
    gb_ref[...] = p[:, 768:1280]
    z_ref[...] = p[:, 1280:1792] * p[:, 1792:2304]


def _in_odd_kernel(grp_ref, x_ref, mod_ref, g_ref, w_ref, q_ref, k_ref, v_ref, f_ref):
    h = _rms_mod(x_ref[...], g_ref[...], mod_ref[1:2, :], mod_ref[0:1, :])
    p = jnp.dot(h.astype(BF16), w_ref[...], preferred_element_type=F32)
    q_ref[...] = (p[:, 0:512] * (HEAD_DIM ** -0.5)).astype(BF16)
    k_ref[...] = p[:, 512:1024].astype(BF16)
    v_ref[...] = p[:, 1024:1536].astype(BF16)
    f_ref[...] = p[:, 1536:2048].astype(BF16)


def _tile_tables():
    t = np.arange(NT_ALL)
    main = t < NT_MAIN
    grp = np.where(main, t // TILES_PER_SEQ, BATCH).astype(np.int32)
    rblk = np.where(main, t % TILES_PER_SEQ, TILES_PER_SEQ).astype(np.int32)
    first = np.where(main, (t % TILES_PER_SEQ) == 0, True).astype(np.int32)
    last = np.where(main, (t % TILES_PER_SEQ) == TILES_PER_SEQ - 1, True).astype(np.int32)
    return grp, rblk, first, last


def _rope_tables():
    t = jnp.arange(SEQ, dtype=I32)
    row = (t // GRID_W).astype(F32)
    col = (t % GRID_W).astype(F32)
    n_freq = HEAD_DIM // 4
    inv_freq = jnp.power(ROPE_THETA, -jnp.arange(n_freq, dtype=F32) / n_freq)
    ang = jnp.concatenate([row[:, None] * inv_freq, col[:, None] * inv_freq], axis=-1)
    cos = jnp.cos(ang)
    sin = jnp.sin(ang)
    cos_f = jnp.concatenate([cos, cos, cos, cos], axis=1)
    sin_f = jnp.concatenate([-sin, sin, -sin, sin], axis=1)
    cos_f = jnp.concatenate([cos_f, jnp.ones((TM, 128), F32)], axis=0)
    sin_f = jnp.concatenate([sin_f, jnp.zeros((TM, 128), F32)], axis=0)
    return cos_f, sin_f


def _in_even(xall, mod, g, w_bf, cos_f, sin_f):
    grp, rblk, _, _ = _tile_tables()
    row = lambda n, dt: jax.ShapeDtypeStruct((N_ALL, n), dt)
    tile = lambda n: pl.BlockSpec((TM, n), lambda i, grp, rb: (i, 0))
    return pl.pallas_call(
        _in_even_kernel,
        out_shape=(row(512, BF16), row(128, BF16), row(128, BF16), row(128, BF16), row(128, BF16),
                   row(512, F32), row(512, F32)),
        grid_spec=pltpu.PrefetchScalarGridSpec(
            num_scalar_prefetch=2,
            grid=(NT_ALL,),
            in_specs=[tile(D),
                      pl.BlockSpec((None, 6, D), lambda i, grp, rb: (grp[i], 0, 0)),
                      pl.BlockSpec((1, D), lambda i, grp, rb: (0, 0)),
                      pl.BlockSpec((D, 2304), lambda i, grp, rb: (0, 0)),
                      pl.BlockSpec((TM, 128), lambda i, grp, rb: (rb[i], 0)),
                      pl.BlockSpec((TM, 128), lambda i, grp, rb: (rb[i], 0))],
            out_specs=(tile(512), tile(128), tile(128), tile(128), tile(128), tile(512), tile(512)),
        ),
        compiler_params=_cparams(("arbitrary",)),
        name="in_proj_even",
    )(jnp.asarray(grp), jnp.asarray(rblk), xall, mod, g, w_bf, cos_f, sin_f)


def _in_odd(xall, mod, g, w_bf):
    grp, _, _, _ = _tile_tables()
    row = lambda n, dt: jax.ShapeDtypeStruct((N_ALL, n), dt)
    tile = lambda n: pl.BlockSpec((TM, n), lambda i, grp: (i, 0))
    return pl.pallas_call(
        _in_odd_kernel,
        out_shape=(row(512, BF16), row(512, BF16), row(512, BF16), row(512, BF16)),
        grid_spec=pltpu.PrefetchScalarGridSpec(
            num_scalar_prefetch=1,
            grid=(NT_ALL,),
            in_specs=[tile(D),
                      pl.BlockSpec((None, 6, D), lambda i, grp: (grp[i], 0, 0)),
                      pl.BlockSpec((1, D), lambda i, grp: (0, 0)),
                      pl.BlockSpec((D, 2048), lambda i, grp: (0, 0))],
            out_specs=(tile(512), tile(512), tile(512), tile(512)),
        ),
        compiler_params=_cparams(("arbitrary",)),
        name="in_proj_odd",
    )(jnp.asarray(grp), xall, mod, g, w_bf)


def _nt(a, b):
    return lax.dot_general(a, b, (((1,), (1,)), ((), ())), preferred_element_type=F32)


def _half_mask(shape, half):
    lane = lax.broadcasted_iota(I32, shape, 1)
    return (lane < 64) if half == 0 else (lane >= 64)


def _win_kernel(sink_ref, q_ref, k_ref, ks_ref, v_ref, vs_ref, kc_ref, ksc_ref, vc_ref, vsc_ref, o_ref):
    n = pl.program_id(1)
    start = pl.multiple_of(jnp.clip((n - 1) * 128, 0, SEQ - 384), 128)
    win = pl.ds(start, 384)
    row = lax.broadcasted_iota(I32, (256, 384), 0)
    col = lax.broadcasted_iota(I32, (256, 384), 1)
    valid = jnp.abs((n * 128 + row % 128) - (start + col)) <= 128
    first = lax.broadcasted_iota(I32, (256, 1), 0) < 128
    kk = (jnp.concatenate([k_ref[win, :], kc_ref[...]], axis=0),
          jnp.concatenate([ks_ref[win, :], ksc_ref[...]], axis=0))
    vv = (jnp.concatenate([v_ref[win, :], vc_ref[...]], axis=0),
          jnp.concatenate([vs_ref[win, :], vsc_ref[...]], axis=0))
    outs = {}
    for hk in range(2):
        for hf in range(2):
            swapped = 0 if hk == hf else 1
            chunks = (2 * hk, 2 * hk + 1)
            qs = [q_ref[:, c * 128:(c + 1) * 128] for c in chunks]
            q2 = jnp.concatenate([jnp.where(_half_mask(t.shape, hf), t, jnp.zeros_like(t)) for t in qs],
                                 axis=0)
            s = _nt(q2, kk[swapped])
            s_loc = jnp.where(valid, s[:, :384], -jnp.inf)
            s_ctx = s[:, 384:]
            sink = jnp.where(first, sink_ref[2 * chunks[0] + hf], sink_ref[2 * chunks[1] + hf])
            m = jnp.maximum(jnp.maximum(jnp.max(s_loc, axis=1, keepdims=True),
                                        jnp.max(s_ctx, axis=1, keepdims=True)), sink)
            p_loc = jnp.exp(s_loc - m)
            p_ctx = jnp.exp(s_ctx - m)
            den = (jnp.sum(p_loc, axis=1, keepdims=True) + jnp.sum(p_ctx, axis=1, keepdims=True)
                   + jnp.exp(sink - m))
            p = jnp.concatenate([p_loc, p_ctx], axis=1).astype(BF16)
            o = jnp.dot(p, vv[swapped], preferred_element_type=F32) / den
            outs[(chunks[0], hf)] = o[:128]
            outs[(chunks[1], hf)] = o[128:]
    for c in range(4):
        o_ref[:, c * 128:(c + 1) * 128] = jnp.where(_half_mask((128, 128), 0),
                                                    outs[(c, 0)], outs[(c, 1)]).astype(BF16)


def _win_attn(sink, q, k, ks, v, vs):
    nb = SEQ // 128
    seq_spec = pl.BlockSpec((SEQ, 128), lambda b, n: (b, 0))
    ctx_spec = pl.BlockSpec((CTX, 128), lambda b, n: (N_MAIN // CTX + b, 0))
    return pl.pallas_call(
        _win_kernel,
        out_shape=jax.ShapeDtypeStruct((N_MAIN, 512), BF16),
        grid=(BATCH, nb),
        in_specs=[pl.BlockSpec(memory_space=pltpu.SMEM),
                  pl.BlockSpec((128, 512), lambda b, n: (b * (SEQ // 128) + n, 0)),
                  seq_spec, seq_spec, seq_spec, seq_spec,
                  ctx_spec, ctx_spec, ctx_spec, ctx_spec],
        out_specs=pl.BlockSpec((128, 512), lambda b, n: (b * (SEQ // 128) + n, 0)),
        compiler_params=_cparams(("arbitrary", "arbitrary")),
        name="window_attn",
    )(sink, q, k, ks, v, vs, k, ks, v, vs)


def _ctx_attn_kernel(sink_ref, q_ref, k_ref, ks_ref, v_ref, vs_ref, o_ref):
    kk = (k_ref[...], ks_ref[...])
    vv = (v_ref[...], vs_ref[...])
    for c in range(4):
        qc = q_ref[:, c * 128:(c + 1) * 128]
        halves = []
        for hf in range(2):
            h = 2 * c + hf
            swapped = 0 if (h // 4) == hf else 1
            qm = jnp.where(_half_mask(qc.shape, hf), qc, jnp.zeros_like(qc))
            s = _nt(qm, kk[swapped])
            sink = sink_ref[h]
            m = jnp.maximum(jnp.max(s, axis=1, keepdims=True), sink)
            p = jnp.exp(s - m)
            den = jnp.sum(p, axis=1, keepdims=True) + jnp.exp(sink - m)
            halves.append(jnp.dot(p.astype(BF16), vv[swapped], preferred_element_type=F32) / den)
        o_ref[:, c * 128:(c + 1) * 128] = jnp.where(_half_mask(halves[0].shape, 0),
                                                    halves[0], halves[1]).astype(BF16)


def _ctx_attn(sink, q, k, ks, v, vs):
    ctx_spec = lambda n: pl.BlockSpec((CTX, n), lambda b: (N_MAIN // CTX + b, 0))
    return pl.pallas_call(
        _ctx_attn_kernel,
        out_shape=jax.ShapeDtypeStruct((N_CTX, 512), BF16),
        grid=(BATCH,),
        in_specs=[pl.BlockSpec(memory_space=pltpu.SMEM),
                  ctx_spec(512), ctx_spec(128), ctx_spec(128), ctx_spec(128), ctx_spec(128)],
        out_specs=pl.BlockSpec((CTX, 512), lambda b: (b, 0)),
        compiler_params=_cparams(("arbitrary",)),
        name="context_attn",
    )(sink, q, k, ks, v, vs)


NA_GROUP = 8
N_GRID_ROWS = SEQ // GRID_W


def _na_kernel(q_ref, k_ref, v_ref, kc_ref, vc_ref, nb_ref, o_ref):
    g = pl.program_id(1)

    def body(i, carry):
        r = g * NA_GROUP + i
        r0 = jnp.clip(r - NA_ROWS // 2, 0, N_GRID_ROWS - NA_ROWS)
        shift = r0 - r + NA_ROWS - 1
        qrows = pl.ds(pl.multiple_of(i * GRID_W, GRID_W), GRID_W)
        krows = pl.ds(pl.multiple_of(r0 * GRID_W, GRID_W), NA_ROWS * GRID_W)
        n_loc = NA_ROWS * GRID_W
        for c in range(4):
            lanes = slice(c * 128, (c + 1) * 128)
            qc = q_ref[qrows, lanes]
            q2 = jnp.concatenate([jnp.where(_half_mask(qc.shape, hf), qc, jnp.zeros_like(qc))
                                  for hf in range(2)], axis=0)
            kcat = jnp.concatenate([k_ref[krows, lanes], kc_ref[:, lanes]], axis=0)
            vcat = jnp.concatenate([v_ref[krows, lanes], vc_ref[:, lanes]], axis=0)
            s = _nt(q2, kcat)
            bias = jnp.concatenate([nb_ref[2 * c, shift], nb_ref[2 * c + 1, shift]], axis=0)
            s_loc = s[:, :n_loc] + bias
            s_ctx = s[:, n_loc:]
            m = jnp.maximum(jnp.max(s_loc, axis=1, keepdims=True), jnp.max(s_ctx, axis=1, keepdims=True))
            p_loc = jnp.exp(s_loc - m)
            p_ctx = jnp.exp(s_ctx - m)
            den = jnp.sum(p_loc, axis=1, keepdims=True) + jnp.sum(p_ctx, axis=1, keepdims=True)
            p = jnp.concatenate([p_loc, p_ctx], axis=1).astype(BF16)
            o = jnp.dot(p, vcat, preferred_element_type=F32) / den
            o_ref[qrows, lanes] = jnp.where(_half_mask(qc.shape, 0), o[:GRID_W], o[GRID_W:]).astype(BF16)
        return carry

    lax.fori_loop(0, NA_GROUP, body, 0)


def _na_bias(rpb):
    col = np.arange(GRID_W)
    c0 = np.clip(col - NA_COLS // 2, 0, GRID_W - NA_COLS)
    col_ok = (col[None, :] >= c0[:, None]) & (col[None, :] < c0[:, None] + NA_COLS)
    dc = np.clip(col[None, :] - col[:, None] + NA_COLS - 1, 0, 2 * NA_COLS - 2)
    onehot = (dc[None] == np.arange(2 * NA_COLS - 1)[:, None, None]).astype(np.float32)
    e = jnp.einsum('hrd,dqk->hrqk', rpb.astype(F32), jnp.asarray(onehot),
                   precision=lax.Precision.HIGHEST)
    e = jnp.where(col_ok[None, None], e, -jnp.inf)
    b = jnp.stack([e[:, s:s + NA_ROWS] for s in range(NA_ROWS)], axis=1)
    b = jnp.transpose(b, (0, 1, 3, 2, 4))
    return b.reshape(rpb.shape[0], NA_ROWS, GRID_W, NA_ROWS * GRID_W)


def _na_attn(q, k, v, nb):
    qrows = NA_GROUP * GRID_W
    n_g = SEQ // qrows
    seq_spec = pl.BlockSpec((SEQ, 512), lambda b, g: (b, 0))
    ctx_spec = pl.BlockSpec((CTX, 512), lambda b, g: (N_MAIN // CTX + b, 0))
    return pl.pallas_call(
        _na_kernel,
        out_shape=jax.ShapeDtypeStruct((N_MAIN, 512), BF16),
        grid=(BATCH, n_g),
        in_specs=[pl.BlockSpec((qrows, 512), lambda b, g: (b * n_g + g, 0)),
                  seq_spec, seq_spec, ctx_spec, ctx_spec,
                  pl.BlockSpec(nb.shape, lambda b, g: (0, 0, 0, 0))],
        out_specs=pl.BlockSpec((qrows, 512), lambda b, g: (b * n_g + g, 0)),
        compiler_params=_cparams(("arbitrary", "arbitrary")),
        name="neighborhood_attn",
    )(q, k, v, k, v, nb)


F_N2_CHUNK = 8
F_K1_CHUNK = 8


def _four1_kernel(x_ref, w_ref, t_ref):
    w = w_ref[...]
    for j in range(F_N2_CHUNK):
        res = jnp.dot(w, x_ref[:, j * 512:(j + 1) * 512], preferred_element_type=F32)
        t_ref[0, j] = res[:64].astype(BF16)
        t_ref[1, j] = res[64:].astype(BF16)


def _four2_kernel(t_ref, m_ref, cs_ref, y_ref):
    cs = cs_ref[...]
    for j in range(F_K1_CHUNK):
        lanes = slice(j * 512, (j + 1) * 512)
        tt = jnp.concatenate([t_ref[0, :, lanes], t_ref[1, :, lanes]], axis=0)
        pp = jnp.dot(m_ref[j], tt, preferred_element_type=F32)
        pc = jnp.concatenate([pp[:64], pp[64:]], axis=1).astype(BF16)
        y_ref[:, lanes] = jnp.dot(pc, cs, preferred_element_type=F32).astype(BF16)


def _fourier_tables():
    a = np.arange(64)
    ang1 = 2.0 * np.pi * np.outer(a, a) / 64.0
    w1 = np.concatenate([np.cos(ang1), -np.sin(ang1)], axis=0)
    k1 = a[:, None, None]
    k2 = a[None, :, None]
    n2 = a[None, None, :]
    theta = 2.0 * np.pi * (n2 * k2 / 64.0 + n2 * k1 / 4096.0)
    mr = np.cos(theta) / 64.0
    mi = -np.sin(theta) / 64.0
    m = np.concatenate([np.concatenate([mr, -mi], axis=2),
                        np.concatenate([mi, mr], axis=2)], axis=1)
    c = np.arange(128)
    angc = 2.0 * np.pi * np.outer(c, c) / 128.0
    eye4 = np.eye(4)
    cc = np.kron(eye4, np.cos(angc)) / np.sqrt(128.0)
    sc = np.kron(eye4, np.sin(angc)) / np.sqrt(128.0)
    cs = np.concatenate([cc, sc], axis=0)
    return tuple(jnp.asarray(t, F32).astype(BF16) for t in (w1, m, cs))


def _fourier(f):
    w1, m, cs = _fourier_tables()
    fv = f.reshape(N_ALL // 64, 64 * 512)
    n_c = 64 // F_N2_CHUNK
    t = pl.pallas_call(
        _four1_kernel,
        out_shape=jax.ShapeDtypeStruct((BATCH, 2, 64, 64, 512), BF16),
        grid=(BATCH, n_c),
        in_specs=[pl.BlockSpec((64, F_N2_CHUNK * 512), lambda b, c: (b, c)),
                  pl.BlockSpec((128, 64), lambda b, c: (0, 0))],
        out_specs=pl.BlockSpec((None, 2, F_N2_CHUNK, 64, 512), lambda b, c: (b, 0, c, 0, 0)),
        compiler_params=_cparams(("arbitrary", "arbitrary")),
        name="fourier_rows",
    )(fv, w1)
    n_k = 64 // F_K1_CHUNK
    t2 = t.reshape(BATCH, 2, 64, 64 * 512)
    y = pl.pallas_call(
        _four2_kernel,
        out_shape=jax.ShapeDtypeStruct((BATCH * 64, 64 * 512), BF16),
        grid=(BATCH, n_k),
        in_specs=[pl.BlockSpec((None, 2, 64, F_K1_CHUNK * 512), lambda b, c: (b, 0, 0, c)),
                  pl.BlockSpec((F_K1_CHUNK, 128, 128), lambda b, c: (c, 0, 0)),
                  pl.BlockSpec((1024, 512), lambda b, c: (0, 0))],
        out_specs=pl.BlockSpec((64, F_K1_CHUNK * 512), lambda b, c: (b, c)),
        compiler_params=_cparams(("arbitrary", "arbitrary")),
        name="fourier_cols",
    )(t2, m, cs)
    return y.reshape(N_MAIN, 512)


def _route(h2, rwt_ref, rb_ref, carry_ref, te_ref, tw_ref, rk_ref, cnt_ref):
    logits = lax.dot_general(rwt_ref[...], h2, (((1,), (1,)), ((), ())),
                             preferred_element_type=F32,
                             precision=lax.Precision.HIGHEST) + rb_ref[...]
    eidx = lax.broadcasted_iota(I32, logits.shape, 0)
    vals = logits
    sels, tops, idxs = [], [], []
    for _ in range(TOP_K):
        m = jnp.max(vals, axis=0, keepdims=True)
        idx = jnp.min(jnp.where(vals == m, eidx, N_EXPERTS), axis=0, keepdims=True)
        sel = eidx == idx
        sels.append(sel)
        tops.append(m)
        idxs.append(idx)
        vals = jnp.where(sel, -jnp.inf, vals)
    ex = [jnp.exp(t - tops[0]) for t in tops]
    den = ex[0] + ex[1] + ex[2] + ex[3]
    onehot = jnp.zeros(logits.shape, F32)
    for sel in sels:
        onehot = onehot + sel.astype(F32)
    r_i = lax.broadcasted_iota(I32, (TM, TM), 0)
    c_i = lax.broadcasted_iota(I32, (TM, TM), 1)
    upper = (r_i < c_i).astype(BF16)
    prefix = jnp.dot(onehot.astype(BF16), upper, preferred_element_type=F32)
    base = carry_ref[:, 0:1] + prefix
    for k in range(TOP_K):
        te_ref[k:k + 1, :] = idxs[k]
        tw_ref[k:k + 1, :] = ex[k] / den
        rk_ref[k:k + 1, :] = jnp.sum(jnp.where(sels[k], base, 0.0), axis=0, keepdims=True).astype(I32)
    new_carry = carry_ref[...] + jnp.sum(onehot, axis=1, keepdims=True)
    carry_ref[...] = new_carry
    cnt_ref[...] = new_carry


def _out_tail(i, x, y, mod_ref, g_ref, rwt_ref, rb_ref, carry_ref,
              xn_ref, h2_ref, te_ref, tw_ref, rk_ref, cnt_ref):
    @pl.when(i == 0)
    def _():
        carry_ref[...] = jnp.zeros_like(carry_ref)

    xn = x + mod_ref[2:3, :] * y
    xn_ref[...] = xn
    h2 = _rms_mod(xn, g_ref[...], mod_ref[4:5, :], mod_ref[3:4, :])
    h2_ref[...] = h2
    _route(h2, rwt_ref, rb_ref, carry_ref, te_ref, tw_ref, rk_ref, cnt_ref)


def _out_even_kernel(grp_ref, first_ref, last_ref,
                     x_ref, a_ref, gb_ref, z_ref, zp_ref, zn_ref, cw_ref, w_ref, mod_ref, g_ref,
                     rwt_ref, rb_ref,
                     xn_ref, h2_ref, te_ref, tw_ref, rk_ref, cnt_ref, carry_ref):
    i = pl.program_id(0)
    z = z_ref[...]
    zprev = jnp.where(first_ref[i] == 1, 0.0, zp_ref[7:8, :])
    znext = jnp.where(last_ref[i] == 1, 0.0, zn_ref[0:1, :])
    rid = lax.broadcasted_iota(I32, z.shape, 0)
    zm1 = jnp.where(rid == 0, zprev, pltpu.roll(z, 1, 0))
    zp1 = jnp.where(rid == TM - 1, znext, pltpu.roll(z, TM - 1, 0))
    conv = gb_ref[...] * (zm1 * cw_ref[0:1, :] + z * cw_ref[1:2, :] + zp1 * cw_ref[2:3, :])
    y = (jnp.dot(a_ref[...], w_ref[0:512, :], preferred_element_type=F32)
         + jnp.dot(conv.astype(BF16), w_ref[512:1024, :], preferred_element_type=F32))
    _out_tail(i, x_ref[...], y, mod_ref, g_ref, rwt_ref, rb_ref, carry_ref,
              xn_ref, h2_ref, te_ref, tw_ref, rk_ref, cnt_ref)


def _out_odd_kernel(grp_ref, x_ref, a_ref, f_ref, w_ref, mod_ref, g_ref, rwt_ref, rb_ref,
                    xn_ref, h2_ref, te_ref, tw_ref, rk_ref, cnt_ref, carry_ref):
    i = pl.program_id(0)
    y = (jnp.dot(a_ref[...], w_ref[0:512, :], preferred_element_type=F32)
         + jnp.dot(f_ref[...], w_ref[512:1024, :], preferred_element_type=F32))
    _out_tail(i, x_ref[...], y, mod_ref, g_ref, rwt_ref, rb_ref, carry_ref,
              xn_ref, h2_ref, te_ref, tw_ref, rk_ref, cnt_ref)


def _out_shapes(n_rows):
    return (jax.ShapeDtypeStruct((n_rows, D), F32), jax.ShapeDtypeStruct((n_rows, D), F32),
            jax.ShapeDtypeStruct((TOP_K, n_rows), I32), jax.ShapeDtypeStruct((TOP_K, n_rows), F32),
            jax.ShapeDtypeStruct((TOP_K, n_rows), I32), jax.ShapeDtypeStruct((N_EXPERTS, 128), F32))


def _out_even(xall, attn, gb, z, conv_w, w_bf, mod, g, rwt, rb):
    grp, _, first, last = _tile_tables()
    n_rows = N_ALL
    zblocks = n_rows // 8
    im = lambda f: (lambda i, grp, fi, la: f(i))
    tile = lambda n: pl.BlockSpec((TM, n), im(lambda i: (i, 0)))
    const = lambda shape: pl.BlockSpec(shape, im(lambda i: (0,) * len(shape)))
    tk = pl.BlockSpec((TOP_K, TM), im(lambda i: (0, i)))
    return pl.pallas_call(
        _out_even_kernel,
        out_shape=_out_shapes(n_rows),
        grid_spec=pltpu.PrefetchScalarGridSpec(
            num_scalar_prefetch=3,
            grid=(NT_ALL,),
            in_specs=[tile(D), tile(512), tile(512), tile(512),
                      pl.BlockSpec((8, 512), im(lambda i: (jnp.maximum(i * (TM // 8) - 1, 0), 0))),
                      pl.BlockSpec((8, 512), im(lambda i: (jnp.minimum((i + 1) * (TM // 8), zblocks - 1), 0))),
                      const((3, 512)), const((D, D)),
                      pl.BlockSpec((None, 6, D), lambda i, grp, fi, la: (grp[i], 0, 0)),
                      const((1, D)), const((N_EXPERTS, D)), const((N_EXPERTS, 1))],
            out_specs=(tile(D), tile(D), tk, tk, tk, const((N_EXPERTS, 128))),
            scratch_shapes=[pltpu.VMEM((N_EXPERTS, 128), F32)],
        ),
        compiler_params=_cparams(("arbitrary",)),
        name="out_proj_even",
    )(jnp.asarray(grp), jnp.asarray(first), jnp.asarray(last),
      xall, attn, gb, z, z, z, conv_w, w_bf, mod, g, rwt, rb)


def _out_odd(xall, attn, four, w_bf, mod, g, rwt, rb):
    grp, _, _, _ = _tile_tables()
    n_rows = N_MAIN
    im = lambda f: (lambda i, grp: f(i))
    tile = lambda n: pl.BlockSpec((TM, n), im(lambda i: (i, 0)))
    const = lambda shape: pl.BlockSpec(shape, im(lambda i: (0,) * len(shape)))
    tk = pl.BlockSpec((TOP_K, TM), im(lambda i: (0, i)))
    return pl.pallas_call(
        _out_odd_kernel,
        out_shape=_out_shapes(n_rows),
        grid_spec=pltpu.PrefetchScalarGridSpec(
            num_scalar_prefetch=1,
            grid=(NT_MAIN,),
            in_specs=[tile(D), tile(512), tile(512), const((D, D)),
                      pl.BlockSpec((None, 6, D), lambda i, grp: (grp[i], 0, 0)),
                      const((1, D)), const((N_EXPERTS, D)), const((N_EXPERTS, 1))],
            out_specs=(tile(D), tile(D), tk, tk, tk, const((N_EXPERTS, 128))),
            scratch_shapes=[pltpu.VMEM((N_EXPERTS, 128), F32)],
        ),
        compiler_params=_cparams(("arbitrary",)),
        name="out_proj_odd",
    )(jnp.asarray(grp), xall, attn, four, w_bf, mod, g, rwt, rb)


def _moe_plan(counts_f, top_e_t, rank_t, n_tok):
    counts = counts_f[:, 0].astype(I32)
    padded = (counts + TMM - 1) // TMM * TMM
    e_i = jnp.arange(N_EXPERTS, dtype=I32)
    incl = e_i[None, :] <= e_i[:, None]
    pad_end = jnp.sum(jnp.where(incl, padded[None, :], 0), axis=1)
    pad_start = pad_end - padded
    sel = top_e_t[None] == e_i[:, None, None]
    dest = jnp.sum(jnp.where(sel, pad_start[:, None, None], 0), axis=0) + rank_t
    n_blocks = n_tok * TOP_K // TMM + N_EXPERTS
    blk_start = jnp.arange(n_blocks, dtype=I32) * TMM
    block_e = jnp.minimum(jnp.sum((blk_start[:, None] >= pad_end[None, :]).astype(I32), axis=1),
                          N_EXPERTS - 1)
    n_used = (pad_end[-1] // TMM).reshape(1)
    n_tiles = n_tok // TT
    dest_tiles = dest.reshape(TOP_K, n_tiles, TT).transpose(1, 0, 2).reshape(n_tiles, TOP_K * TT)
    return dest_tiles, block_e, n_used, pad_end, padded


def _dispatch_kernel(pe_ref, pd_ref, nu_ref, dst_hbm, h2_ref, xs_hbm, dst_s, zbuf, sem_i, sem, sem_z):
    i = pl.program_id(0)
    n_blocks = xs_hbm.shape[0] // TMM

    @pl.when(i == 0)
    def _():
        zbuf[...] = jnp.zeros_like(zbuf)

        def fill(b):
            return pltpu.make_async_copy(zbuf, xs_hbm.at[pl.ds(pl.multiple_of(b * TMM, TMM), TMM)], sem_z)

        def fill_expert(e, n):
            has_rows = pd_ref[e] > 0

            @pl.when(has_rows)
            def _():
                fill(pe_ref[e] // TMM - 1).start()

            return n + has_rows.astype(I32)

        def fill_tail(b, c):
            fill(b).start()
            return c

        def fill_wait(j, c):
            fill(0).wait()
            return c

        n_fill = lax.fori_loop(0, N_EXPERTS, fill_expert, 0) + n_blocks - nu_ref[0]
        lax.fori_loop(nu_ref[0], n_blocks, fill_tail, 0)
        lax.fori_loop(0, n_fill, fill_wait, 0)

    ci = pltpu.make_async_copy(dst_hbm.at[i], dst_s, sem_i)
    ci.start()
    ci.wait()

    def issue(j, c):
        for k in range(TOP_K):
            pltpu.make_async_copy(h2_ref.at[pl.ds(j, 1)], xs_hbm.at[pl.ds(dst_s[k * TT + j], 1)],
                                  sem).start(priority=k % 2)
        return c

    lax.fori_loop(0, TT, issue, 0, unroll=4)
    for _ in range(TOP_K):
        pltpu.make_async_copy(h2_ref, xs_hbm.at[pl.ds(0, TT)], sem).wait()


def _dispatch(h2, dest_tiles, pad_end, padded, n_used, cap):
    n_tiles = dest_tiles.shape[0]
    return pl.pallas_call(
        _dispatch_kernel,
        out_shape=jax.ShapeDtypeStruct((cap, D), F32),
        grid_spec=pltpu.PrefetchScalarGridSpec(
            num_scalar_prefetch=3,
            grid=(n_tiles,),
            in_specs=[pl.BlockSpec(memory_space=pl.ANY),
                      pl.BlockSpec((TT, D), lambda i, pe, pd, nu: (i, 0))],
            out_specs=pl.BlockSpec(memory_space=pl.ANY),
            scratch_shapes=[pltpu.SMEM((TOP_K * TT,), I32), pltpu.VMEM((TMM, D), F32),
                            pltpu.SemaphoreType.DMA, pltpu.SemaphoreType.DMA, pltpu.SemaphoreType.DMA],
        ),
        compiler_params=_cparams(("arbitrary",)),
        name="moe_dispatch",
    )(pad_end, padded, n_used, dest_tiles, h2)


def _moe_kernel(be_ref, nu_ref, x_ref, wgu_ref, bgu_ref, wdn_ref, bdn_ref, y_ref, wgu_bf, wdn_bf):
    i = pl.program_id(0)
    prev = be_ref[jnp.maximum(i - 1, 0)]

    @pl.when((i < nu_ref[0]) & ((i == 0) | (be_ref[i] != prev)))
    def _():
        wgu_bf[...] = wgu_ref[...].astype(BF16)
        wdn_bf[...] = wdn_ref[...].astype(BF16)

    @pl.when(i < nu_ref[0])
    def _():
        xb = x_ref[...].astype(BF16)
        gu = jnp.dot(xb, wgu_bf[...], preferred_element_type=F32) + bgu_ref[...]
        gate = jnp.minimum(gu[:, :D], SWIGLU_LIMIT)
        up = jnp.clip(gu[:, D:], -SWIGLU_LIMIT, SWIGLU_LIMIT)
        act = (up + 1.0) * (gate * (1.0 / (1.0 + jnp.exp(-SWIGLU_ALPHA * gate))))
        y_ref[...] = jnp.dot(act.astype(BF16), wdn_bf[...], preferred_element_type=F32) + bdn_ref[...]

    @pl.when(i >= nu_ref[0])
    def _():
        y_ref[...] = jnp.zeros_like(y_ref)


def _moe(layer, xs, block_e, n_used, w_gu, b_gu, w_dn, b_dn):
    n_blocks = block_e.shape[0]
    n_l = w_gu.shape[0]
    blk = lambda i, be, nu: (jnp.minimum(i, nu[0] - 1), 0)
    out_blk = lambda i, be, nu: (i, 0)
    exp4 = lambda i, be, nu: (layer, be[jnp.minimum(i, nu[0] - 1)], 0, 0)
    return pl.pallas_call(
        _moe_kernel,
        out_shape=jax.ShapeDtypeStruct((n_blocks * TMM, D), F32),
        grid_spec=pltpu.PrefetchScalarGridSpec(
            num_scalar_prefetch=2,
            grid=(n_blocks,),
            in_specs=[pl.BlockSpec((TMM, D), blk),
                      pl.BlockSpec((None, None, D, 2 * D), exp4),
                      pl.BlockSpec((None, None, 1, 2 * D), exp4),
                      pl.BlockSpec((None, None, D, D), exp4),
                      pl.BlockSpec((None, None, 1, D), exp4)],
            out_specs=pl.BlockSpec((TMM, D), out_blk),
            scratch_shapes=[pltpu.VMEM((D, 2 * D), BF16), pltpu.VMEM((D, D), BF16)],
        ),
        compiler_params=_cparams(("arbitrary",)),
        name="moe_experts",
    )(block_e, n_used, xs, w_gu, b_gu.reshape(n_l, N_EXPERTS, 1, 2 * D), w_dn,
      b_dn.reshape(n_l, N_EXPERTS, 1, D))


def _combine_kernel(final, grp_ref, dst_hbm, ys_hbm, x_ref, tw_ref, mod_ref, fn_ref, o_ref,
                    dst_s, buf, sem_i, sem):
    i = pl.program_id(0)
    ci = pltpu.make_async_copy(dst_hbm.at[i], dst_s, sem_i)
    ci.start()
    ci.wait()

    def issue(j, c):
        for k in range(TOP_K):
            pltpu.make_async_copy(ys_hbm.at[pl.ds(dst_s[k * TT + j], 1)], buf.at[k, pl.ds(j, 1)],
                                  sem).start(priority=k % 2)
        return c

    lax.fori_loop(0, TT, issue, 0, unroll=4)
    for k in range(TOP_K):
        pltpu.make_async_copy(ys_hbm.at[pl.ds(0, TT)], buf.at[k], sem).wait()
    tw = tw_ref[...]
    acc = tw[:, 0:1] * buf[0]
    for k in range(1, TOP_K):
        acc = acc + tw[:, k:k + 1] * buf[k]
    out = x_ref[...] + mod_ref[5:6, :] * acc
    if final:
        ms = jnp.mean(out * out, axis=-1, keepdims=True)
        out = out * lax.rsqrt(ms + EPS) * fn_ref[...]
    o_ref[...] = out


def _combine(xn, ys, dest_tiles, top_w_t, mod, final_norm, n_tok, final):
    grp, _, _, _ = _tile_tables()
    n_tiles = n_tok // TT
    tw = top_w_t.T
    return pl.pallas_call(
        functools.partial(_combine_kernel, final),
        out_shape=jax.ShapeDtypeStruct((n_tok, D), F32),
        grid_spec=pltpu.PrefetchScalarGridSpec(
            num_scalar_prefetch=1,
            grid=(n_tiles,),
            in_specs=[pl.BlockSpec(memory_space=pl.ANY), pl.BlockSpec(memory_space=pl.ANY),
                      pl.BlockSpec((TT, D), lambda i, grp: (i, 0)),
                      pl.BlockSpec((TT, TOP_K), lambda i, grp: (i, 0)),
                      pl.BlockSpec((None, 6, D), lambda i, grp: (grp[i], 0, 0)),
                      pl.BlockSpec((1, D), lambda i, grp: (0, 0))],
            out_specs=pl.BlockSpec((TT, D), lambda i, grp: (i, 0)),
            scratch_shapes=[pltpu.SMEM((TOP_K * TT,), I32), pltpu.VMEM((TOP_K, TT, D), F32),
                            pltpu.SemaphoreType.DMA, pltpu.SemaphoreType.DMA],
        ),
        compiler_params=_cparams(("arbitrary",)),
        name="moe_combine",
    )(jnp.asarray(grp), dest_tiles, ys, xn, tw, mod, final_norm.reshape(1, D))


def _moe_layer(layer, xn, h2, top_e_t, top_w_t, rank_t, counts, mod, w_gu, b_gu, w_dn, b_dn, final_norm, final):
    n_tok = xn.shape[0]
    dest_tiles, block_e, n_used, pad_end, padded = _moe_plan(counts, top_e_t, rank_t, n_tok)
    xs = _dispatch(h2, dest_tiles, pad_end, padded, n_used, block_e.shape[0] * TMM)
    ys = _moe(layer, xs, block_e, n_used, w_gu, b_gu, w_dn, b_dn)
    return _combine(xn, ys, dest_tiles, top_w_t, mod, final_norm, n_tok, final)


def kernel(x, c, ctx, c_ctx, ada_w, ada_b, norm_mix, norm_ffn, even_w_in, even_w_out, even_conv_w, even_sink, odd_w_in, odd_w_out, odd_rpb, router_w, router_b, moe_w_gu, moe_b_gu, moe_w_dn, moe_b_dn, final_norm):
    xall = jnp.concatenate([x.reshape(N_MAIN, D), ctx.reshape(N_CTX, D)], axis=0)
    cc = jnp.concatenate([c, c_ctx[None, :], jnp.zeros((3, D), F32)], axis=0)
    mod = _ada(cc, ada_w, ada_b).reshape(2, 8, 6, D)
    cos_f, sin_f = _rope_tables()

    q, k, ks, v, vs, gb, z = _in_even(xall, mod[0], norm_mix[0:1], even_w_in[0].astype(BF16), cos_f, sin_f)
    attn = jnp.concatenate([_win_attn(even_sink[0], q, k, ks, v, vs),
                            _ctx_attn(even_sink[0], q, k, ks, v, vs)], axis=0)
    xn, h2, te, tw, rk, cnt = _out_even(xall, attn, gb, z, even_conv_w[0], even_w_out[0].astype(BF16),
                                        mod[0], norm_ffn[0:1], router_w[0].T, router_b[0][:, None])
    xall = _moe_layer(0, xn, h2, te, tw, rk, cnt, mod[0], moe_w_gu, moe_b_gu, moe_w_dn, moe_b_dn,
                      final_norm, False)

    q, k, v, f = _in_odd(xall, mod[1], norm_mix[1:2], odd_w_in[0].astype(BF16))
    attn = _na_attn(q, k, v, _na_bias(odd_rpb[0]))
    four = _fourier(f)
    xn, h2, te, tw, rk, cnt = _out_odd(xall, attn, four, odd_w_out[0].astype(BF16),
                                       mod[1], norm_ffn[1:2], router_w[1].T, router_b[1][:, None])
    out = _moe_layer(1, xn, h2, te, tw, rk, cnt, mod[1], moe_w_gu, moe_b_gu, moe_w_dn, moe_b_dn,
                     final_norm, True)
    return out.reshape(BATCH, SEQ, D)
```

```python
import functools

import numpy as np
import jax
import jax.numpy as jnp
from jax import lax
from jax.experimental import pallas as pl
from jax.experimental.pallas import tpu as pltpu

F32 = jnp.float32
BF16 = jnp.bfloat16
I32 = jnp.int32

D = 1024
BATCH = 4
SEQ = 4096
CTX = 256
GRID_W = 64
HEAD_DIM = 64
EPS = 1e-6
ROPE_THETA = 10000.0
N_EXPERTS = 32
TOP_K = 4
SWIGLU_LIMIT = 7.0
SWIGLU_ALPHA = 1.702
NA_ROWS = 8
NA_COLS = 16

N_MAIN = BATCH * SEQ
N_CTX = BATCH * CTX
N_ALL = N_MAIN + N_CTX
TM = 256
NT_MAIN = N_MAIN // TM
NT_ALL = N_ALL // TM
TILES_PER_SEQ = SEQ // TM
TMM = 512
TT = 256
VMEM_LIMIT = 56 * 1024 * 1024


def _cparams(sem, vmem=VMEM_LIMIT):
    return pltpu.CompilerParams(dimension_semantics=sem, vmem_limit_bytes=vmem)


def _rms_mod(x, g, sc, sh):
    ms = jnp.mean(x * x, axis=-1, keepdims=True)
    return (x * lax.rsqrt(ms + EPS) * g) * (1.0 + sc) + sh


def _ada_kernel(c_ref, w_ref, b_ref, o_ref):
    c = c_ref[...]
    s = c * (1.0 / (1.0 + jnp.exp(-c)))
    o_ref[...] = jnp.dot(s, w_ref[...], preferred_element_type=F32,
                         precision=lax.Precision.HIGHEST) + b_ref[...]


def _ada(cc, ada_w, ada_b):
    n_l = ada_w.shape[0]
    tn = 1024
    return pl.pallas_call(
        _ada_kernel,
        out_shape=jax.ShapeDtypeStruct((n_l, 8, 6 * D), F32),
        grid=(n_l, 6 * D // tn),
        in_specs=[pl.BlockSpec((8, D), lambda l, j: (0, 0)),
                  pl.BlockSpec((None, D, tn), lambda l, j: (l, 0, j)),
                  pl.BlockSpec((None, 1, tn), lambda l, j: (l, 0, j))],
        out_specs=pl.BlockSpec((None, 8, tn), lambda l, j: (l, 0, j)),
        compiler_params=_cparams(("arbitrary", "arbitrary")),
        name="ada_mod",
    )(cc, ada_w, ada_b.reshape(n_l, 1, 6 * D))


def _rope_apply(t, cos, sin):
    n = t.shape[1]
    lane = lax.broadcasted_iota(I32, t.shape, 1)
    fwd = pltpu.roll(t, n - 32, 1)
    bwd = pltpu.roll(t, 32, 1)
    rot = jnp.where((lane % 64) < 32, fwd, bwd)
    reps = n // 128
    cosf = jnp.concatenate([cos] * reps, axis=1) if reps > 1 else cos
    sinf = jnp.concatenate([sin] * reps, axis=1) if reps > 1 else sin
    return t * cosf + rot * sinf


def _in_even_kernel(grp_ref, rblk_ref, x_ref, mod_ref, g_ref, w_ref, cos_ref, sin_ref,
                    q_ref, k_ref, ks_ref, v_ref, vs_ref, gb_ref, z_ref):
    h = _rms_mod(x_ref[...], g_ref[...], mod_ref[1:2, :], mod_ref[0:1, :])
    p = jnp.dot(h.astype(BF16), w_ref[...], preferred_element_type=F32)
    cos = cos_ref[...]
    sin = sin_ref[...]
    q = _rope_apply(p[:, 0:512], cos, sin) * (HEAD_DIM ** -0.5)
    k = _rope_apply(p[:, 512:640], cos, sin)
    v = p[:, 640:768]
    q_ref[...] = q.astype(BF16)
    k_ref[...] = k.astype(BF16)
    ks_ref[...] = pltpu.roll(k, 64, 1).astype(BF16)
    v_ref[...] = v.astype(BF16)
    vs_ref[...] = pltpu.roll(v, 64, 1).astype(BF16)
    gb_ref[...] = p[:, 768:1280]
    z_ref[...] = p[:, 1280:1792] * p[:, 1792:2304]


def _in_odd_kernel(grp_ref, x_ref, mod_ref, g_ref, w_ref, q_ref, k_ref, v_ref, f_ref):
    h = _rms_mod(x_ref[...], g_ref[...], mod_ref[1:2, :], mod_ref[0:1, :])
    p = jnp.dot(h.astype(BF16), w_ref[...], preferred_element_type=F32)
    q_ref[...] = (p[:, 0:512] * (HEAD_DIM ** -0.5)).astype(BF16)
    k_ref[...] = p[:, 512:1024].astype(BF16)
    v_ref[...] = p[:, 1024:1536].astype(BF16)
    f_ref[...] = p[:, 1536:2048].astype(BF16)


def _tile_tables():
    t = np.arange(NT_ALL)
    main = t < NT_MAIN
    grp = np.where(main, t // TILES_PER_SEQ, BATCH).astype(np.int32)
    rblk = np.where(main, t % TILES_PER_SEQ, TILES_PER_SEQ).astype(np.int32)
    first = np.where(main, (t % TILES_PER_SEQ) == 0, True).astype(np.int32)
    last = np.where(main, (t % TILES_PER_SEQ) == TILES_PER_SEQ - 1, True).astype(np.int32)
    return grp, rblk, first, last


def _rope_tables():
    t = jnp.arange(SEQ, dtype=I32)
    row = (t // GRID_W).astype(F32)
    col = (t % GRID_W).astype(F32)
    n_freq = HEAD_DIM // 4
    inv_freq = jnp.power(ROPE_THETA, -jnp.arange(n_freq, dtype=F32) / n_freq)
    ang = jnp.concatenate([row[:, None] * inv_freq, col[:, None] * inv_freq], axis=-1)
    cos = jnp.cos(ang)
    sin = jnp.sin(ang)
    cos_f = jnp.concatenate([cos, cos, cos, cos], axis=1)
    sin_f = jnp.concatenate([-sin, sin, -sin, sin], axis=1)
    cos_f = jnp.concatenate([cos_f, jnp.ones((TM, 128), F32)], axis=0)
    sin_f = jnp.concatenate([sin_f, jnp.zeros((TM, 128), F32)], axis=0)
    return cos_f, sin_f


def _in_even(xall, mod, g, w_bf, cos_f, sin_f):
    grp, rblk, _, _ = _tile_tables()
    row = lambda n, dt: jax.ShapeDtypeStruct((N_ALL, n), dt)
    tile = lambda n: pl.BlockSpec((TM, n), lambda i, grp, rb: (i, 0))
    return pl.pallas_call(
        _in_even_kernel,
        out_shape=(row(512, BF16), row(128, BF16), row(128, BF16), row(128, BF16), row(128, BF16),
                   row(512, F32), row(512, F32)),
        grid_spec=pltpu.PrefetchScalarGridSpec(
            num_scalar_prefetch=2,
            grid=(NT_ALL,),
            in_specs=[tile(D),
                      pl.BlockSpec((None, 6, D), lambda i, grp, rb: (grp[i], 0, 0)),
                      pl.BlockSpec((1, D), lambda i, grp, rb: (0, 0)),
                      pl.BlockSpec((D, 2304), lambda i, grp, rb: (0, 0)),
                      pl.BlockSpec((TM, 128), lambda i, grp, rb: (rb[i], 0)),
                      pl.BlockSpec((TM, 128), lambda i, grp, rb: (rb[i], 0))],
            out_specs=(tile(512), tile(128), tile(128), tile(128), tile(128), tile(512), tile(512)),
        ),
        compiler_params=_cparams(("arbitrary",)),
        name="in_proj_even",
    )(jnp.asarray(grp), jnp.asarray(rblk), xall, mod, g, w_bf, cos_f, sin_f)


def _in_odd(xall, mod, g, w_bf):
    grp, _, _, _ = _tile_tables()
    row = lambda n, dt: jax.ShapeDtypeStruct((N_ALL, n), dt)
    tile = lambda n: pl.BlockSpec((TM, n), lambda i, grp: (i, 0))
    return pl.pallas_call(
        _in_odd_kernel,
        out_shape=(row(512, BF16), row(512, BF16), row(512, BF16), row(512, BF16)),
        grid_spec=pltpu.PrefetchScalarGridSpec(
            num_scalar_prefetch=1,
            grid=(NT_ALL,),
            in_specs=[tile(D),
                      pl.BlockSpec((None, 6, D), lambda i, grp: (grp[i], 0, 0)),
                      pl.BlockSpec((1, D), lambda i, grp: (0, 0)),
                      pl.BlockSpec((D, 2048), lambda i, grp: (0, 0))],
            out_specs=(tile(512), tile(512), tile(512), tile(512)),
        ),
        compiler_params=_cparams(("arbitrary",)),
        name="in_proj_odd",
    )(jnp.asarray(grp), xall, mod, g, w_bf)


def _nt(a, b):
    return lax.dot_general(a, b, (((1,), (1,)), ((), ())), preferred_element_type=F32)


def _half_mask(shape, half):
    lane = lax.broadcasted_iota(I32, shape, 1)
    return (lane < 64) if half == 0 else (lane >= 64)


def _win_kernel(sink_ref, q_ref, k_ref, ks_ref, v_ref, vs_ref, kc_ref, ksc_ref, vc_ref, vsc_ref, o_ref):
    n = pl.program_id(1)
    start = pl.multiple_of(jnp.clip((n - 1) * 128, 0, SEQ - 384), 128)
    win = pl.ds(start, 384)
    row = lax.broadcasted_iota(I32, (256, 384), 0)
    col = lax.broadcasted_iota(I32, (256, 384), 1)
    valid = jnp.abs((n * 128 + row % 128) - (start + col)) <= 128
    first = lax.broadcasted_iota(I32, (256, 1), 0) < 128
    kk = (jnp.concatenate([k_ref[win, :], kc_ref[...]], axis=0),
          jnp.concatenate([ks_ref[win, :], ksc_ref[...]], axis=0))
    vv = (jnp.concatenate([v_ref[win, :], vc_ref[...]], axis=0),
          jnp.concatenate([vs_ref[win, :], vsc_ref[...]], axis=0))
    outs = {}
    for hk in range(2):
        for hf in range(2):
            swapped = 0 if hk == hf else 1
            chunks = (2 * hk, 2 * hk + 1)
            qs = [q_ref[:, c * 128:(c + 1) * 128] for c in chunks]
            q2 = jnp.concatenate([jnp.where(_half_mask(t.shape, hf), t, jnp.zeros_like(t)) for t in qs],
                                 axis=0)
            s = _nt(q2, kk[swapped])
            s_loc = jnp.where(valid, s[:, :384], -jnp.inf)
            s_ctx = s[:, 384:]
            sink = jnp.where(first, sink_ref[2 * chunks[0] + hf], sink_ref[2 * chunks[1] + hf])
            m = jnp.maximum(jnp.maximum(jnp.max(s_loc, axis=1, keepdims=True),
                                        jnp.max(s_ctx, axis=1, keepdims=True)), sink)
            p_loc = jnp.exp(s_loc - m)
            p_ctx = jnp.exp(s_ctx - m)
            den = (jnp.sum(p_loc, axis=1, keepdims=True) + jnp.sum(p_ctx, axis=1, keepdims=True)
                   + jnp.exp(sink - m))
            p = jnp.concatenate([p_loc, p_ctx], axis=1).astype(BF16)
            o = jnp.dot(p, vv[swapped], preferred_element_type=F32) / den
            outs[(chunks[0], hf)] = o[:128]
            outs[(chunks[1], hf)] = o[128:]
    for c in range(4):
        o_ref[:, c * 128:(c + 1) * 128] = jnp.where(_half_mask((128, 128), 0),
                                                    outs[(c, 0)], outs[(c, 1)]).astype(BF16)


def _win_attn(sink, q, k, ks, v, vs):
    nb = SEQ // 128
    seq_spec = pl.BlockSpec((SEQ, 128), lambda b, n: (b, 0))
    ctx_spec = pl.BlockSpec((CTX, 128), lambda b, n: (N_MAIN // CTX + b, 0))
    return pl.pallas_call(
        _win_kernel,
        out_shape=jax.ShapeDtypeStruct((N_MAIN, 512), BF16),
        grid=(BATCH, nb),
        in_specs=[pl.BlockSpec(memory_space=pltpu.SMEM),
                  pl.BlockSpec((128, 512), lambda b, n: (b * (SEQ // 128) + n, 0)),
                  seq_spec, seq_spec, seq_spec, seq_spec,
                  ctx_spec, ctx_spec, ctx_spec, ctx_spec],
        out_specs=pl.BlockSpec((128, 512), lambda b, n: (b * (SEQ // 128) + n, 0)),
        compiler_params=_cparams(("arbitrary", "arbitrary")),
        name="window_attn",
    )(sink, q, k, ks, v, vs, k, ks, v, vs)


def _ctx_attn_kernel(sink_ref, q_ref, k_ref, ks_ref, v_ref, vs_ref, o_ref):
    kk = (k_ref[...], ks_ref[...])
    vv = (v_ref[...], vs_ref[...])
    for c in range(4):
        qc = q_ref[:, c * 128:(c + 1) * 128]
        halves = []
        for hf in range(2):
            h = 2 * c + hf
            swapped = 0 if (h // 4) == hf else 1
            qm = jnp.where(_half_mask(qc.shape, hf), qc, jnp.zeros_like(qc))
            s = _nt(qm, kk[swapped])
            sink = sink_ref[h]
            m = jnp.maximum(jnp.max(s, axis=1, keepdims=True), sink)
            p = jnp.exp(s - m)
            den = jnp.sum(p, axis=1, keepdims=True) + jnp.exp(sink - m)
            halves.append(jnp.dot(p.astype(BF16), vv[swapped], preferred_element_type=F32) / den)
        o_ref[:, c * 128:(c + 1) * 128] = jnp.where(_half_mask(halves[0].shape, 0),
                                                    halves[0], halves[1]).astype(BF16)


def _ctx_attn(sink, q, k, ks, v, vs):
    ctx_spec = lambda n: pl.BlockSpec((CTX, n), lambda b: (N_MAIN // CTX + b, 0))
    return pl.pallas_call(
        _ctx_attn_kernel,
        out_shape=jax.ShapeDtypeStruct((N_CTX, 512), BF16),
        grid=(BATCH,),
        in_specs=[pl.BlockSpec(memory_space=pltpu.SMEM),
                  ctx_spec(512), ctx_spec(128), ctx_spec(128), ctx_spec(128), ctx_spec(128)],
        out_specs=pl.BlockSpec((CTX, 512), lambda b: (b, 0)),
        compiler_params=_cparams(("arbitrary",)),
        name="context_attn",
    )(sink, q, k, ks, v, vs)


NA_GROUP = 8
N_GRID_ROWS = SEQ // GRID_W


def _na_kernel(q_ref, k_ref, v_ref, kc_ref, vc_ref, nb_ref, o_ref):
    g = pl.program_id(1)

    def body(i, carry):
        r = g * NA_GROUP + i
        r0 = jnp.clip(r - NA_ROWS // 2, 0, N_GRID_ROWS - NA_ROWS)
        shift = r0 - r + NA_ROWS - 1
        qrows = pl.ds(pl.multiple_of(i * GRID_W, GRID_W), GRID_W)
        krows = pl.ds(pl.multiple_of(r0 * GRID_W, GRID_W), NA_ROWS * GRID_W)
        n_loc = NA_ROWS * GRID_W
        for c in range(4):
            lanes = slice(c * 128, (c + 1) * 128)
            qc = q_ref[qrows, lanes]
            q2 = jnp.concatenate([jnp.where(_half_mask(qc.shape, hf), qc, jnp.zeros_like(qc))
                                  for hf in range(2)], axis=0)
            kcat = jnp.concatenate([k_ref[krows, lanes], kc_ref[:, lanes]], axis=0)
            vcat = jnp.concatenate([v_ref[krows, lanes], vc_ref[:, lanes]], axis=0)
            s = _nt(q2, kcat)
            bias = jnp.concatenate([nb_ref[2 * c, shift], nb_ref[2 * c + 1, shift]], axis=0)
            s_loc = s[:, :n_loc] + bias
            s_ctx = s[:, n_loc:]
            m = jnp.maximum(jnp.max(s_loc, axis=1, keepdims=True), jnp.max(s_ctx, axis=1, keepdims=True))
            p_loc = jnp.exp(s_loc - m)
            p_ctx = jnp.exp(s_ctx - m)
            den = jnp.sum(p_loc, axis=1, keepdims=True) + jnp.sum(p_ctx, axis=1, keepdims=True)
            p = jnp.concatenate([p_loc, p_ctx], axis=1).astype(BF16)
            o = jnp.dot(p, vcat, preferred_element_type=F32) / den
            o_ref[qrows, lanes] = jnp.where(_half_mask(qc.shape, 0), o[:GRID_W], o[GRID_W:]).astype(BF16)
        return carry

    lax.fori_loop(0, NA_GROUP, body, 0, unroll=4)


def _na_bias(rpb):
    col = np.arange(GRID_W)
    c0 = np.clip(col - NA_COLS // 2, 0, GRID_W - NA_COLS)
    col_ok = (col[None, :] >= c0[:, None]) & (col[None, :] < c0[:, None] + NA_COLS)
    dc = np.clip(col[None, :] - col[:, None] + NA_COLS - 1, 0, 2 * NA_COLS - 2)
    onehot = (dc[None] == np.arange(2 * NA_COLS - 1)[:, None, None]).astype(np.float32)
    e = jnp.einsum('hrd,dqk->hrqk', rpb.astype(F32), jnp.asarray(onehot),
                   precision=lax.Precision.HIGHEST)
    e = jnp.where(col_ok[None, None], e, -jnp.inf)
    b = jnp.stack([e[:, s:s + NA_ROWS] for s in range(NA_ROWS)], axis=1)
    b = jnp.transpose(b, (0, 1, 3, 2, 4))
    return b.reshape(rpb.shape[0], NA_ROWS, GRID_W, NA_ROWS * GRID_W)


def _na_attn(q, k, v, nb):
    qrows = NA_GROUP * GRID_W
    n_g = SEQ // qrows
    seq_spec = pl.BlockSpec((SEQ, 512), lambda b, g: (b, 0))
    ctx_spec = pl.BlockSpec((CTX, 512), lambda b, g: (N_MAIN // CTX + b, 0))
    return pl.pallas_call(
        _na_kernel,
        out_shape=jax.ShapeDtypeStruct((N_MAIN, 512), BF16),
        grid=(BATCH, n_g),
        in_specs=[pl.BlockSpec((qrows, 512), lambda b, g: (b * n_g + g, 0)),
                  seq_spec, seq_spec, ctx_spec, ctx_spec,
                  pl.BlockSpec(nb.shape, lambda b, g: (0, 0, 0, 0))],
        out_specs=pl.BlockSpec((qrows, 512), lambda b, g: (b * n_g + g, 0)),
        compiler_params=_cparams(("arbitrary", "arbitrary")),
        name="neighborhood_attn",
    )(q, k, v, k, v, nb)


F_N2_CHUNK = 8
F_K1_CHUNK = 8


def _four1_kernel(x_ref, w_ref, t_ref):
    w = w_ref[...]
    for j in range(F_N2_CHUNK):
        res = jnp.dot(w, x_ref[:, j * 512:(j + 1) * 512], preferred_element_type=F32)
        t_ref[0, j] = res[:64].astype(BF16)
        t_ref[1, j] = res[64:].astype(BF16)


def _four2_kernel(t_ref, m_ref, cs_ref, y_ref):
    cs = cs_ref[...]
    for j in range(F_K1_CHUNK):
        lanes = slice(j * 512, (j + 1) * 512)
        tt = jnp.concatenate([t_ref[0, :, lanes], t_ref[1, :, lanes]], axis=0)
        pp = jnp.dot(m_ref[j], tt, preferred_element_type=F32)
        pc = jnp.concatenate([pp[:64], pp[64:]], axis=1).astype(BF16)
        y_ref[:, lanes] = jnp.dot(pc, cs, preferred_element_type=F32).astype(BF16)


def _fourier_tables():
    a = np.arange(64)
    ang1 = 2.0 * np.pi * np.outer(a, a) / 64.0
    w1 = np.concatenate([np.cos(ang1), -np.sin(ang1)], axis=0)
    k1 = a[:, None, None]
    k2 = a[None, :, None]
    n2 = a[None, None, :]
    theta = 2.0 * np.pi * (n2 * k2 / 64.0 + n2 * k1 / 4096.0)
    mr = np.cos(theta) / 64.0
    mi = -np.sin(theta) / 64.0
    m = np.concatenate([np.concatenate([mr, -mi], axis=2),
                        np.concatenate([mi, mr], axis=2)], axis=1)
    c = np.arange(128)
    angc = 2.0 * np.pi * np.outer(c, c) / 128.0
    eye4 = np.eye(4)
    cc = np.kron(eye4, np.cos(angc)) / np.sqrt(128.0)
    sc = np.kron(eye4, np.sin(angc)) / np.sqrt(128.0)
    cs = np.concatenate([cc, sc], axis=0)
    return tuple(jnp.asarray(t, F32).astype(BF16) for t in (w1, m, cs))


def _fourier(f):
    w1, m, cs = _fourier_tables()
    fv = f.reshape(N_ALL // 64, 64 * 512)
    n_c = 64 // F_N2_CHUNK
    t = pl.pallas_call(
        _four1_kernel,
        out_shape=jax.ShapeDtypeStruct((BATCH, 2, 64, 64, 512), BF16),
        grid=(BATCH, n_c),
        in_specs=[pl.BlockSpec((64, F_N2_CHUNK * 512), lambda b, c: (b, c)),
                  pl.BlockSpec((128, 64), lambda b, c: (0, 0))],
        out_specs=pl.BlockSpec((None, 2, F_N2_CHUNK, 64, 512), lambda b, c: (b, 0, c, 0, 0)),
        compiler_params=_cparams(("arbitrary", "arbitrary")),
        name="fourier_rows",
    )(fv, w1)
    n_k = 64 // F_K1_CHUNK
    t2 = t.reshape(BATCH, 2, 64, 64 * 512)
    y = pl.pallas_call(
        _four2_kernel,
        out_shape=jax.ShapeDtypeStruct((BATCH * 64, 64 * 512), BF16),
        grid=(BATCH, n_k),
        in_specs=[pl.BlockSpec((None, 2, 64, F_K1_CHUNK * 512), lambda b, c: (b, 0, 0, c)),
                  pl.BlockSpec((F_K1_CHUNK, 128, 128), lambda b, c: (c, 0, 0)),
                  pl.BlockSpec((1024, 512), lambda b, c: (0, 0))],
        out_specs=pl.BlockSpec((64, F_K1_CHUNK * 512), lambda b, c: (b, c)),
        compiler_params=_cparams(("arbitrary", "arbitrary")),
        name="fourier_cols",
    )(t2, m, cs)
    return y.reshape(N_MAIN, 512)


OUT_TILES = 2
TO = OUT_TILES * TM


def _route(h2, rwt_ref, rb_ref, carry):
    logits = lax.dot_general(rwt_ref[...], h2, (((1,), (1,)), ((), ())),
                             preferred_element_type=F32,
                             precision=lax.Precision.HIGHEST) + rb_ref[...]
    eidx = lax.broadcasted_iota(I32, logits.shape, 0)
    vals = logits
    sels, tops, idxs = [], [], []
    for _ in range(TOP_K):
        m = jnp.max(vals, axis=0, keepdims=True)
        idx = jnp.min(jnp.where(vals == m, eidx, N_EXPERTS), axis=0, keepdims=True)
        sel = eidx == idx
        sels.append(sel)
        tops.append(m)
        idxs.append(idx)
        vals = jnp.where(sel, -jnp.inf, vals)
    ex = [jnp.exp(t - tops[0]) for t in tops]
    den = ex[0] + ex[1] + ex[2] + ex[3]
    onehot = jnp.zeros(logits.shape, F32)
    for sel in sels:
        onehot = onehot + sel.astype(F32)
    r_i = lax.broadcasted_iota(I32, (TM, TM), 0)
    c_i = lax.broadcasted_iota(I32, (TM, TM), 1)
    upper = (r_i < c_i).astype(BF16)
    prefix = jnp.dot(onehot.astype(BF16), upper, preferred_element_type=F32)
    base = carry[:, 0:1] + prefix
    tw = [e / den for e in ex]
    rk = [jnp.sum(jnp.where(sel, base, 0.0), axis=0, keepdims=True).astype(I32) for sel in sels]
    return idxs, tw, rk, carry + jnp.sum(onehot, axis=1, keepdims=True)


def _out_tail(i, x_ref, y, mod_ref, g_ref, rwt_ref, rb_ref, carry_ref,
              xn_ref, h2_ref, te_ref, tw_ref, rk_ref, cnt_ref):
    @pl.when(i == 0)
    def _():
        carry_ref[...] = jnp.zeros_like(carry_ref)

    carry = carry_ref[...]
    for t in range(OUT_TILES):
        rows = slice(t * TM, (t + 1) * TM)
        xn = x_ref[rows, :] + mod_ref[2:3, :] * y[rows]
        xn_ref[rows, :] = xn
        h2 = _rms_mod(xn, g_ref[...], mod_ref[4:5, :], mod_ref[3:4, :])
        h2_ref[rows, :] = h2
        te, tw, rk, carry = _route(h2, rwt_ref, rb_ref, carry)
        for k in range(TOP_K):
            te_ref[k:k + 1, rows] = te[k]
            tw_ref[k:k + 1, rows] = tw[k]
            rk_ref[k:k + 1, rows] = rk[k]
    carry_ref[...] = carry
    cnt_ref[...] = carry


def _out_even_kernel(grp_ref, first_ref, last_ref,
                     x_ref, a_ref, gb_ref, z_ref, zp_ref, zn_ref, cw_ref, w_ref, mod_ref, g_ref,
                     rwt_ref, rb_ref,
                     xn_ref, h2_ref, te_ref, tw_ref, rk_ref, cnt_ref, carry_ref):
    i = pl.program_id(0)
    z = z_ref[...]
    rid = lax.broadcasted_iota(I32, z.shape, 0)
    zm1 = jnp.where(rid == 0, zp_ref[7:8, :], pltpu.roll(z, 1, 0))
    zp1 = jnp.where(rid == TO - 1, zn_ref[0:1, :], pltpu.roll(z, TO - 1, 0))
    for t in range(OUT_TILES):
        zm1 = jnp.where(jnp.logical_and(rid == t * TM, first_ref[OUT_TILES * i + t] == 1), 0.0, zm1)
        zp1 = jnp.where(jnp.logical_and(rid == (t + 1) * TM - 1, last_ref[OUT_TILES * i + t] == 1), 0.0, zp1)
    conv = gb_ref[...] * (zm1 * cw_ref[0:1, :] + z * cw_ref[1:2, :] + zp1 * cw_ref[2:3, :])
    y = (jnp.dot(a_ref[...], w_ref[0:512, :], preferred_element_type=F32)
         + jnp.dot(conv.astype(BF16), w_ref[512:1024, :], preferred_element_type=F32))
    _out_tail(i, x_ref, y, mod_ref, g_ref, rwt_ref, rb_ref, carry_ref,
              xn_ref, h2_ref, te_ref, tw_ref, rk_ref, cnt_ref)


def _out_odd_kernel(grp_ref, x_ref, a_ref, f_ref, w_ref, mod_ref, g_ref, rwt_ref, rb_ref,
                    xn_ref, h2_ref, te_ref, tw_ref, rk_ref, cnt_ref, carry_ref):
    i = pl.program_id(0)
    y = (jnp.dot(a_ref[...], w_ref[0:512, :], preferred_element_type=F32)
         + jnp.dot(f_ref[...], w_ref[512:1024, :], preferred_element_type=F32))
    _out_tail(i, x_ref, y, mod_ref, g_ref, rwt_ref, rb_ref, carry_ref,
              xn_ref, h2_ref, te_ref, tw_ref, rk_ref, cnt_ref)


def _out_shapes(n_rows):
    return (jax.ShapeDtypeStruct((n_rows, D), F32), jax.ShapeDtypeStruct((n_rows, D), F32),
            jax.ShapeDtypeStruct((TOP_K, n_rows), I32), jax.ShapeDtypeStruct((TOP_K, n_rows), F32),
            jax.ShapeDtypeStruct((TOP_K, n_rows), I32), jax.ShapeDtypeStruct((N_EXPERTS, 128), F32))


def _out_even(xall, attn, gb, z, conv_w, w_bf, mod, g, rwt, rb):
    grp, _, first, last = _tile_tables()
    n_rows = N_ALL
    zblocks = n_rows // 8
    im = lambda f: (lambda i, grp, fi, la: f(i))
    tile = lambda n: pl.BlockSpec((TO, n), im(lambda i: (i, 0)))
    const = lambda shape: pl.BlockSpec(shape, im(lambda i: (0,) * len(shape)))
    tk = pl.BlockSpec((TOP_K, TO), im(lambda i: (0, i)))
    return pl.pallas_call(
        _out_even_kernel,
        out_shape=_out_shapes(n_rows),
        grid_spec=pltpu.PrefetchScalarGridSpec(
            num_scalar_prefetch=3,
            grid=(NT_ALL // OUT_TILES,),
            in_specs=[tile(D), tile(512), tile(512), tile(512),
                      pl.BlockSpec((8, 512), im(lambda i: (jnp.maximum(i * (TO // 8) - 1, 0), 0))),
                      pl.BlockSpec((8, 512), im(lambda i: (jnp.minimum((i + 1) * (TO // 8), zblocks - 1), 0))),
                      const((3, 512)), const((D, D)),
                      pl.BlockSpec((None, 6, D), lambda i, grp, fi, la: (grp[OUT_TILES * i], 0, 0)),
                      const((1, D)), const((N_EXPERTS, D)), const((N_EXPERTS, 1))],
            out_specs=(tile(D), tile(D), tk, tk, tk, const((N_EXPERTS, 128))),
            scratch_shapes=[pltpu.VMEM((N_EXPERTS, 128), F32)],
        ),
        compiler_params=_cparams(("arbitrary",)),
        name="out_proj_even",
    )(jnp.asarray(grp), jnp.asarray(first), jnp.asarray(last),
      xall, attn, gb, z, z, z, conv_w, w_bf, mod, g, rwt, rb)


def _out_odd(xall, attn, four, w_bf, mod, g, rwt, rb):
    grp, _, _, _ = _tile_tables()
    n_rows = N_MAIN
    im = lambda f: (lambda i, grp: f(i))
    tile = lambda n: pl.BlockSpec((TO, n), im(lambda i: (i, 0)))
    const = lambda shape: pl.BlockSpec(shape, im(lambda i: (0,) * len(shape)))
    tk = pl.BlockSpec((TOP_K, TO), im(lambda i: (0, i)))
    return pl.pallas_call(
        _out_odd_kernel,
        out_shape=_out_shapes(n_rows),
        grid_spec=pltpu.PrefetchScalarGridSpec(
            num_scalar_prefetch=1,
            grid=(NT_MAIN // OUT_TILES,),
            in_specs=[tile(D), tile(512), tile(512), const((D, D)),
                      pl.BlockSpec((None, 6, D), lambda i, grp: (grp[OUT_TILES * i], 0, 0)),
                      const((1, D)), const((N_EXPERTS, D)), const((N_EXPERTS, 1))],
            out_specs=(tile(D), tile(D), tk, tk, tk, const((N_EXPERTS, 128))),
            scratch_shapes=[pltpu.VMEM((N_EXPERTS, 128), F32)],
        ),
        compiler_params=_cparams(("arbitrary",)),
        name="out_proj_odd",
    )(jnp.asarray(grp), xall, attn, four, w_bf, mod, g, rwt, rb)


def _moe_plan(counts_f, top_e_t, rank_t, n_tok):
    counts = counts_f[:, 0].astype(I32)
    padded = (counts + TMM - 1) // TMM * TMM
    e_i = jnp.arange(N_EXPERTS, dtype=I32)
    incl = e_i[None, :] <= e_i[:, None]
    pad_end = jnp.sum(jnp.where(incl, padded[None, :], 0), axis=1)
    pad_start = pad_end - padded
    sel = top_e_t[None] == e_i[:, None, None]
    dest = jnp.sum(jnp.where(sel, pad_start[:, None, None], 0), axis=0) + rank_t
    n_blocks = n_tok * TOP_K // TMM + N_EXPERTS
    blk_start = jnp.arange(n_blocks, dtype=I32) * TMM
    block_e = jnp.minimum(jnp.sum((blk_start[:, None] >= pad_end[None, :]).astype(I32), axis=1),
                          N_EXPERTS - 1)
    n_used = (pad_end[-1] // TMM).reshape(1)
    n_tiles = n_tok // TT
    dest_tiles = dest.reshape(TOP_K, n_tiles, TT).transpose(1, 0, 2).reshape(n_tiles, TOP_K * TT)
    return dest_tiles, block_e, n_used, pad_end, padded


def _dispatch_kernel(pe_ref, pd_ref, nu_ref, dst_hbm, h2_ref, xs_hbm, dst_s, zbuf, sem_i, sem, sem_z):
    i = pl.program_id(0)
    n_blocks = xs_hbm.shape[0] // TMM

    @pl.when(i == 0)
    def _():
        zbuf[...] = jnp.zeros_like(zbuf)

        def fill(b):
            return pltpu.make_async_copy(zbuf, xs_hbm.at[pl.ds(pl.multiple_of(b * TMM, TMM), TMM)], sem_z)

        def fill_expert(e, n):
            has_rows = pd_ref[e] > 0

            @pl.when(has_rows)
            def _():
                fill(pe_ref[e] // TMM - 1).start()

            return n + has_rows.astype(I32)

        def fill_tail(b, c):
            fill(b).start()
            return c

        def fill_wait(j, c):
            fill(0).wait()
            return c

        n_fill = lax.fori_loop(0, N_EXPERTS, fill_expert, 0) + n_blocks - nu_ref[0]
        lax.fori_loop(nu_ref[0], n_blocks, fill_tail, 0)
        lax.fori_loop(0, n_fill, fill_wait, 0)

    ci = pltpu.make_async_copy(dst_hbm.at[i], dst_s, sem_i)
    ci.start()
    ci.wait()

    def issue(j, c):
        for k in range(TOP_K):
            pltpu.make_async_copy(h2_ref.at[pl.ds(j, 1)], xs_hbm.at[pl.ds(dst_s[k * TT + j], 1)],
                                  sem).start(priority=k % 2)
        return c

    lax.fori_loop(0, TT, issue, 0, unroll=4)
    for _ in range(TOP_K):
        pltpu.make_async_copy(h2_ref, xs_hbm.at[pl.ds(0, TT)], sem).wait()


def _dispatch(h2, dest_tiles, pad_end, padded, n_used, cap):
    n_tiles = dest_tiles.shape[0]
    return pl.pallas_call(
        _dispatch_kernel,
        out_shape=jax.ShapeDtypeStruct((cap, D), F32),
        grid_spec=pltpu.PrefetchScalarGridSpec(
            num_scalar_prefetch=3,
            grid=(n_tiles,),
            in_specs=[pl.BlockSpec(memory_space=pl.ANY),
                      pl.BlockSpec((TT, D), lambda i, pe, pd, nu: (i, 0))],
            out_specs=pl.BlockSpec(memory_space=pl.ANY),
            scratch_shapes=[pltpu.SMEM((TOP_K * TT,), I32), pltpu.VMEM((TMM, D), F32),
                            pltpu.SemaphoreType.DMA, pltpu.SemaphoreType.DMA, pltpu.SemaphoreType.DMA],
        ),
        compiler_params=_cparams(("arbitrary",)),
        name="moe_dispatch",
    )(pad_end, padded, n_used, dest_tiles, h2)


def _moe_kernel(be_ref, nu_ref, x_ref, wgu_ref, bgu_ref, wdn_ref, bdn_ref, y_ref, wgu_bf, wdn_bf):
    i = pl.program_id(0)
    prev = be_ref[jnp.maximum(i - 1, 0)]

    @pl.when((i < nu_ref[0]) & ((i == 0) | (be_ref[i] != prev)))
    def _():
        wgu_bf[...] = wgu_ref[...].astype(BF16)
        wdn_bf[...] = wdn_ref[...].astype(BF16)

    @pl.when(i < nu_ref[0])
    def _():
        xb = x_ref[...].astype(BF16)
        gu = jnp.dot(xb, wgu_bf[...], preferred_element_type=F32) + bgu_ref[...]
        gate = jnp.minimum(gu[:, :D], SWIGLU_LIMIT)
        up = jnp.clip(gu[:, D:], -SWIGLU_LIMIT, SWIGLU_LIMIT)
        act = (up + 1.0) * (gate * (1.0 / (1.0 + jnp.exp(-SWIGLU_ALPHA * gate))))
        y_ref[...] = jnp.dot(act.astype(BF16), wdn_bf[...], preferred_element_type=F32) + bdn_ref[...]

    @pl.when(i >= nu_ref[0])
    def _():
        y_ref[...] = jnp.zeros_like(y_ref)


def _moe(layer, xs, block_e, n_used, w_gu, b_gu, w_dn, b_dn):
    n_blocks = block_e.shape[0]
    n_l = w_gu.shape[0]
    blk = lambda i, be, nu: (jnp.minimum(i, nu[0] - 1), 0)
    out_blk = lambda i, be, nu: (i, 0)
    exp4 = lambda i, be, nu: (layer, be[jnp.minimum(i, nu[0] - 1)], 0, 0)
    return pl.pallas_call(
        _moe_kernel,
        out_shape=jax.ShapeDtypeStruct((n_blocks * TMM, D), F32),
        grid_spec=pltpu.PrefetchScalarGridSpec(
            num_scalar_prefetch=2,
            grid=(n_blocks,),
            in_specs=[pl.BlockSpec((TMM, D), blk),
                      pl.BlockSpec((None, None, D, 2 * D), exp4),
                      pl.BlockSpec((None, None, 1, 2 * D), exp4),
                      pl.BlockSpec((None, None, D, D), exp4),
                      pl.BlockSpec((None, None, 1, D), exp4)],
            out_specs=pl.BlockSpec((TMM, D), out_blk),
            scratch_shapes=[pltpu.VMEM((D, 2 * D), BF16), pltpu.VMEM((D, D), BF16)],
        ),
        compiler_params=_cparams(("arbitrary",)),
        name="moe_experts",
    )(block_e, n_used, xs, w_gu, b_gu.reshape(n_l, N_EXPERTS, 1, 2 * D), w_dn,
      b_dn.reshape(n_l, N_EXPERTS, 1, D))


def _combine_kernel(final, grp_ref, dst_hbm, ys_hbm, x_ref, tw_ref, mod_ref, fn_ref, o_ref,
                    dst_s, buf, sem_i, sem):
    i = pl.program_id(0)
    ci = pltpu.make_async_copy(dst_hbm.at[i], dst_s, sem_i)
    ci.start()
    ci.wait()

    def issue(j, c):
        for k in range(TOP_K):
            pltpu.make_async_copy(ys_hbm.at[pl.ds(dst_s[k * TT + j], 1)], buf.at[k, pl.ds(j, 1)],
                                  sem).start(priority=k % 2)
        return c

    lax.fori_loop(0, TT, issue, 0, unroll=4)
    for k in range(TOP_K):
        pltpu.make_async_copy(ys_hbm.at[pl.ds(0, TT)], buf.at[k], sem).wait()
    tw = tw_ref[...]
    acc = tw[:, 0:1] * buf[0]
    for k in range(1, TOP_K):
        acc = acc + tw[:, k:k + 1] * buf[k]
    out = x_ref[...] + mod_ref[5:6, :] * acc
    if final:
        ms = jnp.mean(out * out, axis=-1, keepdims=True)
        out = out * lax.rsqrt(ms + EPS) * fn_ref[...]
    o_ref[...] = out


def _combine(xn, ys, dest_tiles, top_w_t, mod, final_norm, n_tok, final):
    grp, _, _, _ = _tile_tables()
    n_tiles = n_tok // TT
    tw = top_w_t.T
    return pl.pallas_call(
        functools.partial(_combine_kernel, final),
        out_shape=jax.ShapeDtypeStruct((n_tok, D), F32),
        grid_spec=pltpu.PrefetchScalarGridSpec(
            num_scalar_prefetch=1,
            grid=(n_tiles,),
            in_specs=[pl.BlockSpec(memory_space=pl.ANY), pl.BlockSpec(memory_space=pl.ANY),
                      pl.BlockSpec((TT, D), lambda i, grp: (i, 0)),
                      pl.BlockSpec((TT, TOP_K), lambda i, grp: (i, 0)),
                      pl.BlockSpec((None, 6, D), lambda i, grp: (grp[i], 0, 0)),
                      pl.BlockSpec((1, D), lambda i, grp: (0, 0))],
            out_specs=pl.BlockSpec((TT, D), lambda i, grp: (i, 0)),
            scratch_shapes=[pltpu.SMEM((TOP_K * TT,), I32), pltpu.VMEM((TOP_K, TT, D), F32),
                            pltpu.SemaphoreType.DMA, pltpu.SemaphoreType.DMA],
        ),
        compiler_params=_cparams(("arbitrary",)),
        name="moe_combine",
    )(jnp.asarray(grp), dest_tiles, ys, xn, tw, mod, final_norm.reshape(1, D))


def _moe_layer(layer, xn, h2, top_e_t, top_w_t, rank_t, counts, mod, w_gu, b_gu, w_dn, b_dn, final_norm, final):
    n_tok = xn.shape[0]
    dest_tiles, block_e, n_used, pad_end, padded = _moe_plan(counts, top_e_t, rank_t, n_tok)
    xs = _dispatch(h2, dest_tiles, pad_end, padded, n_used, block_e.shape[0] * TMM)
    ys = _moe(layer, xs, block_e, n_used, w_gu, b_gu, w_dn, b_dn)
    return _combine(xn, ys, dest_tiles, top_w_t, mod, final_norm, n_tok, final)


def kernel(x, c, ctx, c_ctx, ada_w, ada_b, norm_mix, norm_ffn, even_w_in, even_w_out, even_conv_w, even_sink, odd_w_in, odd_w_out, odd_rpb, router_w, router_b, moe_w_gu, moe_b_gu, moe_w_dn, moe_b_dn, final_norm):
    xall = jnp.concatenate([x.reshape(N_MAIN, D), ctx.reshape(N_CTX, D)], axis=0)
    cc = jnp.concatenate([c, c_ctx[None, :], jnp.zeros((3, D), F32)], axis=0)
    mod = _ada(cc, ada_w, ada_b).reshape(2, 8, 6, D)
    cos_f, sin_f = _rope_tables()

    q, k, ks, v, vs, gb, z = _in_even(xall, mod[0], norm_mix[0:1], even_w_in[0].astype(BF16), cos_f, sin_f)
    attn = jnp.concatenate([_win_attn(even_sink[0], q, k, ks, v, vs),
                            _ctx_attn(even_sink[0], q, k, ks, v, vs)], axis=0)
    xn, h2, te, tw, rk, cnt = _out_even(xall, attn, gb, z, even_conv_w[0], even_w_out[0].astype(BF16),
                                        mod[0], norm_ffn[0:1], router_w[0].T, router_b[0][:, None])
    xall = _moe_layer(0, xn, h2, te, tw, rk, cnt, mod[0], moe_w_gu, moe_b_gu, moe_w_dn, moe_b_dn,
                      final_norm, False)

    q, k, v, f = _in_odd(xall, mod[1], norm_mix[1:2], odd_w_in[0].astype(BF16))
    attn = _na_attn(q, k, v, _na_bias(odd_rpb[0]))
    four = _fourier(f)
    xn, h2, te, tw, rk, cnt = _out_odd(xall, attn, four, odd_w_out[0].astype(BF16),
                                       mod[1], norm_ffn[1:2], router_w[1].T, router_b[1][:, None])
    out = _moe_layer(1, xn, h2, te, tw, rk, cnt, mod[1], moe_w_gu, moe_b_gu, moe_w_dn, moe_b_dn,
                     final_norm, True)
    return out.reshape(BATCH, SEQ, D)
```

```python
import functools

import numpy as np
import jax
import jax.numpy as jnp
from jax import lax
from jax.experimental import pallas as pl
from jax.experimental.pallas import tpu as pltpu

F32 = jnp.float32
BF16 = jnp.bfloat16
I32 = jnp.int32

D = 1024
BATCH = 4
SEQ = 4096
CTX = 256
GRID_W = 64
HEAD_DIM = 64
EPS = 1e-6
ROPE_THETA = 10000.0
N_EXPERTS = 32
TOP_K = 4
SWIGLU_LIMIT = 7.0
SWIGLU_ALPHA = 1.702
NA_ROWS = 8
NA_COLS = 16

N_MAIN = BATCH * SEQ
N_CTX = BATCH * CTX
N_ALL = N_MAIN + N_CTX
TM = 256
NT_MAIN = N_MAIN // TM
NT_ALL = N_ALL // TM
TILES_PER_SEQ = SEQ // TM
TMM = 512
TT = 256
VMEM_LIMIT = 56 * 1024 * 1024


def _cparams(sem, vmem=VMEM_LIMIT):
    return pltpu.CompilerParams(dimension_semantics=sem, vmem_limit_bytes=vmem)


def _rms_mod(x, g, sc, sh):
    ms = jnp.mean(x * x, axis=-1, keepdims=True)
    return (x * lax.rsqrt(ms + EPS) * g) * (1.0 + sc) + sh


def _ada_kernel(c_ref, w_ref, b_ref, o_ref):
    c = c_ref[...]
    s = c * (1.0 / (1.0 + jnp.exp(-c)))
    o_ref[...] = jnp.dot(s, w_ref[...], preferred_element_type=F32,
                         precision=lax.Precision.HIGHEST) + b_ref[...]


def _ada(cc, ada_w, ada_b):
    n_l = ada_w.shape[0]
    tn = 1024
    return pl.pallas_call(
        _ada_kernel,
        out_shape=jax.ShapeDtypeStruct((n_l, 8, 6 * D), F32),
        grid=(n_l, 6 * D // tn),
        in_specs=[pl.BlockSpec((8, D), lambda l, j: (0, 0)),
                  pl.BlockSpec((None, D, tn), lambda l, j: (l, 0, j)),
                  pl.BlockSpec((None, 1, tn), lambda l, j: (l, 0, j))],
        out_specs=pl.BlockSpec((None, 8, tn), lambda l, j: (l, 0, j)),
        compiler_params=_cparams(("arbitrary", "arbitrary")),
        name="ada_mod",
    )(cc, ada_w, ada_b.reshape(n_l, 1, 6 * D))


def _rope_apply(t, cos, sin):
    n = t.shape[1]
    lane = lax.broadcasted_iota(I32, t.shape, 1)
    fwd = pltpu.roll(t, n - 32, 1)
    bwd = pltpu.roll(t, 32, 1)
    rot = jnp.where((lane % 64) < 32, fwd, bwd)
    reps = n // 128
    cosf = jnp.concatenate([cos] * reps, axis=1) if reps > 1 else cos
    sinf = jnp.concatenate([sin] * reps, axis=1) if reps > 1 else sin
    return t * cosf + rot * sinf


def _in_even_kernel(grp_ref, rblk_ref, x_ref, mod_ref, g_ref, w_ref, cos_ref, sin_ref,
                    q_ref, k_ref, ks_ref, v_ref, vs_ref, gb_ref, z_ref):
    h = _rms_mod(x_ref[...], g_ref[...], mod_ref[1:2, :], mod_ref[0:1, :])
    p = jnp.dot(h.astype(BF16), w_ref[...], preferred_element_type=F32)
    cos = cos_ref[...]
    sin = sin_ref[...]
    q = _rope_apply(p[:, 0:512], cos, sin) * (HEAD_DIM ** -0.5)
    k = _rope_apply(p[:, 512:640], cos, sin)
    v = p[:, 640:768]
    q_ref[...] = q.astype(BF16)
    k_ref[...] = k.astype(BF16)
    ks_ref[...] = pltpu.roll(k, 64, 1).astype(BF16)
    v_ref[...] = v.astype(BF16)
    vs_ref[...] = pltpu.roll(v, 64, 1).astype(BF16)
    gb_ref[...] = p[:, 768:1280]
    z_ref[...] = p[:, 1280:1792] * p[:, 1792:2304]


def _in_odd_kernel(grp_ref, x_ref, mod_ref, g_ref, w_ref, q_ref, k_ref, v_ref, f_ref):
    h = _rms_mod(x_ref[...], g_ref[...], mod_ref[1:2, :], mod_ref[0:1, :])
    p = jnp.dot(h.astype(BF16), w_ref[...], preferred_element_type=F32)
    q_ref[...] = (p[:, 0:512] * (HEAD_DIM ** -0.5)).astype(BF16)
    k_ref[...] = p[:, 512:1024].astype(BF16)
    v_ref[...] = p[:, 1024:1536].astype(BF16)
    f_ref[...] = p[:, 1536:2048].astype(BF16)


def _tile_tables():
    t = np.arange(NT_ALL)
    main = t < NT_MAIN
    grp = np.where(main, t // TILES_PER_SEQ, BATCH).astype(np.int32)
    rblk = np.where(main, t % TILES_PER_SEQ, TILES_PER_SEQ).astype(np.int32)
    first = np.where(main, (t % TILES_PER_SEQ) == 0, True).astype(np.int32)
    last = np.where(main, (t % TILES_PER_SEQ) == TILES_PER_SEQ - 1, True).astype(np.int32)
    return grp, rblk, first, last


def _rope_tables():
    t = jnp.arange(SEQ, dtype=I32)
    row = (t // GRID_W).astype(F32)
    col = (t % GRID_W).astype(F32)
    n_freq = HEAD_DIM // 4
    inv_freq = jnp.power(ROPE_THETA, -jnp.arange(n_freq, dtype=F32) / n_freq)
    ang = jnp.concatenate([row[:, None] * inv_freq, col[:, None] * inv_freq], axis=-1)
    cos = jnp.cos(ang)
    sin = jnp.sin(ang)
    cos_f = jnp.concatenate([cos, cos, cos, cos], axis=1)
    sin_f = jnp.concatenate([-sin, sin, -sin, sin], axis=1)
    cos_f = jnp.concatenate([cos_f, jnp.ones((TM, 128), F32)], axis=0)
    sin_f = jnp.concatenate([sin_f, jnp.zeros((TM, 128), F32)], axis=0)
    return cos_f, sin_f


def _in_even(xall, mod, g, w_bf, cos_f, sin_f):
    grp, rblk, _, _ = _tile_tables()
    row = lambda n, dt: jax.ShapeDtypeStruct((N_ALL, n), dt)
    tile = lambda n: pl.BlockSpec((TM, n), lambda i, grp, rb: (i, 0))
    return pl.pallas_call(
        _in_even_kernel,
        out_shape=(row(512, BF16), row(128, BF16), row(128, BF16), row(128, BF16), row(128, BF16),
                   row(512, F32), row(512, F32)),
        grid_spec=pltpu.PrefetchScalarGridSpec(
            num_scalar_prefetch=2,
            grid=(NT_ALL,),
            in_specs=[tile(D),
                      pl.BlockSpec((None, 6, D), lambda i, grp, rb: (grp[i], 0, 0)),
                      pl.BlockSpec((1, D), lambda i, grp, rb: (0, 0)),
                      pl.BlockSpec((D, 2304), lambda i, grp, rb: (0, 0)),
                      pl.BlockSpec((TM, 128), lambda i, grp, rb: (rb[i], 0)),
                      pl.BlockSpec((TM, 128), lambda i, grp, rb: (rb[i], 0))],
            out_specs=(tile(512), tile(128), tile(128), tile(128), tile(128), tile(512), tile(512)),
        ),
        compiler_params=_cparams(("arbitrary",)),
        name="in_proj_even",
    )(jnp.asarray(grp), jnp.asarray(rblk), xall, mod, g, w_bf, cos_f, sin_f)


def _in_odd(xall, mod, g, w_bf):
    grp, _, _, _ = _tile_tables()
    row = lambda n, dt: jax.ShapeDtypeStruct((N_ALL, n), dt)
    tile = lambda n: pl.BlockSpec((TM, n), lambda i, grp: (i, 0))
    return pl.pallas_call(
        _in_odd_kernel,
        out_shape=(row(512, BF16), row(512, BF16), row(512, BF16), row(512, BF16)),
        grid_spec=pltpu.PrefetchScalarGridSpec(
            num_scalar_prefetch=1,
            grid=(NT_ALL,),
            in_specs=[tile(D),
                      pl.BlockSpec((None, 6, D), lambda i, grp: (grp[i], 0, 0)),
                      pl.BlockSpec((1, D), lambda i, grp: (0, 0)),
                      pl.BlockSpec((D, 2048), lambda i, grp: (0, 0))],
            out_specs=(tile(512), tile(512), tile(512), tile(512)),
        ),
        compiler_params=_cparams(("arbitrary",)),
        name="in_proj_odd",
    )(jnp.asarray(grp), xall, mod, g, w_bf)


def _nt(a, b):
    return lax.dot_general(a, b, (((1,), (1,)), ((), ())), preferred_element_type=F32)


def _half_mask(shape, half):
    lane = lax.broadcasted_iota(I32, shape, 1)
    return (lane < 64) if half == 0 else (lane >= 64)


def _win_kernel(sink_ref, q_ref, k_ref, ks_ref, v_ref, vs_ref, kc_ref, ksc_ref, vc_ref, vsc_ref, o_ref):
    n = pl.program_id(1)
    start = pl.multiple_of(jnp.clip((n - 1) * 128, 0, SEQ - 384), 128)
    win = pl.ds(start, 384)
    row = lax.broadcasted_iota(I32, (256, 384), 0)
    col = lax.broadcasted_iota(I32, (256, 384), 1)
    valid = jnp.abs((n * 128 + row % 128) - (start + col)) <= 128
    first = lax.broadcasted_iota(I32, (256, 1), 0) < 128
    kk = (jnp.concatenate([k_ref[win, :], kc_ref[...]], axis=0),
          jnp.concatenate([ks_ref[win, :], ksc_ref[...]], axis=0))
    vv = (jnp.concatenate([v_ref[win, :], vc_ref[...]], axis=0),
          jnp.concatenate([vs_ref[win, :], vsc_ref[...]], axis=0))
    outs = {}
    for hk in range(2):
        for hf in range(2):
            swapped = 0 if hk == hf else 1
            chunks = (2 * hk, 2 * hk + 1)
            qs = [q_ref[:, c * 128:(c + 1) * 128] for c in chunks]
            q2 = jnp.concatenate([jnp.where(_half_mask(t.shape, hf), t, jnp.zeros_like(t)) for t in qs],
                                 axis=0)
            s = _nt(q2, kk[swapped])
            s_loc = jnp.where(valid, s[:, :384], -jnp.inf)
            s_ctx = s[:, 384:]
            sink = jnp.where(first, sink_ref[2 * chunks[0] + hf], sink_ref[2 * chunks[1] + hf])
            m = jnp.maximum(jnp.maximum(jnp.max(s_loc, axis=1, keepdims=True),
                                        jnp.max(s_ctx, axis=1, keepdims=True)), sink)
            p_loc = jnp.exp(s_loc - m)
            p_ctx = jnp.exp(s_ctx - m)
            den = (jnp.sum(p_loc, axis=1, keepdims=True) + jnp.sum(p_ctx, axis=1, keepdims=True)
                   + jnp.exp(sink - m))
            p = jnp.concatenate([p_loc, p_ctx], axis=1).astype(BF16)
            o = jnp.dot(p, vv[swapped], preferred_element_type=F32) / den
            outs[(chunks[0], hf)] = o[:128]
            outs[(chunks[1], hf)] = o[128:]
    for c in range(4):
        o_ref[:, c * 128:(c + 1) * 128] = jnp.where(_half_mask((128, 128), 0),
                                                    outs[(c, 0)], outs[(c, 1)]).astype(BF16)


def _win_attn(sink, q, k, ks, v, vs):
    nb = SEQ // 128
    seq_spec = pl.BlockSpec((SEQ, 128), lambda b, n: (b, 0))
    ctx_spec = pl.BlockSpec((CTX, 128), lambda b, n: (N_MAIN // CTX + b, 0))
    return pl.pallas_call(
        _win_kernel,
        out_shape=jax.ShapeDtypeStruct((N_MAIN, 512), BF16),
        grid=(BATCH, nb),
        in_specs=[pl.BlockSpec(memory_space=pltpu.SMEM),
                  pl.BlockSpec((128, 512), lambda b, n: (b * (SEQ // 128) + n, 0)),
                  seq_spec, seq_spec, seq_spec, seq_spec,
                  ctx_spec, ctx_spec, ctx_spec, ctx_spec],
        out_specs=pl.BlockSpec((128, 512), lambda b, n: (b * (SEQ // 128) + n, 0)),
        compiler_params=_cparams(("arbitrary", "arbitrary")),
        name="window_attn",
    )(sink, q, k, ks, v, vs, k, ks, v, vs)


def _ctx_attn_kernel(sink_ref, q_ref, k_ref, ks_ref, v_ref, vs_ref, o_ref):
    kk = (k_ref[...], ks_ref[...])
    vv = (v_ref[...], vs_ref[...])
    for c in range(4):
        qc = q_ref[:, c * 128:(c + 1) * 128]
        halves = []
        for hf in range(2):
            h = 2 * c + hf
            swapped = 0 if (h // 4) == hf else 1
            qm = jnp.where(_half_mask(qc.shape, hf), qc, jnp.zeros_like(qc))
            s = _nt(qm, kk[swapped])
            sink = sink_ref[h]
            m = jnp.maximum(jnp.max(s, axis=1, keepdims=True), sink)
            p = jnp.exp(s - m)
            den = jnp.sum(p, axis=1, keepdims=True) + jnp.exp(sink - m)
            halves.append(jnp.dot(p.astype(BF16), vv[swapped], preferred_element_type=F32) / den)
        o_ref[:, c * 128:(c + 1) * 128] = jnp.where(_half_mask(halves[0].shape, 0),
                                                    halves[0], halves[1]).astype(BF16)


def _ctx_attn(sink, q, k, ks, v, vs):
    ctx_spec = lambda n: pl.BlockSpec((CTX, n), lambda b: (N_MAIN // CTX + b, 0))
    return pl.pallas_call(
        _ctx_attn_kernel,
        out_shape=jax.ShapeDtypeStruct((N_CTX, 512), BF16),
        grid=(BATCH,),
        in_specs=[pl.BlockSpec(memory_space=pltpu.SMEM),
                  ctx_spec(512), ctx_spec(128), ctx_spec(128), ctx_spec(128), ctx_spec(128)],
        out_specs=pl.BlockSpec((CTX, 512), lambda b: (b, 0)),
        compiler_params=_cparams(("arbitrary",)),
        name="context_attn",
    )(sink, q, k, ks, v, vs)


NA_GROUP = 8
N_GRID_ROWS = SEQ // GRID_W


def _na_kernel(q_ref, k_ref, v_ref, kc_ref, vc_ref, nb_ref, o_ref):
    g = pl.program_id(1)

    def body(i, carry):
        r = g * NA_GROUP + i
        r0 = jnp.clip(r - NA_ROWS // 2, 0, N_GRID_ROWS - NA_ROWS)
        shift = r0 - r + NA_ROWS - 1
        qrows = pl.ds(pl.multiple_of(i * GRID_W, GRID_W), GRID_W)
        krows = pl.ds(pl.multiple_of(r0 * GRID_W, GRID_W), NA_ROWS * GRID_W)
        n_loc = NA_ROWS * GRID_W
        for c in range(4):
            lanes = slice(c * 128, (c + 1) * 128)
            qc = q_ref[qrows, lanes]
            q2 = jnp.concatenate([jnp.where(_half_mask(qc.shape, hf), qc, jnp.zeros_like(qc))
                                  for hf in range(2)], axis=0)
            kcat = jnp.concatenate([k_ref[krows, lanes], kc_ref[:, lanes]], axis=0)
            vcat = jnp.concatenate([v_ref[krows, lanes], vc_ref[:, lanes]], axis=0)
            s = _nt(q2, kcat)
            bias = jnp.concatenate([nb_ref[2 * c, shift], nb_ref[2 * c + 1, shift]], axis=0)
            s_loc = s[:, :n_loc] + bias
            s_ctx = s[:, n_loc:]
            m = jnp.maximum(jnp.max(s_loc, axis=1, keepdims=True), jnp.max(s_ctx, axis=1, keepdims=True))
            p_loc = jnp.exp(s_loc - m)
            p_ctx = jnp.exp(s_ctx - m)
            den = jnp.sum(p_loc, axis=1, keepdims=True) + jnp.sum(p_ctx, axis=1, keepdims=True)
            p = jnp.concatenate([p_loc, p_ctx], axis=1).astype(BF16)
            o = jnp.dot(p, vcat, preferred_element_type=F32) / den
            o_ref[qrows, lanes] = jnp.where(_half_mask(qc.shape, 0), o[:GRID_W], o[GRID_W:]).astype(BF16)
        return carry

    lax.fori_loop(0, NA_GROUP, body, 0, unroll=4)


def _na_bias(rpb):
    col = np.arange(GRID_W)
    c0 = np.clip(col - NA_COLS // 2, 0, GRID_W - NA_COLS)
    col_ok = (col[None, :] >= c0[:, None]) & (col[None, :] < c0[:, None] + NA_COLS)
    dc = np.clip(col[None, :] - col[:, None] + NA_COLS - 1, 0, 2 * NA_COLS - 2)
    onehot = (dc[None] == np.arange(2 * NA_COLS - 1)[:, None, None]).astype(np.float32)
    e = jnp.einsum('hrd,dqk->hrqk', rpb.astype(F32), jnp.asarray(onehot),
                   precision=lax.Precision.HIGHEST)
    e = jnp.where(col_ok[None, None], e, -jnp.inf)
    b = jnp.stack([e[:, s:s + NA_ROWS] for s in range(NA_ROWS)], axis=1)
    b = jnp.transpose(b, (0, 1, 3, 2, 4))
    return b.reshape(rpb.shape[0], NA_ROWS, GRID_W, NA_ROWS * GRID_W)


def _na_attn(q, k, v, nb):
    qrows = NA_GROUP * GRID_W
    n_g = SEQ // qrows
    seq_spec = pl.BlockSpec((SEQ, 512), lambda b, g: (b, 0))
    ctx_spec = pl.BlockSpec((CTX, 512), lambda b, g: (N_MAIN // CTX + b, 0))
    return pl.pallas_call(
        _na_kernel,
        out_shape=jax.ShapeDtypeStruct((N_MAIN, 512), BF16),
        grid=(BATCH, n_g),
        in_specs=[pl.BlockSpec((qrows, 512), lambda b, g: (b * n_g + g, 0)),
                  seq_spec, seq_spec, ctx_spec, ctx_spec,
                  pl.BlockSpec(nb.shape, lambda b, g: (0, 0, 0, 0))],
        out_specs=pl.BlockSpec((qrows, 512), lambda b, g: (b * n_g + g, 0)),
        compiler_params=_cparams(("arbitrary", "arbitrary")),
        name="neighborhood_attn",
    )(q, k, v, k, v, nb)


F_N2_CHUNK = 8
F_K1_CHUNK = 8


def _four1_kernel(x_ref, w_ref, t_ref):
    w = w_ref[...]
    for j in range(F_N2_CHUNK):
        res = jnp.dot(w, x_ref[:, j * 512:(j + 1) * 512], preferred_element_type=F32)
        t_ref[0, j] = res[:64].astype(BF16)
        t_ref[1, j] = res[64:].astype(BF16)


def _four2_kernel(t_ref, m_ref, cs_ref, y_ref):
    cs = cs_ref[...]
    for j in range(F_K1_CHUNK):
        lanes = slice(j * 512, (j + 1) * 512)
        tt = jnp.concatenate([t_ref[0, :, lanes], t_ref[1, :, lanes]], axis=0)
        pp = jnp.dot(m_ref[j], tt, preferred_element_type=F32)
        pc = jnp.concatenate([pp[:64], pp[64:]], axis=1).astype(BF16)
        y_ref[:, lanes] = jnp.dot(pc, cs, preferred_element_type=F32).astype(BF16)


def _fourier_tables():
    a = np.arange(64)
    ang1 = 2.0 * np.pi * np.outer(a, a) / 64.0
    w1 = np.concatenate([np.cos(ang1), -np.sin(ang1)], axis=0)
    k1 = a[:, None, None]
    k2 = a[None, :, None]
    n2 = a[None, None, :]
    theta = 2.0 * np.pi * (n2 * k2 / 64.0 + n2 * k1 / 4096.0)
    mr = np.cos(theta) / 64.0
    mi = -np.sin(theta) / 64.0
    m = np.concatenate([np.concatenate([mr, -mi], axis=2),
                        np.concatenate([mi, mr], axis=2)], axis=1)
    c = np.arange(128)
    angc = 2.0 * np.pi * np.outer(c, c) / 128.0
    eye4 = np.eye(4)
    cc = np.kron(eye4, np.cos(angc)) / np.sqrt(128.0)
    sc = np.kron(eye4, np.sin(angc)) / np.sqrt(128.0)
    cs = np.concatenate([cc, sc], axis=0)
    return tuple(jnp.asarray(t, F32).astype(BF16) for t in (w1, m, cs))


def _fourier(f):
    w1, m, cs = _fourier_tables()
    fv = f.reshape(N_ALL // 64, 64 * 512)
    n_c = 64 // F_N2_CHUNK
    t = pl.pallas_call(
        _four1_kernel,
        out_shape=jax.ShapeDtypeStruct((BATCH, 2, 64, 64, 512), BF16),
        grid=(BATCH, n_c),
        in_specs=[pl.BlockSpec((64, F_N2_CHUNK * 512), lambda b, c: (b, c)),
                  pl.BlockSpec((128, 64), lambda b, c: (0, 0))],
        out_specs=pl.BlockSpec((None, 2, F_N2_CHUNK, 64, 512), lambda b, c: (b, 0, c, 0, 0)),
        compiler_params=_cparams(("arbitrary", "arbitrary")),
        name="fourier_rows",
    )(fv, w1)
    n_k = 64 // F_K1_CHUNK
    t2 = t.reshape(BATCH, 2, 64, 64 * 512)
    y = pl.pallas_call(
        _four2_kernel,
        out_shape=jax.ShapeDtypeStruct((BATCH * 64, 64 * 512), BF16),
        grid=(BATCH, n_k),
        in_specs=[pl.BlockSpec((None, 2, 64, F_K1_CHUNK * 512), lambda b, c: (b, 0, 0, c)),
                  pl.BlockSpec((F_K1_CHUNK, 128, 128), lambda b, c: (c, 0, 0)),
                  pl.BlockSpec((1024, 512), lambda b, c: (0, 0))],
        out_specs=pl.BlockSpec((64, F_K1_CHUNK * 512), lambda b, c: (b, c)),
        compiler_params=_cparams(("arbitrary", "arbitrary")),
        name="fourier_cols",
    )(t2, m, cs)
    return y.reshape(N_MAIN, 512)


OUT_TILES = 2
TO = OUT_TILES * TM


def _route(h2, rwt_ref, rb_ref, carry):
    logits = lax.dot_general(rwt_ref[...], h2, (((1,), (1,)), ((), ())),
                             preferred_element_type=F32,
                             precision=lax.Precision.HIGHEST) + rb_ref[...]
    eidx = lax.broadcasted_iota(I32, logits.shape, 0)
    vals = logits
    sels, tops, idxs = [], [], []
    for _ in range(TOP_K):
        m = jnp.max(vals, axis=0, keepdims=True)
        idx = jnp.min(jnp.where(vals == m, eidx, N_EXPERTS), axis=0, keepdims=True)
        sel = eidx == idx
        sels.append(sel)
        tops.append(m)
        idxs.append(idx)
        vals = jnp.where(sel, -jnp.inf, vals)
    ex = [jnp.exp(t - tops[0]) for t in tops]
    den = ex[0] + ex[1] + ex[2] + ex[3]
    onehot = jnp.zeros(logits.shape, F32)
    for sel in sels:
        onehot = onehot + sel.astype(F32)
    r_i = lax.broadcasted_iota(I32, (TM, TM), 0)
    c_i = lax.broadcasted_iota(I32, (TM, TM), 1)
    upper = (r_i < c_i).astype(BF16)
    prefix = jnp.dot(onehot.astype(BF16), upper, preferred_element_type=F32)
    base = carry[:, 0:1] + prefix
    tw = [e / den for e in ex]
    rk = [jnp.sum(jnp.where(sel, base, 0.0), axis=0, keepdims=True).astype(I32) for sel in sels]
    return idxs, tw, rk, carry + jnp.sum(onehot, axis=1, keepdims=True)


def _out_tail(i, x_ref, y, mod_ref, g_ref, rwt_ref, rb_ref, carry_ref,
              xn_ref, h2_ref, te_ref, tw_ref, rk_ref, cnt_ref):
    @pl.when(i == 0)
    def _():
        carry_ref[...] = jnp.zeros_like(carry_ref)

    carry = carry_ref[...]
    for t in range(OUT_TILES):
        rows = slice(t * TM, (t + 1) * TM)
        xn = x_ref[rows, :] + mod_ref[2:3, :] * y[rows]
        xn_ref[rows, :] = xn
        h2 = _rms_mod(xn, g_ref[...], mod_ref[4:5, :], mod_ref[3:4, :])
        h2_ref[rows, :] = h2
        te, tw, rk, carry = _route(h2, rwt_ref, rb_ref, carry)
        for k in range(TOP_K):
            te_ref[k:k + 1, rows] = te[k]
            tw_ref[k:k + 1, rows] = tw[k]
            rk_ref[k:k + 1, rows] = rk[k]
    carry_ref[...] = carry
    cnt_ref[...] = carry


def _out_even_kernel(grp_ref, first_ref, last_ref,
                     x_ref, a_ref, gb_ref, z_ref, zp_ref, zn_ref, cw_ref, w_ref, mod_ref, g_ref,
                     rwt_ref, rb_ref,
                     xn_ref, h2_ref, te_ref, tw_ref, rk_ref, cnt_ref, carry_ref):
    i = pl.program_id(0)
    z = z_ref[...]
    rid = lax.broadcasted_iota(I32, z.shape, 0)
    zm1 = jnp.where(rid == 0, zp_ref[7:8, :], pltpu.roll(z, 1, 0))
    zp1 = jnp.where(rid == TO - 1, zn_ref[0:1, :], pltpu.roll(z, TO - 1, 0))
    for t in range(OUT_TILES):
        zm1 = jnp.where(jnp.logical_and(rid == t * TM, first_ref[OUT_TILES * i + t] == 1), 0.0, zm1)
        zp1 = jnp.where(jnp.logical_and(rid == (t + 1) * TM - 1, last_ref[OUT_TILES * i + t] == 1), 0.0, zp1)
    conv = gb_ref[...] * (zm1 * cw_ref[0:1, :] + z * cw_ref[1:2, :] + zp1 * cw_ref[2:3, :])
    y = (jnp.dot(a_ref[...], w_ref[0:512, :], preferred_element_type=F32)
         + jnp.dot(conv.astype(BF16), w_ref[512:1024, :], preferred_element_type=F32))
    _out_tail(i, x_ref, y, mod_ref, g_ref, rwt_ref, rb_ref, carry_ref,
              xn_ref, h2_ref, te_ref, tw_ref, rk_ref, cnt_ref)


def _out_odd_kernel(grp_ref, x_ref, a_ref, f_ref, w_ref, mod_ref, g_ref, rwt_ref, rb_ref,
                    xn_ref, h2_ref, te_ref, tw_ref, rk_ref, cnt_ref, carry_ref):
    i = pl.program_id(0)
    y = (jnp.dot(a_ref[...], w_ref[0:512, :], preferred_element_type=F32)
         + jnp.dot(f_ref[...], w_ref[512:1024, :], preferred_element_type=F32))
    _out_tail(i, x_ref, y, mod_ref, g_ref, rwt_ref, rb_ref, carry_ref,
              xn_ref, h2_ref, te_ref, tw_ref, rk_ref, cnt_ref)


def _out_shapes(n_rows):
    return (jax.ShapeDtypeStruct((n_rows, D), F32), jax.ShapeDtypeStruct((n_rows, D), F32),
            jax.ShapeDtypeStruct((TOP_K, n_rows), I32), jax.ShapeDtypeStruct((TOP_K, n_rows), F32),
            jax.ShapeDtypeStruct((TOP_K, n_rows), I32), jax.ShapeDtypeStruct((N_EXPERTS, 128), F32))


def _out_even(xall, attn, gb, z, conv_w, w_bf, mod, g, rwt, rb):
    grp, _, first, last = _tile_tables()
    n_rows = N_ALL
    zblocks = n_rows // 8
    im = lambda f: (lambda i, grp, fi, la: f(i))
    tile = lambda n: pl.BlockSpec((TO, n), im(lambda i: (i, 0)))
    const = lambda shape: pl.BlockSpec(shape, im(lambda i: (0,) * len(shape)))
    tk = pl.BlockSpec((TOP_K, TO), im(lambda i: (0, i)))
    return pl.pallas_call(
        _out_even_kernel,
        out_shape=_out_shapes(n_rows),
        grid_spec=pltpu.PrefetchScalarGridSpec(
            num_scalar_prefetch=3,
            grid=(NT_ALL // OUT_TILES,),
            in_specs=[tile(D), tile(512), tile(512), tile(512),
                      pl.BlockSpec((8, 512), im(lambda i: (jnp.maximum(i * (TO // 8) - 1, 0), 0))),
                      pl.BlockSpec((8, 512), im(lambda i: (jnp.minimum((i + 1) * (TO // 8), zblocks - 1), 0))),
                      const((3, 512)), const((D, D)),
                      pl.BlockSpec((None, 6, D), lambda i, grp, fi, la: (grp[OUT_TILES * i], 0, 0)),
                      const((1, D)), const((N_EXPERTS, D)), const((N_EXPERTS, 1))],
            out_specs=(tile(D), tile(D), tk, tk, tk, const((N_EXPERTS, 128))),
            scratch_shapes=[pltpu.VMEM((N_EXPERTS, 128), F32)],
        ),
        compiler_params=_cparams(("arbitrary",)),
        name="out_proj_even",
    )(jnp.asarray(grp), jnp.asarray(first), jnp.asarray(last),
      xall, attn, gb, z, z, z, conv_w, w_bf, mod, g, rwt, rb)


def _out_odd(xall, attn, four, w_bf, mod, g, rwt, rb):
    grp, _, _, _ = _tile_tables()
    n_rows = N_MAIN
    im = lambda f: (lambda i, grp: f(i))
    tile = lambda n: pl.BlockSpec((TO, n), im(lambda i: (i, 0)))
    const = lambda shape: pl.BlockSpec(shape, im(lambda i: (0,) * len(shape)))
    tk = pl.BlockSpec((TOP_K, TO), im(lambda i: (0, i)))
    return pl.pallas_call(
        _out_odd_kernel,
        out_shape=_out_shapes(n_rows),
        grid_spec=pltpu.PrefetchScalarGridSpec(
            num_scalar_prefetch=1,
            grid=(NT_MAIN // OUT_TILES,),
            in_specs=[tile(D), tile(512), tile(512), const((D, D)),
                      pl.BlockSpec((None, 6, D), lambda i, grp: (grp[OUT_TILES * i], 0, 0)),
                      const((1, D)), const((N_EXPERTS, D)), const((N_EXPERTS, 1))],
            out_specs=(tile(D), tile(D), tk, tk, tk, const((N_EXPERTS, 128))),
            scratch_shapes=[pltpu.VMEM((N_EXPERTS, 128), F32)],
        ),
        compiler_params=_cparams(("arbitrary",)),
        name="out_proj_odd",
    )(jnp.asarray(grp), xall, attn, four, w_bf, mod, g, rwt, rb)


def _moe_plan(counts_f, top_e_t, rank_t, n_tok):
    counts = counts_f[:, 0].astype(I32)
    padded = (counts + TMM - 1) // TMM * TMM
    e_i = jnp.arange(N_EXPERTS, dtype=I32)
    incl = e_i[None, :] <= e_i[:, None]
    pad_end = jnp.sum(jnp.where(incl, padded[None, :], 0), axis=1)
    pad_start = pad_end - padded
    sel = top_e_t[None] == e_i[:, None, None]
    dest = jnp.sum(jnp.where(sel, pad_start[:, None, None], 0), axis=0) + rank_t
    n_blocks = n_tok * TOP_K // TMM + N_EXPERTS
    blk_start = jnp.arange(n_blocks, dtype=I32) * TMM
    block_e = jnp.minimum(jnp.sum((blk_start[:, None] >= pad_end[None, :]).astype(I32), axis=1),
                          N_EXPERTS - 1)
    n_used = (pad_end[-1] // TMM).reshape(1)
    n_tiles = n_tok // TT
    dest_flat = dest.reshape(TOP_K, n_tiles, TT).transpose(1, 0, 2).reshape(-1)
    return dest_flat, block_e, n_used, pad_end, padded


def _dispatch_kernel(pe_ref, pd_ref, nu_ref, dst_ref, h2_ref, xs_hbm, zbuf, sem, sem_z):
    i = pl.program_id(0)
    n_blocks = xs_hbm.shape[0] // TMM

    @pl.when(i == 0)
    def _():
        zbuf[...] = jnp.zeros_like(zbuf)

        def fill(b):
            return pltpu.make_async_copy(zbuf, xs_hbm.at[pl.ds(pl.multiple_of(b * TMM, TMM), TMM)], sem_z)

        def fill_expert(e, n):
            has_rows = pd_ref[e] > 0

            @pl.when(has_rows)
            def _():
                fill(pe_ref[e] // TMM - 1).start()

            return n + has_rows.astype(I32)

        def fill_tail(b, c):
            fill(b).start()
            return c

        def fill_wait(j, c):
            fill(0).wait()
            return c

        n_fill = lax.fori_loop(0, N_EXPERTS, fill_expert, 0) + n_blocks - nu_ref[0]
        lax.fori_loop(nu_ref[0], n_blocks, fill_tail, 0)
        lax.fori_loop(0, n_fill, fill_wait, 0)

    base = i * (TOP_K * TT)

    def issue(j, c):
        for k in range(TOP_K):
            pltpu.make_async_copy(h2_ref.at[pl.ds(j, 1)], xs_hbm.at[pl.ds(dst_ref[base + k * TT + j], 1)],
                                  sem).start(priority=k % 2)
        return c

    lax.fori_loop(0, TT, issue, 0, unroll=4)
    for _ in range(TOP_K):
        pltpu.make_async_copy(h2_ref, xs_hbm.at[pl.ds(0, TT)], sem).wait()


def _dispatch(h2, dest_flat, pad_end, padded, n_used, cap):
    n_tiles = h2.shape[0] // TT
    return pl.pallas_call(
        _dispatch_kernel,
        out_shape=jax.ShapeDtypeStruct((cap, D), F32),
        grid_spec=pltpu.PrefetchScalarGridSpec(
            num_scalar_prefetch=4,
            grid=(n_tiles,),
            in_specs=[pl.BlockSpec((TT, D), lambda i, pe, pd, nu, dst: (i, 0))],
            out_specs=pl.BlockSpec(memory_space=pl.ANY),
            scratch_shapes=[pltpu.VMEM((TMM, D), F32), pltpu.SemaphoreType.DMA, pltpu.SemaphoreType.DMA],
        ),
        compiler_params=_cparams(("arbitrary",)),
        name="moe_dispatch",
    )(pad_end, padded, n_used, dest_flat, h2)


def _moe_kernel(be_ref, nu_ref, x_ref, wgu_ref, bgu_ref, wdn_ref, bdn_ref, y_ref, wgu_bf, wdn_bf):
    i = pl.program_id(0)
    prev = be_ref[jnp.maximum(i - 1, 0)]

    @pl.when((i < nu_ref[0]) & ((i == 0) | (be_ref[i] != prev)))
    def _():
        wgu_bf[...] = wgu_ref[...].astype(BF16)
        wdn_bf[...] = wdn_ref[...].astype(BF16)

    @pl.when(i < nu_ref[0])
    def _():
        xb = x_ref[...].astype(BF16)
        gu = jnp.dot(xb, wgu_bf[...], preferred_element_type=F32) + bgu_ref[...]
        gate = jnp.minimum(gu[:, :D], SWIGLU_LIMIT)
        up = jnp.clip(gu[:, D:], -SWIGLU_LIMIT, SWIGLU_LIMIT)
        act = (up + 1.0) * (gate * (1.0 / (1.0 + jnp.exp(-SWIGLU_ALPHA * gate))))
        y_ref[...] = jnp.dot(act.astype(BF16), wdn_bf[...], preferred_element_type=F32) + bdn_ref[...]

    @pl.when(i >= nu_ref[0])
    def _():
        y_ref[...] = jnp.zeros_like(y_ref)


def _moe(layer, xs, block_e, n_used, w_gu, b_gu, w_dn, b_dn):
    n_blocks = block_e.shape[0]
    n_l = w_gu.shape[0]
    blk = lambda i, be, nu: (jnp.minimum(i, nu[0] - 1), 0)
    out_blk = lambda i, be, nu: (i, 0)
    exp4 = lambda i, be, nu: (layer, be[jnp.minimum(i, nu[0] - 1)], 0, 0)
    return pl.pallas_call(
        _moe_kernel,
        out_shape=jax.ShapeDtypeStruct((n_blocks * TMM, D), F32),
        grid_spec=pltpu.PrefetchScalarGridSpec(
            num_scalar_prefetch=2,
            grid=(n_blocks,),
            in_specs=[pl.BlockSpec((TMM, D), blk),
                      pl.BlockSpec((None, None, D, 2 * D), exp4),
                      pl.BlockSpec((None, None, 1, 2 * D), exp4),
                      pl.BlockSpec((None, None, D, D), exp4),
                      pl.BlockSpec((None, None, 1, D), exp4)],
            out_specs=pl.BlockSpec((TMM, D), out_blk),
            scratch_shapes=[pltpu.VMEM((D, 2 * D), BF16), pltpu.VMEM((D, D), BF16)],
        ),
        compiler_params=_cparams(("arbitrary",)),
        name="moe_experts",
    )(block_e, n_used, xs, w_gu, b_gu.reshape(n_l, N_EXPERTS, 1, 2 * D), w_dn,
      b_dn.reshape(n_l, N_EXPERTS, 1, D))


def _combine_kernel(final, grp_ref, dst_ref, ys_hbm, x_ref, tw_ref, mod_ref, fn_ref, o_ref, buf, sem):
    i = pl.program_id(0)
    n_tiles = pl.num_programs(0)

    def gather(tile, slot):
        base = tile * (TOP_K * TT)

        def issue(j, c):
            for k in range(TOP_K):
                pltpu.make_async_copy(ys_hbm.at[pl.ds(dst_ref[base + k * TT + j], 1)],
                                      buf.at[slot, k, pl.ds(j, 1)], sem.at[slot]).start(priority=k % 2)
            return c

        lax.fori_loop(0, TT, issue, 0, unroll=4)

    @pl.when(i == 0)
    def _():
        gather(0, 0)

    @pl.when(i + 1 < n_tiles)
    def _():
        gather(i + 1, (i + 1) % 2)

    slot = i % 2
    for k in range(TOP_K):
        pltpu.make_async_copy(ys_hbm.at[pl.ds(0, TT)], buf.at[slot, k], sem.at[slot]).wait()
    tw = tw_ref[...]
    acc = tw[:, 0:1] * buf[slot, 0]
    for k in range(1, TOP_K):
        acc = acc + tw[:, k:k + 1] * buf[slot, k]
    out = x_ref[...] + mod_ref[5:6, :] * acc
    if final:
        ms = jnp.mean(out * out, axis=-1, keepdims=True)
        out = out * lax.rsqrt(ms + EPS) * fn_ref[...]
    o_ref[...] = out


def _combine(xn, ys, dest_flat, top_w_t, mod, final_norm, n_tok, final):
    grp, _, _, _ = _tile_tables()
    n_tiles = n_tok // TT
    tw = top_w_t.T
    return pl.pallas_call(
        functools.partial(_combine_kernel, final),
        out_shape=jax.ShapeDtypeStruct((n_tok, D), F32),
        grid_spec=pltpu.PrefetchScalarGridSpec(
            num_scalar_prefetch=2,
            grid=(n_tiles,),
            in_specs=[pl.BlockSpec(memory_space=pl.ANY),
                      pl.BlockSpec((TT, D), lambda i, grp, dst: (i, 0)),
                      pl.BlockSpec((TT, TOP_K), lambda i, grp, dst: (i, 0)),
                      pl.BlockSpec((None, 6, D), lambda i, grp, dst: (grp[i], 0, 0)),
                      pl.BlockSpec((1, D), lambda i, grp, dst: (0, 0))],
            out_specs=pl.BlockSpec((TT, D), lambda i, grp, dst: (i, 0)),
            scratch_shapes=[pltpu.VMEM((2, TOP_K, TT, D), F32), pltpu.SemaphoreType.DMA((2,))],
        ),
        compiler_params=_cparams(("arbitrary",)),
        name="moe_combine",
    )(jnp.asarray(grp), dest_flat, ys, xn, tw, mod, final_norm.reshape(1, D))


def _moe_layer(layer, xn, h2, top_e_t, top_w_t, rank_t, counts, mod, w_gu, b_gu, w_dn, b_dn, final_norm, final):
    n_tok = xn.shape[0]
    dest_flat, block_e, n_used, pad_end, padded = _moe_plan(counts, top_e_t, rank_t, n_tok)
    xs = _dispatch(h2, dest_flat, pad_end, padded, n_used, block_e.shape[0] * TMM)
    ys = _moe(layer, xs, block_e, n_used, w_gu, b_gu, w_dn, b_dn)
    return _combine(xn, ys, dest_flat, top_w_t, mod, final_norm, n_tok, final)


def kernel(x, c, ctx, c_ctx, ada_w, ada_b, norm_mix, norm_ffn, even_w_in, even_w_out, even_conv_w, even_sink, odd_w_in, odd_w_out, odd_rpb, router_w, router_b, moe_w_gu, moe_b_gu, moe_w_dn, moe_b_dn, final_norm):
    xall = jnp.concatenate([x.reshape(N_MAIN, D), ctx.reshape(N_CTX, D)], axis=0)
    cc = jnp.concatenate([c, c_ctx[None, :], jnp.zeros((3, D), F32)], axis=0)
    mod = _ada(cc, ada_w, ada_b).reshape(2, 8, 6, D)
    cos_f, sin_f = _rope_tables()

    q, k, ks, v, vs, gb, z = _in_even(xall, mod[0], norm_mix[0:1], even_w_in[0].astype(BF16), cos_f, sin_f)
    attn = jnp.concatenate([_win_attn(even_sink[0], q, k, ks, v, vs),
                            _ctx_attn(even_sink[0], q, k, ks, v, vs)], axis=0)
    xn, h2, te, tw, rk, cnt = _out_even(xall, attn, gb, z, even_conv_w[0], even_w_out[0].astype(BF16),
                                        mod[0], norm_ffn[0:1], router_w[0].T, router_b[0][:, None])
    xall = _moe_layer(0, xn, h2, te, tw, rk, cnt, mod[0], moe_w_gu, moe_b_gu, moe_w_dn, moe_b_dn,
                      final_norm, False)

    q, k, v, f = _in_odd(xall, mod[1], norm_mix[1:2], odd_w_in[0].astype(BF16))
    attn = _na_attn(q, k, v, _na_bias(odd_rpb[0]))
    four = _fourier(f)
    xn, h2, te, tw, rk, cnt = _out_odd(xall, attn, four, odd_w_out[0].astype(BF16),
                                       mod[1], norm_ffn[1:2], router_w[1].T, router_b[1][:, None])
    out = _moe_layer(1, xn, h2, te, tw, rk, cnt, mod[1], moe_w_gu, moe_b_gu, moe_w_dn, moe_b_dn,
                     final_norm, True)
    return out.reshape(BATCH, SEQ, D)
```

```python
import functools

import numpy as np
import jax
import jax.numpy as jnp
from jax import lax
from jax.experimental import pallas as pl
from jax.experimental.pallas import tpu as pltpu

F32 = jnp.float32
BF16 = jnp.bfloat16
I32 = jnp.int32

D = 1024
BATCH = 4
SEQ = 4096
CTX = 256
GRID_W = 64
HEAD_DIM = 64
EPS = 1e-6
ROPE_THETA = 10000.0
N_EXPERTS = 32
TOP_K = 4
SWIGLU_LIMIT = 7.0
SWIGLU_ALPHA = 1.702
NA_ROWS = 8
NA_COLS = 16

N_MAIN = BATCH * SEQ
N_CTX = BATCH * CTX
N_ALL = N_MAIN + N_CTX
TM = 256
NT_MAIN = N_MAIN // TM
NT_ALL = N_ALL // TM
TILES_PER_SEQ = SEQ // TM
TMM = 512
TT = 512
VMEM_LIMIT = 56 * 1024 * 1024


def _cparams(sem, vmem=VMEM_LIMIT):
    return pltpu.CompilerParams(dimension_semantics=sem, vmem_limit_bytes=vmem)


def _rms_mod(x, g, sc, sh):
    ms = jnp.mean(x * x, axis=-1, keepdims=True)
    return (x * lax.rsqrt(ms + EPS) * g) * (1.0 + sc) + sh


def _ada_kernel(c_ref, w_ref, b_ref, o_ref):
    c = c_ref[...]
    s = c * (1.0 / (1.0 + jnp.exp(-c)))
    o_ref[...] = jnp.dot(s, w_ref[...], preferred_element_type=F32,
                         precision=lax.Precision.HIGHEST) + b_ref[...]


def _ada(cc, ada_w, ada_b):
    n_l = ada_w.shape[0]
    tn = 1024
    return pl.pallas_call(
        _ada_kernel,
        out_shape=jax.ShapeDtypeStruct((n_l, 8, 6 * D), F32),
        grid=(n_l, 6 * D // tn),
        in_specs=[pl.BlockSpec((8, D), lambda l, j: (0, 0)),
                  pl.BlockSpec((None, D, tn), lambda l, j: (l, 0, j)),
                  pl.BlockSpec((None, 1, tn), lambda l, j: (l, 0, j))],
        out_specs=pl.BlockSpec((None, 8, tn), lambda l, j: (l, 0, j)),
        compiler_params=_cparams(("arbitrary", "arbitrary")),
        name="ada_mod",
    )(cc, ada_w, ada_b.reshape(n_l, 1, 6 * D))


def _rope_apply(t, cos, sin):
    n = t.shape[1]
    lane = lax.broadcasted_iota(I32, t.shape, 1)
    fwd = pltpu.roll(t, n - 32, 1)
    bwd = pltpu.roll(t, 32, 1)
    rot = jnp.where((lane % 64) < 32, fwd, bwd)
    reps = n // 128
    cosf = jnp.concatenate([cos] * reps, axis=1) if reps > 1 else cos
    sinf = jnp.concatenate([sin] * reps, axis=1) if reps > 1 else sin
    return t * cosf + rot * sinf


def _in_even_kernel(grp_ref, rblk_ref, x_ref, c_ref, mod_ref, g_ref, w_ref, cos_ref, sin_ref,
                    q_ref, k_ref, ks_ref, v_ref, vs_ref, gb_ref, z_ref):
    xt = jnp.where(pl.program_id(0) < NT_MAIN, x_ref[...], c_ref[...])
    h = _rms_mod(xt, g_ref[...], mod_ref[1:2, :], mod_ref[0:1, :])
    p = jnp.dot(h.astype(BF16), w_ref[...], preferred_element_type=F32)
    cos = cos_ref[...]
    sin = sin_ref[...]
    q = _rope_apply(p[:, 0:512], cos, sin) * (HEAD_DIM ** -0.5)
    k = _rope_apply(p[:, 512:640], cos, sin)
    v = p[:, 640:768]
    q_ref[...] = q.astype(BF16)
    k_ref[...] = k.astype(BF16)
    ks_ref[...] = pltpu.roll(k, 64, 1).astype(BF16)
    v_ref[...] = v.astype(BF16)
    vs_ref[...] = pltpu.roll(v, 64, 1).astype(BF16)
    gb_ref[...] = p[:, 768:1280]
    z_ref[...] = p[:, 1280:1792] * p[:, 1792:2304]


def _in_odd_kernel(grp_ref, x_ref, mod_ref, g_ref, w_ref, q_ref, k_ref, v_ref, f_ref):
    h = _rms_mod(x_ref[...], g_ref[...], mod_ref[1:2, :], mod_ref[0:1, :])
    p = jnp.dot(h.astype(BF16), w_ref[...], preferred_element_type=F32)
    q_ref[...] = (p[:, 0:512] * (HEAD_DIM ** -0.5)).astype(BF16)
    k_ref[...] = p[:, 512:1024].astype(BF16)
    v_ref[...] = p[:, 1024:1536].astype(BF16)
    f_ref[...] = p[:, 1536:2048].astype(BF16)


def _tile_tables():
    t = np.arange(NT_ALL)
    main = t < NT_MAIN
    grp = np.where(main, t // TILES_PER_SEQ, BATCH).astype(np.int32)
    rblk = np.where(main, t % TILES_PER_SEQ, TILES_PER_SEQ).astype(np.int32)
    first = np.where(main, (t % TILES_PER_SEQ) == 0, True).astype(np.int32)
    last = np.where(main, (t % TILES_PER_SEQ) == TILES_PER_SEQ - 1, True).astype(np.int32)
    return grp, rblk, first, last


def _rope_tables():
    t = jnp.arange(SEQ, dtype=I32)
    row = (t // GRID_W).astype(F32)
    col = (t % GRID_W).astype(F32)
    n_freq = HEAD_DIM // 4
    inv_freq = jnp.power(ROPE_THETA, -jnp.arange(n_freq, dtype=F32) / n_freq)
    ang = jnp.concatenate([row[:, None] * inv_freq, col[:, None] * inv_freq], axis=-1)
    cos = jnp.cos(ang)
    sin = jnp.sin(ang)
    cos_f = jnp.concatenate([cos, cos, cos, cos], axis=1)
    sin_f = jnp.concatenate([-sin, sin, -sin, sin], axis=1)
    cos_f = jnp.concatenate([cos_f, jnp.ones((TM, 128), F32)], axis=0)
    sin_f = jnp.concatenate([sin_f, jnp.zeros((TM, 128), F32)], axis=0)
    return cos_f, sin_f


def _in_even(x2d, c2d, mod, g, w_bf, cos_f, sin_f):
    grp, rblk, _, _ = _tile_tables()
    row = lambda n, dt: jax.ShapeDtypeStruct((N_ALL, n), dt)
    tile = lambda n: pl.BlockSpec((TM, n), lambda i, grp, rb: (i, 0))
    x_spec = pl.BlockSpec((TM, D), lambda i, grp, rb: (jnp.minimum(i, NT_MAIN - 1), 0))
    c_spec = pl.BlockSpec((TM, D), lambda i, grp, rb: (jnp.maximum(i - NT_MAIN, 0), 0))
    return pl.pallas_call(
        _in_even_kernel,
        out_shape=(row(512, BF16), row(128, BF16), row(128, BF16), row(128, BF16), row(128, BF16),
                   row(512, F32), row(512, F32)),
        grid_spec=pltpu.PrefetchScalarGridSpec(
            num_scalar_prefetch=2,
            grid=(NT_ALL,),
            in_specs=[x_spec, c_spec,
                      pl.BlockSpec((None, 6, D), lambda i, grp, rb: (grp[i], 0, 0)),
                      pl.BlockSpec((1, D), lambda i, grp, rb: (0, 0)),
                      pl.BlockSpec((D, 2304), lambda i, grp, rb: (0, 0)),
                      pl.BlockSpec((TM, 128), lambda i, grp, rb: (rb[i], 0)),
                      pl.BlockSpec((TM, 128), lambda i, grp, rb: (rb[i], 0))],
            out_specs=(tile(512), tile(128), tile(128), tile(128), tile(128), tile(512), tile(512)),
        ),
        compiler_params=_cparams(("arbitrary",)),
        name="in_proj_even",
    )(jnp.asarray(grp), jnp.asarray(rblk), x2d, c2d, mod, g, w_bf, cos_f, sin_f)


def _in_odd(xall, mod, g, w_bf):
    grp, _, _, _ = _tile_tables()
    row = lambda n, dt: jax.ShapeDtypeStruct((N_ALL, n), dt)
    tile = lambda n: pl.BlockSpec((TM, n), lambda i, grp: (i, 0))
    return pl.pallas_call(
        _in_odd_kernel,
        out_shape=(row(512, BF16), row(512, BF16), row(512, BF16), row(512, BF16)),
        grid_spec=pltpu.PrefetchScalarGridSpec(
            num_scalar_prefetch=1,
            grid=(NT_ALL,),
            in_specs=[tile(D),
                      pl.BlockSpec((None, 6, D), lambda i, grp: (grp[i], 0, 0)),
                      pl.BlockSpec((1, D), lambda i, grp: (0, 0)),
                      pl.BlockSpec((D, 2048), lambda i, grp: (0, 0))],
            out_specs=(tile(512), tile(512), tile(512), tile(512)),
        ),
        compiler_params=_cparams(("arbitrary",)),
        name="in_proj_odd",
    )(jnp.asarray(grp), xall, mod, g, w_bf)


def _nt(a, b):
    return lax.dot_general(a, b, (((1,), (1,)), ((), ())), preferred_element_type=F32)


def _half_mask(shape, half):
    lane = lax.broadcasted_iota(I32, shape, 1)
    return (lane < 64) if half == 0 else (lane >= 64)


def _win_kernel(sink_ref, q_ref, k_ref, ks_ref, v_ref, vs_ref, kc_ref, ksc_ref, vc_ref, vsc_ref, o_ref):
    n = pl.program_id(1)
    start = pl.multiple_of(jnp.clip((n - 1) * 128, 0, SEQ - 384), 128)
    win = pl.ds(start, 384)
    row = lax.broadcasted_iota(I32, (256, 384), 0)
    col = lax.broadcasted_iota(I32, (256, 384), 1)
    valid = jnp.abs((n * 128 + row % 128) - (start + col)) <= 128
    first = lax.broadcasted_iota(I32, (256, 1), 0) < 128
    kk = (jnp.concatenate([k_ref[win, :], kc_ref[...]], axis=0),
          jnp.concatenate([ks_ref[win, :], ksc_ref[...]], axis=0))
    vv = (jnp.concatenate([v_ref[win, :], vc_ref[...]], axis=0),
          jnp.concatenate([vs_ref[win, :], vsc_ref[...]], axis=0))
    outs = {}
    for hk in range(2):
        for hf in range(2):
            swapped = 0 if hk == hf else 1
            chunks = (2 * hk, 2 * hk + 1)
            qs = [q_ref[:, c * 128:(c + 1) * 128] for c in chunks]
            q2 = jnp.concatenate([jnp.where(_half_mask(t.shape, hf), t, jnp.zeros_like(t)) for t in qs],
                                 axis=0)
            s = _nt(q2, kk[swapped])
            s_loc = jnp.where(valid, s[:, :384], -jnp.inf)
            s_ctx = s[:, 384:]
            sink = jnp.where(first, sink_ref[2 * chunks[0] + hf], sink_ref[2 * chunks[1] + hf])
            m = jnp.maximum(jnp.maximum(jnp.max(s_loc, axis=1, keepdims=True),
                                        jnp.max(s_ctx, axis=1, keepdims=True)), sink)
            p_loc = jnp.exp(s_loc - m)
            p_ctx = jnp.exp(s_ctx - m)
            den = (jnp.sum(p_loc, axis=1, keepdims=True) + jnp.sum(p_ctx, axis=1, keepdims=True)
                   + jnp.exp(sink - m))
            p = jnp.concatenate([p_loc, p_ctx], axis=1).astype(BF16)
            o = jnp.dot(p, vv[swapped], preferred_element_type=F32) / den
            outs[(chunks[0], hf)] = o[:128]
            outs[(chunks[1], hf)] = o[128:]
    for c in range(4):
        o_ref[:, c * 128:(c + 1) * 128] = jnp.where(_half_mask((128, 128), 0),
                                                    outs[(c, 0)], outs[(c, 1)]).astype(BF16)


def _win_attn(sink, q, k, ks, v, vs):
    nb = SEQ // 128
    seq_spec = pl.BlockSpec((SEQ, 128), lambda b, n: (b, 0))
    ctx_spec = pl.BlockSpec((CTX, 128), lambda b, n: (N_MAIN // CTX + b, 0))
    return pl.pallas_call(
        _win_kernel,
        out_shape=jax.ShapeDtypeStruct((N_MAIN, 512), BF16),
        grid=(BATCH, nb),
        in_specs=[pl.BlockSpec(memory_space=pltpu.SMEM),
                  pl.BlockSpec((128, 512), lambda b, n: (b * (SEQ // 128) + n, 0)),
                  seq_spec, seq_spec, seq_spec, seq_spec,
                  ctx_spec, ctx_spec, ctx_spec, ctx_spec],
        out_specs=pl.BlockSpec((128, 512), lambda b, n: (b * (SEQ // 128) + n, 0)),
        compiler_params=_cparams(("arbitrary", "arbitrary")),
        name="window_attn",
    )(sink, q, k, ks, v, vs, k, ks, v, vs)


def _ctx_attn_kernel(sink_ref, q_ref, k_ref, ks_ref, v_ref, vs_ref, o_ref):
    kk = (k_ref[...], ks_ref[...])
    vv = (v_ref[...], vs_ref[...])
    for c in range(4):
        qc = q_ref[:, c * 128:(c + 1) * 128]
        halves = []
        for hf in range(2):
            h = 2 * c + hf
            swapped = 0 if (h // 4) == hf else 1
            qm = jnp.where(_half_mask(qc.shape, hf), qc, jnp.zeros_like(qc))
            s = _nt(qm, kk[swapped])
            sink = sink_ref[h]
            m = jnp.maximum(jnp.max(s, axis=1, keepdims=True), sink)
            p = jnp.exp(s - m)
            den = jnp.sum(p, axis=1, keepdims=True) + jnp.exp(sink - m)
            halves.append(jnp.dot(p.astype(BF16), vv[swapped], preferred_element_type=F32) / den)
        o_ref[:, c * 128:(c + 1) * 128] = jnp.where(_half_mask(halves[0].shape, 0),
                                                    halves[0], halves[1]).astype(BF16)


def _ctx_attn(sink, q, k, ks, v, vs):
    ctx_spec = lambda n: pl.BlockSpec((CTX, n), lambda b: (N_MAIN // CTX + b, 0))
    return pl.pallas_call(
        _ctx_attn_kernel,
        out_shape=jax.ShapeDtypeStruct((N_CTX, 512), BF16),
        grid=(BATCH,),
        in_specs=[pl.BlockSpec(memory_space=pltpu.SMEM),
                  ctx_spec(512), ctx_spec(128), ctx_spec(128), ctx_spec(128), ctx_spec(128)],
        out_specs=pl.BlockSpec((CTX, 512), lambda b: (b, 0)),
        compiler_params=_cparams(("arbitrary",)),
        name="context_attn",
    )(sink, q, k, ks, v, vs)


NA_GROUP = 8
N_GRID_ROWS = SEQ // GRID_W


def _na_kernel(q_ref, k_ref, v_ref, kc_ref, vc_ref, nb_ref, o_ref):
    g = pl.program_id(1)

    def body(i, carry):
        r = g * NA_GROUP + i
        r0 = jnp.clip(r - NA_ROWS // 2, 0, N_GRID_ROWS - NA_ROWS)
        shift = r0 - r + NA_ROWS - 1
        qrows = pl.ds(pl.multiple_of(i * GRID_W, GRID_W), GRID_W)
        krows = pl.ds(pl.multiple_of(r0 * GRID_W, GRID_W), NA_ROWS * GRID_W)
        n_loc = NA_ROWS * GRID_W
        for c in range(4):
            lanes = slice(c * 128, (c + 1) * 128)
            qc = q_ref[qrows, lanes]
            q2 = jnp.concatenate([jnp.where(_half_mask(qc.shape, hf), qc, jnp.zeros_like(qc))
                                  for hf in range(2)], axis=0)
            kcat = jnp.concatenate([k_ref[krows, lanes], kc_ref[:, lanes]], axis=0)
            vcat = jnp.concatenate([v_ref[krows, lanes], vc_ref[:, lanes]], axis=0)
            s = _nt(q2, kcat)
            bias = jnp.concatenate([nb_ref[2 * c, shift], nb_ref[2 * c + 1, shift]], axis=0)
            s_loc = s[:, :n_loc] + bias
            s_ctx = s[:, n_loc:]
            m = jnp.maximum(jnp.max(s_loc, axis=1, keepdims=True), jnp.max(s_ctx, axis=1, keepdims=True))
            p_loc = jnp.exp(s_loc - m)
            p_ctx = jnp.exp(s_ctx - m)
            den = jnp.sum(p_loc, axis=1, keepdims=True) + jnp.sum(p_ctx, axis=1, keepdims=True)
            p = jnp.concatenate([p_loc, p_ctx], axis=1).astype(BF16)
            o = jnp.dot(p, vcat, preferred_element_type=F32) / den
            o_ref[qrows, lanes] = jnp.where(_half_mask(qc.shape, 0), o[:GRID_W], o[GRID_W:]).astype(BF16)
        return carry

    lax.fori_loop(0, NA_GROUP, body, 0, unroll=4)


def _na_bias(rpb):
    col = np.arange(GRID_W)
    c0 = np.clip(col - NA_COLS // 2, 0, GRID_W - NA_COLS)
    col_ok = (col[None, :] >= c0[:, None]) & (col[None, :] < c0[:, None] + NA_COLS)
    dc = np.clip(col[None, :] - col[:, None] + NA_COLS - 1, 0, 2 * NA_COLS - 2)
    onehot = (dc[None] == np.arange(2 * NA_COLS - 1)[:, None, None]).astype(np.float32)
    e = jnp.einsum('hrd,dqk->hrqk', rpb.astype(F32), jnp.asarray(onehot),
                   precision=lax.Precision.HIGHEST)
    e = jnp.where(col_ok[None, None], e, -jnp.inf)
    b = jnp.stack([e[:, s:s + NA_ROWS] for s in range(NA_ROWS)], axis=1)
    b = jnp.transpose(b, (0, 1, 3, 2, 4))
    return b.reshape(rpb.shape[0], NA_ROWS, GRID_W, NA_ROWS * GRID_W)


def _na_attn(q, k, v, nb):
    qrows = NA_GROUP * GRID_W
    n_g = SEQ // qrows
    seq_spec = pl.BlockSpec((SEQ, 512), lambda b, g: (b, 0))
    ctx_spec = pl.BlockSpec((CTX, 512), lambda b, g: (N_MAIN // CTX + b, 0))
    return pl.pallas_call(
        _na_kernel,
        out_shape=jax.ShapeDtypeStruct((N_MAIN, 512), BF16),
        grid=(BATCH, n_g),
        in_specs=[pl.BlockSpec((qrows, 512), lambda b, g: (b * n_g + g, 0)),
                  seq_spec, seq_spec, ctx_spec, ctx_spec,
                  pl.BlockSpec(nb.shape, lambda b, g: (0, 0, 0, 0))],
        out_specs=pl.BlockSpec((qrows, 512), lambda b, g: (b * n_g + g, 0)),
        compiler_params=_cparams(("arbitrary", "arbitrary")),
        name="neighborhood_attn",
    )(q, k, v, k, v, nb)


F_N2_CHUNK = 8
F_K1_CHUNK = 8


def _four1_kernel(x_ref, w_ref, t_ref):
    w = w_ref[...]
    for j in range(F_N2_CHUNK):
        res = jnp.dot(w, x_ref[:, j * 512:(j + 1) * 512], preferred_element_type=F32)
        t_ref[0, j] = res[:64].astype(BF16)
        t_ref[1, j] = res[64:].astype(BF16)


def _four2_kernel(t_ref, m_ref, cs_ref, y_ref):
    cs = cs_ref[...]
    for j in range(F_K1_CHUNK):
        lanes = slice(j * 512, (j + 1) * 512)
        tt = jnp.concatenate([t_ref[0, :, lanes], t_ref[1, :, lanes]], axis=0)
        pp = jnp.dot(m_ref[j], tt, preferred_element_type=F32)
        pc = jnp.concatenate([pp[:64], pp[64:]], axis=1).astype(BF16)
        y_ref[:, lanes] = jnp.dot(pc, cs, preferred_element_type=F32).astype(BF16)


def _fourier_tables():
    a = np.arange(64)
    ang1 = 2.0 * np.pi * np.outer(a, a) / 64.0
    w1 = np.concatenate([np.cos(ang1), -np.sin(ang1)], axis=0)
    k1 = a[:, None, None]
    k2 = a[None, :, None]
    n2 = a[None, None, :]
    theta = 2.0 * np.pi * (n2 * k2 / 64.0 + n2 * k1 / 4096.0)
    mr = np.cos(theta) / 64.0
    mi = -np.sin(theta) / 64.0
    m = np.concatenate([np.concatenate([mr, -mi], axis=2),
                        np.concatenate([mi, mr], axis=2)], axis=1)
    c = np.arange(128)
    angc = 2.0 * np.pi * np.outer(c, c) / 128.0
    eye4 = np.eye(4)
    cc = np.kron(eye4, np.cos(angc)) / np.sqrt(128.0)
    sc = np.kron(eye4, np.sin(angc)) / np.sqrt(128.0)
    cs = np.concatenate([cc, sc], axis=0)
    return tuple(jnp.asarray(t, F32).astype(BF16) for t in (w1, m, cs))


def _fourier(f):
    w1, m, cs = _fourier_tables()
    fv = f.reshape(N_ALL // 64, 64 * 512)
    n_c = 64 // F_N2_CHUNK
    t = pl.pallas_call(
        _four1_kernel,
        out_shape=jax.ShapeDtypeStruct((BATCH, 2, 64, 64, 512), BF16),
        grid=(BATCH, n_c),
        in_specs=[pl.BlockSpec((64, F_N2_CHUNK * 512), lambda b, c: (b, c)),
                  pl.BlockSpec((128, 64), lambda b, c: (0, 0))],
        out_specs=pl.BlockSpec((None, 2, F_N2_CHUNK, 64, 512), lambda b, c: (b, 0, c, 0, 0)),
        compiler_params=_cparams(("arbitrary", "arbitrary")),
        name="fourier_rows",
    )(fv, w1)
    n_k = 64 // F_K1_CHUNK
    t2 = t.reshape(BATCH, 2, 64, 64 * 512)
    y = pl.pallas_call(
        _four2_kernel,
        out_shape=jax.ShapeDtypeStruct((BATCH * 64, 64 * 512), BF16),
        grid=(BATCH, n_k),
        in_specs=[pl.BlockSpec((None, 2, 64, F_K1_CHUNK * 512), lambda b, c: (b, 0, 0, c)),
                  pl.BlockSpec((F_K1_CHUNK, 128, 128), lambda b, c: (c, 0, 0)),
                  pl.BlockSpec((1024, 512), lambda b, c: (0, 0))],
        out_specs=pl.BlockSpec((64, F_K1_CHUNK * 512), lambda b, c: (b, c)),
        compiler_params=_cparams(("arbitrary", "arbitrary")),
        name="fourier_cols",
    )(t2, m, cs)
    return y.reshape(N_MAIN, 512)


OUT_TILES = 2
TO = OUT_TILES * TM


def _route(h2, rwt_ref, rb_ref, carry):
    logits = lax.dot_general(rwt_ref[...], h2, (((1,), (1,)), ((), ())),
                             preferred_element_type=F32,
                             precision=lax.Precision.HIGHEST) + rb_ref[...]
    eidx = lax.broadcasted_iota(I32, logits.shape, 0)
    vals = logits
    sels, tops, idxs = [], [], []
    for _ in range(TOP_K):
        m = jnp.max(vals, axis=0, keepdims=True)
        idx = jnp.min(jnp.where(vals == m, eidx, N_EXPERTS), axis=0, keepdims=True)
        sel = eidx == idx
        sels.append(sel)
        tops.append(m)
        idxs.append(idx)
        vals = jnp.where(sel, -jnp.inf, vals)
    ex = [jnp.exp(t - tops[0]) for t in tops]
    den = ex[0] + ex[1] + ex[2] + ex[3]
    onehot = jnp.zeros(logits.shape, F32)
    for sel in sels:
        onehot = onehot + sel.astype(F32)
    r_i = lax.broadcasted_iota(I32, (TM, TM), 0)
    c_i = lax.broadcasted_iota(I32, (TM, TM), 1)
    upper = (r_i < c_i).astype(BF16)
    prefix = jnp.dot(onehot.astype(BF16), upper, preferred_element_type=F32)
    base = carry[:, 0:1] + prefix
    tw = [e / den for e in ex]
    rk = [jnp.sum(jnp.where(sel, base, 0.0), axis=0, keepdims=True).astype(I32) for sel in sels]
    return idxs, tw, rk, carry + jnp.sum(onehot, axis=1, keepdims=True)


def _out_tail(i, x, y, mod_ref, g_ref, rwt_ref, rb_ref, carry_ref,
              xn_ref, h2_ref, te_ref, tw_ref, rk_ref, cnt_ref):
    @pl.when(i == 0)
    def _():
        carry_ref[...] = jnp.zeros_like(carry_ref)

    carry = carry_ref[...]
    for t in range(OUT_TILES):
        rows = slice(t * TM, (t + 1) * TM)
        xn = x[rows] + mod_ref[2:3, :] * y[rows]
        xn_ref[rows, :] = xn
        h2 = _rms_mod(xn, g_ref[...], mod_ref[4:5, :], mod_ref[3:4, :])
        h2_ref[rows, :] = h2
        te, tw, rk, carry = _route(h2, rwt_ref, rb_ref, carry)
        for k in range(TOP_K):
            te_ref[k:k + 1, rows] = te[k]
            tw_ref[k:k + 1, rows] = tw[k]
            rk_ref[k:k + 1, rows] = rk[k]
    carry_ref[...] = carry
    cnt_ref[...] = carry


def _out_even_kernel(grp_ref, first_ref, last_ref,
                     x_ref, c_ref, a_ref, gb_ref, z_ref, zp_ref, zn_ref, cw_ref, w_ref, mod_ref, g_ref,
                     rwt_ref, rb_ref,
                     xn_ref, h2_ref, te_ref, tw_ref, rk_ref, cnt_ref, carry_ref):
    i = pl.program_id(0)
    z = z_ref[...]
    rid = lax.broadcasted_iota(I32, z.shape, 0)
    zm1 = jnp.where(rid == 0, zp_ref[7:8, :], pltpu.roll(z, 1, 0))
    zp1 = jnp.where(rid == TO - 1, zn_ref[0:1, :], pltpu.roll(z, TO - 1, 0))
    for t in range(OUT_TILES):
        zm1 = jnp.where(jnp.logical_and(rid == t * TM, first_ref[OUT_TILES * i + t] == 1), 0.0, zm1)
        zp1 = jnp.where(jnp.logical_and(rid == (t + 1) * TM - 1, last_ref[OUT_TILES * i + t] == 1), 0.0, zp1)
    conv = gb_ref[...] * (zm1 * cw_ref[0:1, :] + z * cw_ref[1:2, :] + zp1 * cw_ref[2:3, :])
    y = (jnp.dot(a_ref[...], w_ref[0:512, :], preferred_element_type=F32)
         + jnp.dot(conv.astype(BF16), w_ref[512:1024, :], preferred_element_type=F32))
    x = jnp.where(i < NT_MAIN // OUT_TILES, x_ref[...], c_ref[...])
    _out_tail(i, x, y, mod_ref, g_ref, rwt_ref, rb_ref, carry_ref,
              xn_ref, h2_ref, te_ref, tw_ref, rk_ref, cnt_ref)


def _out_odd_kernel(grp_ref, x_ref, a_ref, f_ref, w_ref, mod_ref, g_ref, rwt_ref, rb_ref,
                    xn_ref, h2_ref, te_ref, tw_ref, rk_ref, cnt_ref, carry_ref):
    i = pl.program_id(0)
    y = (jnp.dot(a_ref[...], w_ref[0:512, :], preferred_element_type=F32)
         + jnp.dot(f_ref[...], w_ref[512:1024, :], preferred_element_type=F32))
    _out_tail(i, x_ref[...], y, mod_ref, g_ref, rwt_ref, rb_ref, carry_ref,
              xn_ref, h2_ref, te_ref, tw_ref, rk_ref, cnt_ref)


def _out_shapes(n_rows):
    return (jax.ShapeDtypeStruct((n_rows, D), F32), jax.ShapeDtypeStruct((n_rows, D), F32),
            jax.ShapeDtypeStruct((TOP_K, n_rows), I32), jax.ShapeDtypeStruct((TOP_K, n_rows), F32),
            jax.ShapeDtypeStruct((TOP_K, n_rows), I32), jax.ShapeDtypeStruct((N_EXPERTS, 128), F32))


def _out_even(x2d, c2d, attn, gb, z, conv_w, w_bf, mod, g, rwt, rb):
    grp, _, first, last = _tile_tables()
    n_rows = N_ALL
    zblocks = n_rows // 8
    im = lambda f: (lambda i, grp, fi, la: f(i))
    tile = lambda n: pl.BlockSpec((TO, n), im(lambda i: (i, 0)))
    const = lambda shape: pl.BlockSpec(shape, im(lambda i: (0,) * len(shape)))
    tk = pl.BlockSpec((TOP_K, TO), im(lambda i: (0, i)))
    return pl.pallas_call(
        _out_even_kernel,
        out_shape=_out_shapes(n_rows),
        grid_spec=pltpu.PrefetchScalarGridSpec(
            num_scalar_prefetch=3,
            grid=(NT_ALL // OUT_TILES,),
            in_specs=[pl.BlockSpec((TO, D), im(lambda i: (jnp.minimum(i, NT_MAIN // OUT_TILES - 1), 0))),
                      pl.BlockSpec((TO, D), im(lambda i: (jnp.maximum(i - NT_MAIN // OUT_TILES, 0), 0))),
                      tile(512), tile(512), tile(512),
                      pl.BlockSpec((8, 512), im(lambda i: (jnp.maximum(i * (TO // 8) - 1, 0), 0))),
                      pl.BlockSpec((8, 512), im(lambda i: (jnp.minimum((i + 1) * (TO // 8), zblocks - 1), 0))),
                      const((3, 512)), const((D, D)),
                      pl.BlockSpec((None, 6, D), lambda i, grp, fi, la: (grp[OUT_TILES * i], 0, 0)),
                      const((1, D)), const((N_EXPERTS, D)), const((N_EXPERTS, 1))],
            out_specs=(tile(D), tile(D), tk, tk, tk, const((N_EXPERTS, 128))),
            scratch_shapes=[pltpu.VMEM((N_EXPERTS, 128), F32)],
        ),
        compiler_params=_cparams(("arbitrary",)),
        name="out_proj_even",
    )(jnp.asarray(grp), jnp.asarray(first), jnp.asarray(last),
      x2d, c2d, attn, gb, z, z, z, conv_w, w_bf, mod, g, rwt, rb)


def _out_odd(xall, attn, four, w_bf, mod, g, rwt, rb):
    grp, _, _, _ = _tile_tables()
    n_rows = N_MAIN
    im = lambda f: (lambda i, grp: f(i))
    tile = lambda n: pl.BlockSpec((TO, n), im(lambda i: (i, 0)))
    const = lambda shape: pl.BlockSpec(shape, im(lambda i: (0,) * len(shape)))
    tk = pl.BlockSpec((TOP_K, TO), im(lambda i: (0, i)))
    return pl.pallas_call(
        _out_odd_kernel,
        out_shape=_out_shapes(n_rows),
        grid_spec=pltpu.PrefetchScalarGridSpec(
            num_scalar_prefetch=1,
            grid=(NT_MAIN // OUT_TILES,),
            in_specs=[tile(D), tile(512), tile(512), const((D, D)),
                      pl.BlockSpec((None, 6, D), lambda i, grp: (grp[OUT_TILES * i], 0, 0)),
                      const((1, D)), const((N_EXPERTS, D)), const((N_EXPERTS, 1))],
            out_specs=(tile(D), tile(D), tk, tk, tk, const((N_EXPERTS, 128))),
            scratch_shapes=[pltpu.VMEM((N_EXPERTS, 128), F32)],
        ),
        compiler_params=_cparams(("arbitrary",)),
        name="out_proj_odd",
    )(jnp.asarray(grp), xall, attn, four, w_bf, mod, g, rwt, rb)


def _moe_plan(counts_f, top_e_t, rank_t, n_tok):
    counts = counts_f[:, 0].astype(I32)
    padded = (counts + TMM - 1) // TMM * TMM
    e_i = jnp.arange(N_EXPERTS, dtype=I32)
    incl = e_i[None, :] <= e_i[:, None]
    pad_end = jnp.sum(jnp.where(incl, padded[None, :], 0), axis=1)
    pad_start = pad_end - padded
    sel = top_e_t[None] == e_i[:, None, None]
    dest = jnp.sum(jnp.where(sel, pad_start[:, None, None], 0), axis=0) + rank_t
    n_blocks = n_tok * TOP_K // TMM + N_EXPERTS
    blk_start = jnp.arange(n_blocks, dtype=I32) * TMM
    block_e = jnp.minimum(jnp.sum((blk_start[:, None] >= pad_end[None, :]).astype(I32), axis=1),
                          N_EXPERTS - 1)
    n_used = (pad_end[-1] // TMM).reshape(1)
    n_tiles = n_tok // TT
    dest_flat = dest.reshape(TOP_K, n_tiles, TT).transpose(1, 0, 2).reshape(-1)
    return dest_flat, block_e, n_used, pad_end, padded


def _dispatch_kernel(pe_ref, pd_ref, nu_ref, dst_ref, h2_ref, xs_hbm, zbuf, sem, sem_z):
    i = pl.program_id(0)
    n_blocks = xs_hbm.shape[0] // TMM

    @pl.when(i == 0)
    def _():
        zbuf[...] = jnp.zeros_like(zbuf)

        def fill(b):
            return pltpu.make_async_copy(zbuf, xs_hbm.at[pl.ds(pl.multiple_of(b * TMM, TMM), TMM)], sem_z)

        def fill_expert(e, n):
            has_rows = pd_ref[e] > 0

            @pl.when(has_rows)
            def _():
                fill(pe_ref[e] // TMM - 1).start()

            return n + has_rows.astype(I32)

        def fill_tail(b, c):
            fill(b).start()
            return c

        def fill_wait(j, c):
            fill(0).wait()
            return c

        n_fill = lax.fori_loop(0, N_EXPERTS, fill_expert, 0) + n_blocks - nu_ref[0]
        lax.fori_loop(nu_ref[0], n_blocks, fill_tail, 0)
        lax.fori_loop(0, n_fill, fill_wait, 0)

    base = i * (TOP_K * TT)

    def issue(j, c):
        for k in range(TOP_K):
            pltpu.make_async_copy(h2_ref.at[pl.ds(j, 1)], xs_hbm.at[pl.ds(dst_ref[base + k * TT + j], 1)],
                                  sem).start(priority=k % 2)
        return c

    lax.fori_loop(0, TT, issue, 0, unroll=4)
    for _ in range(TOP_K):
        pltpu.make_async_copy(h2_ref, xs_hbm.at[pl.ds(0, TT)], sem).wait()


def _dispatch(h2, dest_flat, pad_end, padded, n_used, cap):
    n_tiles = h2.shape[0] // TT
    return pl.pallas_call(
        _dispatch_kernel,
        out_shape=jax.ShapeDtypeStruct((cap, D), F32),
        grid_spec=pltpu.PrefetchScalarGridSpec(
            num_scalar_prefetch=4,
            grid=(n_tiles,),
            in_specs=[pl.BlockSpec((TT, D), lambda i, pe, pd, nu, dst: (i, 0))],
            out_specs=pl.BlockSpec(memory_space=pl.ANY),
            scratch_shapes=[pltpu.VMEM((TMM, D), F32), pltpu.SemaphoreType.DMA, pltpu.SemaphoreType.DMA],
        ),
        compiler_params=_cparams(("arbitrary",)),
        name="moe_dispatch",
    )(pad_end, padded, n_used, dest_flat, h2)


def _moe_kernel(be_ref, nu_ref, x_ref, wgu_ref, bgu_ref, wdn_ref, bdn_ref, y_ref, wgu_bf, wdn_bf):
    i = pl.program_id(0)
    prev = be_ref[jnp.maximum(i - 1, 0)]

    @pl.when((i < nu_ref[0]) & ((i == 0) | (be_ref[i] != prev)))
    def _():
        wgu_bf[...] = wgu_ref[...].astype(BF16)
        wdn_bf[...] = wdn_ref[...].astype(BF16)

    @pl.when(i < nu_ref[0])
    def _():
        xb = x_ref[...].astype(BF16)
        gu = jnp.dot(xb, wgu_bf[...], preferred_element_type=F32) + bgu_ref[...]
        gate = jnp.minimum(gu[:, :D], SWIGLU_LIMIT)
        up = jnp.clip(gu[:, D:], -SWIGLU_LIMIT, SWIGLU_LIMIT)
        act = (up + 1.0) * (gate * (1.0 / (1.0 + jnp.exp(-SWIGLU_ALPHA * gate))))
        y_ref[...] = jnp.dot(act.astype(BF16), wdn_bf[...], preferred_element_type=F32) + bdn_ref[...]

    @pl.when(i >= nu_ref[0])
    def _():
        y_ref[...] = jnp.zeros_like(y_ref)


def _moe(layer, xs, block_e, n_used, w_gu, b_gu, w_dn, b_dn):
    n_blocks = block_e.shape[0]
    n_l = w_gu.shape[0]
    blk = lambda i, be, nu: (jnp.minimum(i, nu[0] - 1), 0)
    out_blk = lambda i, be, nu: (i, 0)
    exp4 = lambda i, be, nu: (layer, be[jnp.minimum(i, nu[0] - 1)], 0, 0)
    return pl.pallas_call(
        _moe_kernel,
        out_shape=jax.ShapeDtypeStruct((n_blocks * TMM, D), F32),
        grid_spec=pltpu.PrefetchScalarGridSpec(
            num_scalar_prefetch=2,
            grid=(n_blocks,),
            in_specs=[pl.BlockSpec((TMM, D), blk),
                      pl.BlockSpec((None, None, D, 2 * D), exp4),
                      pl.BlockSpec((None, None, 1, 2 * D), exp4),
                      pl.BlockSpec((None, None, D, D), exp4),
                      pl.BlockSpec((None, None, 1, D), exp4)],
            out_specs=pl.BlockSpec((TMM, D), out_blk),
            scratch_shapes=[pltpu.VMEM((D, 2 * D), BF16), pltpu.VMEM((D, D), BF16)],
        ),
        compiler_params=_cparams(("arbitrary",)),
        name="moe_experts",
    )(block_e, n_used, xs, w_gu, b_gu.reshape(n_l, N_EXPERTS, 1, 2 * D), w_dn,
      b_dn.reshape(n_l, N_EXPERTS, 1, D))


def _combine_kernel(final, grp_ref, dst_ref, ys_hbm, x_ref, tw_ref, mod_ref, fn_ref, o_ref, buf, sem):
    i = pl.program_id(0)
    n_tiles = pl.num_programs(0)

    def gather(tile, slot):
        base = tile * (TOP_K * TT)

        def issue(j, c):
            for k in range(TOP_K):
                pltpu.make_async_copy(ys_hbm.at[pl.ds(dst_ref[base + k * TT + j], 1)],
                                      buf.at[slot, k, pl.ds(j, 1)], sem.at[slot]).start(priority=k % 2)
            return c

        lax.fori_loop(0, TT, issue, 0, unroll=4)

    @pl.when(i == 0)
    def _():
        gather(0, 0)

    @pl.when(i + 1 < n_tiles)
    def _():
        gather(i + 1, (i + 1) % 2)

    slot = i % 2
    for k in range(TOP_K):
        pltpu.make_async_copy(ys_hbm.at[pl.ds(0, TT)], buf.at[slot, k], sem.at[slot]).wait()
    tw = tw_ref[...]
    acc = tw[:, 0:1] * buf[slot, 0]
    for k in range(1, TOP_K):
        acc = acc + tw[:, k:k + 1] * buf[slot, k]
    out = x_ref[...] + mod_ref[5:6, :] * acc
    if final:
        ms = jnp.mean(out * out, axis=-1, keepdims=True)
        out = out * lax.rsqrt(ms + EPS) * fn_ref[...]
    o_ref[...] = out


def _combine(xn, ys, dest_flat, top_w_t, mod, final_norm, n_tok, final):
    grp, _, _, _ = _tile_tables()
    n_tiles = n_tok // TT
    tw = top_w_t.T
    return pl.pallas_call(
        functools.partial(_combine_kernel, final),
        out_shape=jax.ShapeDtypeStruct((n_tok, D), F32),
        grid_spec=pltpu.PrefetchScalarGridSpec(
            num_scalar_prefetch=2,
            grid=(n_tiles,),
            in_specs=[pl.BlockSpec(memory_space=pl.ANY),
                      pl.BlockSpec((TT, D), lambda i, grp, dst: (i, 0)),
                      pl.BlockSpec((TT, TOP_K), lambda i, grp, dst: (i, 0)),
                      pl.BlockSpec((None, 6, D), lambda i, grp, dst: (grp[i * (TT // TM)], 0, 0)),
                      pl.BlockSpec((1, D), lambda i, grp, dst: (0, 0))],
            out_specs=pl.BlockSpec((TT, D), lambda i, grp, dst: (i, 0)),
            scratch_shapes=[pltpu.VMEM((2, TOP_K, TT, D), F32), pltpu.SemaphoreType.DMA((2,))],
        ),
        compiler_params=_cparams(("arbitrary",)),
        name="moe_combine",
    )(jnp.asarray(grp), dest_flat, ys, xn, tw, mod, final_norm.reshape(1, D))


def _moe_layer(layer, xn, h2, top_e_t, top_w_t, rank_t, counts, mod, w_gu, b_gu, w_dn, b_dn, final_norm, final):
    n_tok = xn.shape[0]
    dest_flat, block_e, n_used, pad_end, padded = _moe_plan(counts, top_e_t, rank_t, n_tok)
    xs = _dispatch(h2, dest_flat, pad_end, padded, n_used, block_e.shape[0] * TMM)
    ys = _moe(layer, xs, block_e, n_used, w_gu, b_gu, w_dn, b_dn)
    return _combine(xn, ys, dest_flat, top_w_t, mod, final_norm, n_tok, final)


def kernel(x, c, ctx, c_ctx, ada_w, ada_b, norm_mix, norm_ffn, even_w_in, even_w_out, even_conv_w, even_sink, odd_w_in, odd_w_out, odd_rpb, router_w, router_b, moe_w_gu, moe_b_gu, moe_w_dn, moe_b_dn, final_norm):
    x2d = x.reshape(N_MAIN, D)
    c2d = ctx.reshape(N_CTX, D)
    cc = jnp.concatenate([c, c_ctx[None, :], jnp.zeros((3, D), F32)], axis=0)
    mod = _ada(cc, ada_w, ada_b).reshape(2, 8, 6, D)
    cos_f, sin_f = _rope_tables()

    q, k, ks, v, vs, gb, z = _in_even(x2d, c2d, mod[0], norm_mix[0:1], even_w_in[0].astype(BF16), cos_f, sin_f)
    attn = jnp.concatenate([_win_attn(even_sink[0], q, k, ks, v, vs),
                            _ctx_attn(even_sink[0], q, k, ks, v, vs)], axis=0)
    xn, h2, te, tw, rk, cnt = _out_even(x2d, c2d, attn, gb, z, even_conv_w[0], even_w_out[0].astype(BF16),
                                        mod[0], norm_ffn[0:1], router_w[0].T, router_b[0][:, None])
    xall = _moe_layer(0, xn, h2, te, tw, rk, cnt, mod[0], moe_w_gu, moe_b_gu, moe_w_dn, moe_b_dn,
                      final_norm, False)

    q, k, v, f = _in_odd(xall, mod[1], norm_mix[1:2], odd_w_in[0].astype(BF16))
    attn = _na_attn(q, k, v, _na_bias(odd_rpb[0]))
    four = _fourier(f)
    xn, h2, te, tw, rk, cnt = _out_odd(xall, attn, four, odd_w_out[0].astype(BF16),
                                       mod[1], norm_ffn[1:2], router_w[1].T, router_b[1][:, None])
    out = _moe_layer(1, xn, h2, te, tw, rk, cnt, mod[1], moe_w_gu, moe_b_gu, moe_w_dn, moe_b_dn,
                     final_norm, True)
    return out.reshape(BATCH, SEQ, D)
```

```python
import functools

import numpy as np
import jax
import jax.numpy as jnp
from jax import lax
from jax.experimental import pallas as pl
from jax.experimental.pallas import tpu as pltpu

F32 = jnp.float32
BF16 = jnp.bfloat16
I32 = jnp.int32

D = 1024
BATCH = 4
SEQ = 4096
CTX = 256
GRID_W = 64
HEAD_DIM = 64
EPS = 1e-6
ROPE_THETA = 10000.0
N_EXPERTS = 32
TOP_K = 4
SWIGLU_LIMIT = 7.0
SWIGLU_ALPHA = 1.702
NA_ROWS = 8
NA_COLS = 16

N_MAIN = BATCH * SEQ
N_CTX = BATCH * CTX
N_ALL = N_MAIN + N_CTX
TM = 256
NT_MAIN = N_MAIN // TM
NT_ALL = N_ALL // TM
TILES_PER_SEQ = SEQ // TM
TMM = 512
TT = 512
VMEM_LIMIT = 56 * 1024 * 1024


def _cparams(sem, vmem=VMEM_LIMIT):
    return pltpu.CompilerParams(dimension_semantics=sem, vmem_limit_bytes=vmem)


def _rms_mod(x, g, sc, sh):
    ms = jnp.mean(x * x, axis=-1, keepdims=True)
    return (x * lax.rsqrt(ms + EPS) * g) * (1.0 + sc) + sh


def _ada_kernel(c_ref, w_ref, b_ref, o_ref):
    c = c_ref[...]
    s = c * (1.0 / (1.0 + jnp.exp(-c)))
    o_ref[...] = jnp.dot(s, w_ref[...], preferred_element_type=F32,
                         precision=lax.Precision.HIGHEST) + b_ref[...]


def _ada(cc, ada_w, ada_b):
    n_l = ada_w.shape[0]
    tn = 1024
    return pl.pallas_call(
        _ada_kernel,
        out_shape=jax.ShapeDtypeStruct((n_l, 8, 6 * D), F32),
        grid=(n_l, 6 * D // tn),
        in_specs=[pl.BlockSpec((8, D), lambda l, j: (0, 0)),
                  pl.BlockSpec((None, D, tn), lambda l, j: (l, 0, j)),
                  pl.BlockSpec((None, 1, tn), lambda l, j: (l, 0, j))],
        out_specs=pl.BlockSpec((None, 8, tn), lambda l, j: (l, 0, j)),
        compiler_params=_cparams(("arbitrary", "arbitrary")),
        name="ada_mod",
    )(cc, ada_w, ada_b.reshape(n_l, 1, 6 * D))


def _rope_apply(t, cos, sin):
    n = t.shape[1]
    lane = lax.broadcasted_iota(I32, t.shape, 1)
    fwd = pltpu.roll(t, n - 32, 1)
    bwd = pltpu.roll(t, 32, 1)
    rot = jnp.where((lane % 64) < 32, fwd, bwd)
    reps = n // 128
    cosf = jnp.concatenate([cos] * reps, axis=1) if reps > 1 else cos
    sinf = jnp.concatenate([sin] * reps, axis=1) if reps > 1 else sin
    return t * cosf + rot * sinf


def _in_even_kernel(grp_ref, rblk_ref, x_ref, c_ref, mod_ref, g_ref, w_ref, cos_ref, sin_ref,
                    q_ref, k_ref, ks_ref, v_ref, vs_ref, gb_ref, z_ref):
    xt = jnp.where(pl.program_id(0) < NTI_MAIN, x_ref[...], c_ref[...])
    h = _rms_mod(xt, g_ref[...], mod_ref[1:2, :], mod_ref[0:1, :])
    p = jnp.dot(h.astype(BF16), w_ref[...], preferred_element_type=F32)
    cos = cos_ref[...]
    sin = sin_ref[...]
    q = _rope_apply(p[:, 0:512], cos, sin) * (HEAD_DIM ** -0.5)
    k = _rope_apply(p[:, 512:640], cos, sin)
    v = p[:, 640:768]
    q_ref[...] = q.astype(BF16)
    k_ref[...] = k.astype(BF16)
    ks_ref[...] = pltpu.roll(k, 64, 1).astype(BF16)
    v_ref[...] = v.astype(BF16)
    vs_ref[...] = pltpu.roll(v, 64, 1).astype(BF16)
    gb_ref[...] = p[:, 768:1280]
    z_ref[...] = p[:, 1280:1792] * p[:, 1792:2304]


def _in_odd_kernel(grp_ref, x_ref, mod_ref, g_ref, w_ref, q_ref, k_ref, v_ref, f_ref):
    h = _rms_mod(x_ref[...], g_ref[...], mod_ref[1:2, :], mod_ref[0:1, :])
    p = jnp.dot(h.astype(BF16), w_ref[...], preferred_element_type=F32)
    q_ref[...] = (p[:, 0:512] * (HEAD_DIM ** -0.5)).astype(BF16)
    k_ref[...] = p[:, 512:1024].astype(BF16)
    v_ref[...] = p[:, 1024:1536].astype(BF16)
    f_ref[...] = p[:, 1536:2048].astype(BF16)


def _tile_tables():
    t = np.arange(NT_ALL)
    main = t < NT_MAIN
    grp = np.where(main, t // TILES_PER_SEQ, BATCH).astype(np.int32)
    rblk = np.where(main, t % TILES_PER_SEQ, TILES_PER_SEQ).astype(np.int32)
    first = np.where(main, (t % TILES_PER_SEQ) == 0, True).astype(np.int32)
    last = np.where(main, (t % TILES_PER_SEQ) == TILES_PER_SEQ - 1, True).astype(np.int32)
    return grp, rblk, first, last


TI = 512
NTI_MAIN = N_MAIN // TI
NTI_ALL = N_ALL // TI


def _in_tile_tables():
    t = np.arange(NTI_ALL)
    main = t < NTI_MAIN
    grp = np.where(main, t // (SEQ // TI), BATCH).astype(np.int32)
    rblk = np.where(main, t % (SEQ // TI), SEQ // TI).astype(np.int32)
    return grp, rblk


def _rope_tables():
    t = jnp.arange(SEQ, dtype=I32)
    row = (t // GRID_W).astype(F32)
    col = (t % GRID_W).astype(F32)
    n_freq = HEAD_DIM // 4
    inv_freq = jnp.power(ROPE_THETA, -jnp.arange(n_freq, dtype=F32) / n_freq)
    ang = jnp.concatenate([row[:, None] * inv_freq, col[:, None] * inv_freq], axis=-1)
    cos = jnp.cos(ang)
    sin = jnp.sin(ang)
    cos_f = jnp.concatenate([cos, cos, cos, cos], axis=1)
    sin_f = jnp.concatenate([-sin, sin, -sin, sin], axis=1)
    cos_f = jnp.concatenate([cos_f, jnp.ones((TI, 128), F32)], axis=0)
    sin_f = jnp.concatenate([sin_f, jnp.zeros((TI, 128), F32)], axis=0)
    return cos_f, sin_f


def _in_even(x2d, c2d, mod, g, w_bf, cos_f, sin_f):
    grp, rblk = _in_tile_tables()
    row = lambda n, dt: jax.ShapeDtypeStruct((N_ALL, n), dt)
    tile = lambda n: pl.BlockSpec((TI, n), lambda i, grp, rb: (i, 0))
    x_spec = pl.BlockSpec((TI, D), lambda i, grp, rb: (jnp.minimum(i, NTI_MAIN - 1), 0))
    c_spec = pl.BlockSpec((TI, D), lambda i, grp, rb: (jnp.maximum(i - NTI_MAIN, 0), 0))
    return pl.pallas_call(
        _in_even_kernel,
        out_shape=(row(512, BF16), row(128, BF16), row(128, BF16), row(128, BF16), row(128, BF16),
                   row(512, F32), row(512, F32)),
        grid_spec=pltpu.PrefetchScalarGridSpec(
            num_scalar_prefetch=2,
            grid=(NTI_ALL,),
            in_specs=[x_spec, c_spec,
                      pl.BlockSpec((None, 6, D), lambda i, grp, rb: (grp[i], 0, 0)),
                      pl.BlockSpec((1, D), lambda i, grp, rb: (0, 0)),
                      pl.BlockSpec((D, 2304), lambda i, grp, rb: (0, 0)),
                      pl.BlockSpec((TI, 128), lambda i, grp, rb: (rb[i], 0)),
                      pl.BlockSpec((TI, 128), lambda i, grp, rb: (rb[i], 0))],
            out_specs=(tile(512), tile(128), tile(128), tile(128), tile(128), tile(512), tile(512)),
        ),
        compiler_params=_cparams(("arbitrary",)),
        name="in_proj_even",
    )(jnp.asarray(grp), jnp.asarray(rblk), x2d, c2d, mod, g, w_bf, cos_f, sin_f)


def _in_odd(xall, mod, g, w_bf):
    grp, _ = _in_tile_tables()
    row = lambda n, dt: jax.ShapeDtypeStruct((N_ALL, n), dt)
    tile = lambda n: pl.BlockSpec((TI, n), lambda i, grp: (i, 0))
    return pl.pallas_call(
        _in_odd_kernel,
        out_shape=(row(512, BF16), row(512, BF16), row(512, BF16), row(512, BF16)),
        grid_spec=pltpu.PrefetchScalarGridSpec(
            num_scalar_prefetch=1,
            grid=(NTI_ALL,),
            in_specs=[tile(D),
                      pl.BlockSpec((None, 6, D), lambda i, grp: (grp[i], 0, 0)),
                      pl.BlockSpec((1, D), lambda i, grp: (0, 0)),
                      pl.BlockSpec((D, 2048), lambda i, grp: (0, 0))],
            out_specs=(tile(512), tile(512), tile(512), tile(512)),
        ),
        compiler_params=_cparams(("arbitrary",)),
        name="in_proj_odd",
    )(jnp.asarray(grp), xall, mod, g, w_bf)


def _nt(a, b):
    return lax.dot_general(a, b, (((1,), (1,)), ((), ())), preferred_element_type=F32)


def _half_mask(shape, half):
    lane = lax.broadcasted_iota(I32, shape, 1)
    return (lane < 64) if half == 0 else (lane >= 64)


def _win_kernel(sink_ref, q_ref, k_ref, ks_ref, v_ref, vs_ref, kc_ref, ksc_ref, vc_ref, vsc_ref, o_ref):
    n = pl.program_id(1)
    start = pl.multiple_of(jnp.clip((n - 1) * 128, 0, SEQ - 384), 128)
    win = pl.ds(start, 384)
    row = lax.broadcasted_iota(I32, (256, 384), 0)
    col = lax.broadcasted_iota(I32, (256, 384), 1)
    valid = jnp.abs((n * 128 + row % 128) - (start + col)) <= 128
    first = lax.broadcasted_iota(I32, (256, 1), 0) < 128
    kk = (jnp.concatenate([k_ref[win, :], kc_ref[...]], axis=0),
          jnp.concatenate([ks_ref[win, :], ksc_ref[...]], axis=0))
    vv = (jnp.concatenate([v_ref[win, :], vc_ref[...]], axis=0),
          jnp.concatenate([vs_ref[win, :], vsc_ref[...]], axis=0))
    outs = {}
    for hk in range(2):
        for hf in range(2):
            swapped = 0 if hk == hf else 1
            chunks = (2 * hk, 2 * hk + 1)
            qs = [q_ref[:, c * 128:(c + 1) * 128] for c in chunks]
            q2 = jnp.concatenate([jnp.where(_half_mask(t.shape, hf), t, jnp.zeros_like(t)) for t in qs],
                                 axis=0)
            s = _nt(q2, kk[swapped])
            s_loc = jnp.where(valid, s[:, :384], -jnp.inf)
            s_ctx = s[:, 384:]
            sink = jnp.where(first, sink_ref[2 * chunks[0] + hf], sink_ref[2 * chunks[1] + hf])
            m = jnp.maximum(jnp.maximum(jnp.max(s_loc, axis=1, keepdims=True),
                                        jnp.max(s_ctx, axis=1, keepdims=True)), sink)
            p_loc = jnp.exp(s_loc - m)
            p_ctx = jnp.exp(s_ctx - m)
            den = (jnp.sum(p_loc, axis=1, keepdims=True) + jnp.sum(p_ctx, axis=1, keepdims=True)
                   + jnp.exp(sink - m))
            p = jnp.concatenate([p_loc, p_ctx], axis=1).astype(BF16)
            o = jnp.dot(p, vv[swapped], preferred_element_type=F32) / den
            outs[(chunks[0], hf)] = o[:128]
            outs[(chunks[1], hf)] = o[128:]
    for c in range(4):
        o_ref[:, c * 128:(c + 1) * 128] = jnp.where(_half_mask((128, 128), 0),
                                                    outs[(c, 0)], outs[(c, 1)]).astype(BF16)


def _win_attn(sink, q, k, ks, v, vs):
    nb = SEQ // 128
    seq_spec = pl.BlockSpec((SEQ, 128), lambda b, n: (b, 0))
    ctx_spec = pl.BlockSpec((CTX, 128), lambda b, n: (N_MAIN // CTX + b, 0))
    return pl.pallas_call(
        _win_kernel,
        out_shape=jax.ShapeDtypeStruct((N_MAIN, 512), BF16),
        grid=(BATCH, nb),
        in_specs=[pl.BlockSpec(memory_space=pltpu.SMEM),
                  pl.BlockSpec((128, 512), lambda b, n: (b * (SEQ // 128) + n, 0)),
                  seq_spec, seq_spec, seq_spec, seq_spec,
                  ctx_spec, ctx_spec, ctx_spec, ctx_spec],
        out_specs=pl.BlockSpec((128, 512), lambda b, n: (b * (SEQ // 128) + n, 0)),
        compiler_params=_cparams(("arbitrary", "arbitrary")),
        name="window_attn",
    )(sink, q, k, ks, v, vs, k, ks, v, vs)


def _ctx_attn_kernel(sink_ref, q_ref, k_ref, ks_ref, v_ref, vs_ref, o_ref):
    kk = (k_ref[...], ks_ref[...])
    vv = (v_ref[...], vs_ref[...])
    for c in range(4):
        qc = q_ref[:, c * 128:(c + 1) * 128]
        halves = []
        for hf in range(2):
            h = 2 * c + hf
            swapped = 0 if (h // 4) == hf else 1
            qm = jnp.where(_half_mask(qc.shape, hf), qc, jnp.zeros_like(qc))
            s = _nt(qm, kk[swapped])
            sink = sink_ref[h]
            m = jnp.maximum(jnp.max(s, axis=1, keepdims=True), sink)
            p = jnp.exp(s - m)
            den = jnp.sum(p, axis=1, keepdims=True) + jnp.exp(sink - m)
            halves.append(jnp.dot(p.astype(BF16), vv[swapped], preferred_element_type=F32) / den)
        o_ref[:, c * 128:(c + 1) * 128] = jnp.where(_half_mask(halves[0].shape, 0),
                                                    halves[0], halves[1]).astype(BF16)


def _ctx_attn(sink, q, k, ks, v, vs):
    ctx_spec = lambda n: pl.BlockSpec((CTX, n), lambda b: (N_MAIN // CTX + b, 0))
    return pl.pallas_call(
        _ctx_attn_kernel,
        out_shape=jax.ShapeDtypeStruct((N_CTX, 512), BF16),
        grid=(BATCH,),
        in_specs=[pl.BlockSpec(memory_space=pltpu.SMEM),
                  ctx_spec(512), ctx_spec(128), ctx_spec(128), ctx_spec(128), ctx_spec(128)],
        out_specs=pl.BlockSpec((CTX, 512), lambda b: (b, 0)),
        compiler_params=_cparams(("arbitrary",)),
        name="context_attn",
    )(sink, q, k, ks, v, vs)


NA_GROUP = 16
N_GRID_ROWS = SEQ // GRID_W


def _na_kernel(q_ref, k_ref, v_ref, kc_ref, vc_ref, nb_ref, o_ref):
    g = pl.program_id(1)

    def body(i, carry):
        r = g * NA_GROUP + i
        r0 = jnp.clip(r - NA_ROWS // 2, 0, N_GRID_ROWS - NA_ROWS)
        shift = r0 - r + NA_ROWS - 1
        qrows = pl.ds(pl.multiple_of(i * GRID_W, GRID_W), GRID_W)
        krows = pl.ds(pl.multiple_of(r0 * GRID_W, GRID_W), NA_ROWS * GRID_W)
        n_loc = NA_ROWS * GRID_W
        for c in range(4):
            lanes = slice(c * 128, (c + 1) * 128)
            qc = q_ref[qrows, lanes]
            q2 = jnp.concatenate([jnp.where(_half_mask(qc.shape, hf), qc, jnp.zeros_like(qc))
                                  for hf in range(2)], axis=0)
            kcat = jnp.concatenate([k_ref[krows, lanes], kc_ref[:, lanes]], axis=0)
            vcat = jnp.concatenate([v_ref[krows, lanes], vc_ref[:, lanes]], axis=0)
            s = _nt(q2, kcat)
            bias = jnp.concatenate([nb_ref[2 * c, shift], nb_ref[2 * c + 1, shift]], axis=0)
            s_loc = s[:, :n_loc] + bias
            s_ctx = s[:, n_loc:]
            m = jnp.maximum(jnp.max(s_loc, axis=1, keepdims=True), jnp.max(s_ctx, axis=1, keepdims=True))
            p_loc = jnp.exp(s_loc - m)
            p_ctx = jnp.exp(s_ctx - m)
            den = jnp.sum(p_loc, axis=1, keepdims=True) + jnp.sum(p_ctx, axis=1, keepdims=True)
            p = jnp.concatenate([p_loc, p_ctx], axis=1).astype(BF16)
            o = jnp.dot(p, vcat, preferred_element_type=F32) / den
            o_ref[qrows, lanes] = jnp.where(_half_mask(qc.shape, 0), o[:GRID_W], o[GRID_W:]).astype(BF16)
        return carry

    lax.fori_loop(0, NA_GROUP, body, 0, unroll=4)


def _na_bias(rpb):
    col = np.arange(GRID_W)
    c0 = np.clip(col - NA_COLS // 2, 0, GRID_W - NA_COLS)
    col_ok = (col[None, :] >= c0[:, None]) & (col[None, :] < c0[:, None] + NA_COLS)
    dc = np.clip(col[None, :] - col[:, None] + NA_COLS - 1, 0, 2 * NA_COLS - 2)
    onehot = (dc[None] == np.arange(2 * NA_COLS - 1)[:, None, None]).astype(np.float32)
    e = jnp.einsum('hrd,dqk->hrqk', rpb.astype(F32), jnp.asarray(onehot),
                   precision=lax.Precision.HIGHEST)
    e = jnp.where(col_ok[None, None], e, -jnp.inf)
    b = jnp.stack([e[:, s:s + NA_ROWS] for s in range(NA_ROWS)], axis=1)
    b = jnp.transpose(b, (0, 1, 3, 2, 4))
    return b.reshape(rpb.shape[0], NA_ROWS, GRID_W, NA_ROWS * GRID_W)


def _na_attn(q, k, v, nb):
    qrows = NA_GROUP * GRID_W
    n_g = SEQ // qrows
    seq_spec = pl.BlockSpec((SEQ, 512), lambda b, g: (b, 0))
    ctx_spec = pl.BlockSpec((CTX, 512), lambda b, g: (N_MAIN // CTX + b, 0))
    return pl.pallas_call(
        _na_kernel,
        out_shape=jax.ShapeDtypeStruct((N_MAIN, 512), BF16),
        grid=(BATCH, n_g),
        in_specs=[pl.BlockSpec((qrows, 512), lambda b, g: (b * n_g + g, 0)),
                  seq_spec, seq_spec, ctx_spec, ctx_spec,
                  pl.BlockSpec(nb.shape, lambda b, g: (0, 0, 0, 0))],
        out_specs=pl.BlockSpec((qrows, 512), lambda b, g: (b * n_g + g, 0)),
        compiler_params=_cparams(("arbitrary", "arbitrary")),
        name="neighborhood_attn",
    )(q, k, v, k, v, nb)


F_N2_CHUNK = 8
F_K1_CHUNK = 8


def _four1_kernel(x_ref, w_ref, t_ref):
    w = w_ref[...]
    for j in range(F_N2_CHUNK):
        res = jnp.dot(w, x_ref[:, j * 512:(j + 1) * 512], preferred_element_type=F32)
        t_ref[0, j] = res[:64].astype(BF16)
        t_ref[1, j] = res[64:].astype(BF16)


def _four2_kernel(t_ref, m_ref, cs_ref, y_ref):
    cs = cs_ref[...]
    for j in range(F_K1_CHUNK):
        lanes = slice(j * 512, (j + 1) * 512)
        tt = jnp.concatenate([t_ref[0, :, lanes], t_ref[1, :, lanes]], axis=0)
        pp = jnp.dot(m_ref[j], tt, preferred_element_type=F32)
        pc = jnp.concatenate([pp[:64], pp[64:]], axis=1).astype(BF16)
        y_ref[:, lanes] = jnp.dot(pc, cs, preferred_element_type=F32).astype(BF16)


def _fourier_tables():
    a = np.arange(64)
    ang1 = 2.0 * np.pi * np.outer(a, a) / 64.0
    w1 = np.concatenate([np.cos(ang1), -np.sin(ang1)], axis=0)
    k1 = a[:, None, None]
    k2 = a[None, :, None]
    n2 = a[None, None, :]
    theta = 2.0 * np.pi * (n2 * k2 / 64.0 + n2 * k1 / 4096.0)
    mr = np.cos(theta) / 64.0
    mi = -np.sin(theta) / 64.0
    m = np.concatenate([np.concatenate([mr, -mi], axis=2),
                        np.concatenate([mi, mr], axis=2)], axis=1)
    c = np.arange(128)
    angc = 2.0 * np.pi * np.outer(c, c) / 128.0
    eye4 = np.eye(4)
    cc = np.kron(eye4, np.cos(angc)) / np.sqrt(128.0)
    sc = np.kron(eye4, np.sin(angc)) / np.sqrt(128.0)
    cs = np.concatenate([cc, sc], axis=0)
    return tuple(jnp.asarray(t, F32).astype(BF16) for t in (w1, m, cs))


def _fourier(f):
    w1, m, cs = _fourier_tables()
    fv = f.reshape(N_ALL // 64, 64 * 512)
    n_c = 64 // F_N2_CHUNK
    t = pl.pallas_call(
        _four1_kernel,
        out_shape=jax.ShapeDtypeStruct((BATCH, 2, 64, 64, 512), BF16),
        grid=(BATCH, n_c),
        in_specs=[pl.BlockSpec((64, F_N2_CHUNK * 512), lambda b, c: (b, c)),
                  pl.BlockSpec((128, 64), lambda b, c: (0, 0))],
        out_specs=pl.BlockSpec((None, 2, F_N2_CHUNK, 64, 512), lambda b, c: (b, 0, c, 0, 0)),
        compiler_params=_cparams(("arbitrary", "arbitrary")),
        name="fourier_rows",
    )(fv, w1)
    n_k = 64 // F_K1_CHUNK
    t2 = t.reshape(BATCH, 2, 64, 64 * 512)
    y = pl.pallas_call(
        _four2_kernel,
        out_shape=jax.ShapeDtypeStruct((BATCH * 64, 64 * 512), BF16),
        grid=(BATCH, n_k),
        in_specs=[pl.BlockSpec((None, 2, 64, F_K1_CHUNK * 512), lambda b, c: (b, 0, 0, c)),
                  pl.BlockSpec((F_K1_CHUNK, 128, 128), lambda b, c: (c, 0, 0)),
                  pl.BlockSpec((1024, 512), lambda b, c: (0, 0))],
        out_specs=pl.BlockSpec((64, F_K1_CHUNK * 512), lambda b, c: (b, c)),
        compiler_params=_cparams(("arbitrary", "arbitrary")),
        name="fourier_cols",
    )(t2, m, cs)
    return y.reshape(N_MAIN, 512)


OUT_TILES = 2
TO = OUT_TILES * TM


def _route(h2, rwt_ref, rb_ref, carry):
    logits = lax.dot_general(rwt_ref[...], h2, (((1,), (1,)), ((), ())),
                             preferred_element_type=F32,
                             precision=lax.Precision.HIGHEST) + rb_ref[...]
    eidx = lax.broadcasted_iota(I32, logits.shape, 0)
    vals = logits
    sels, tops, idxs = [], [], []
    for _ in range(TOP_K):
        m = jnp.max(vals, axis=0, keepdims=True)
        idx = jnp.min(jnp.where(vals == m, eidx, N_EXPERTS), axis=0, keepdims=True)
        sel = eidx == idx
        sels.append(sel)
        tops.append(m)
        idxs.append(idx)
        vals = jnp.where(sel, -jnp.inf, vals)
    ex = [jnp.exp(t - tops[0]) for t in tops]
    den = ex[0] + ex[1] + ex[2] + ex[3]
    onehot = jnp.zeros(logits.shape, F32)
    for sel in sels:
        onehot = onehot + sel.astype(F32)
    r_i = lax.broadcasted_iota(I32, (TM, TM), 0)
    c_i = lax.broadcasted_iota(I32, (TM, TM), 1)
    upper = (r_i < c_i).astype(BF16)
    prefix = jnp.dot(onehot.astype(BF16), upper, preferred_element_type=F32)
    base = carry[:, 0:1] + prefix
    tw = [e / den for e in ex]
    rk = [jnp.sum(jnp.where(sel, base, 0.0), axis=0, keepdims=True).astype(I32) for sel in sels]
    return idxs, tw, rk, carry + jnp.sum(onehot, axis=1, keepdims=True)


def _out_tail(i, x, y, mod_ref, g_ref, rwt_ref, rb_ref, carry_ref,
              xn_ref, h2_ref, te_ref, tw_ref, rk_ref, cnt_ref):
    @pl.when(i == 0)
    def _():
        carry_ref[...] = jnp.zeros_like(carry_ref)

    carry = carry_ref[...]
    for t in range(OUT_TILES):
        rows = slice(t * TM, (t + 1) * TM)
        xn = x[rows] + mod_ref[2:3, :] * y[rows]
        xn_ref[rows, :] = xn
        h2 = _rms_mod(xn, g_ref[...], mod_ref[4:5, :], mod_ref[3:4, :])
        h2_ref[rows, :] = h2
        te, tw, rk, carry = _route(h2, rwt_ref, rb_ref, carry)
        for k in range(TOP_K):
            te_ref[k:k + 1, rows] = te[k]
            tw_ref[k:k + 1, rows] = tw[k]
            rk_ref[k:k + 1, rows] = rk[k]
    carry_ref[...] = carry
    cnt_ref[...] = carry


def _out_even_kernel(grp_ref, first_ref, last_ref,
                     x_ref, c_ref, a_ref, gb_ref, z_ref, zp_ref, zn_ref, cw_ref, w_ref, mod_ref, g_ref,
                     rwt_ref, rb_ref,
                     xn_ref, h2_ref, te_ref, tw_ref, rk_ref, cnt_ref, carry_ref):
    i = pl.program_id(0)
    z = z_ref[...]
    rid = lax.broadcasted_iota(I32, z.shape, 0)
    zm1 = jnp.where(rid == 0, zp_ref[7:8, :], pltpu.roll(z, 1, 0))
    zp1 = jnp.where(rid == TO - 1, zn_ref[0:1, :], pltpu.roll(z, TO - 1, 0))
    for t in range(OUT_TILES):
        zm1 = jnp.where(jnp.logical_and(rid == t * TM, first_ref[OUT_TILES * i + t] == 1), 0.0, zm1)
        zp1 = jnp.where(jnp.logical_and(rid == (t + 1) * TM - 1, last_ref[OUT_TILES * i + t] == 1), 0.0, zp1)
    conv = gb_ref[...] * (zm1 * cw_ref[0:1, :] + z * cw_ref[1:2, :] + zp1 * cw_ref[2:3, :])
    y = (jnp.dot(a_ref[...], w_ref[0:512, :], preferred_element_type=F32)
         + jnp.dot(conv.astype(BF16), w_ref[512:1024, :], preferred_element_type=F32))
    x = jnp.where(i < NT_MAIN // OUT_TILES, x_ref[...], c_ref[...])
    _out_tail(i, x, y, mod_ref, g_ref, rwt_ref, rb_ref, carry_ref,
              xn_ref, h2_ref, te_ref, tw_ref, rk_ref, cnt_ref)


def _out_odd_kernel(grp_ref, x_ref, a_ref, f_ref, w_ref, mod_ref, g_ref, rwt_ref, rb_ref,
                    xn_ref, h2_ref, te_ref, tw_ref, rk_ref, cnt_ref, carry_ref):
    i = pl.program_id(0)
    y = (jnp.dot(a_ref[...], w_ref[0:512, :], preferred_element_type=F32)
         + jnp.dot(f_ref[...], w_ref[512:1024, :], preferred_element_type=F32))
    _out_tail(i, x_ref[...], y, mod_ref, g_ref, rwt_ref, rb_ref, carry_ref,
              xn_ref, h2_ref, te_ref, tw_ref, rk_ref, cnt_ref)


def _out_shapes(n_rows):
    return (jax.ShapeDtypeStruct((n_rows, D), F32), jax.ShapeDtypeStruct((n_rows, D), F32),
            jax.ShapeDtypeStruct((TOP_K, n_rows), I32), jax.ShapeDtypeStruct((TOP_K, n_rows), F32),
            jax.ShapeDtypeStruct((TOP_K, n_rows), I32), jax.ShapeDtypeStruct((N_EXPERTS, 128), F32))


def _out_even(x2d, c2d, attn, gb, z, conv_w, w_bf, mod, g, rwt, rb):
    grp, _, first, last = _tile_tables()
    n_rows = N_ALL
    zblocks = n_rows // 8
    im = lambda f: (lambda i, grp, fi, la: f(i))
    tile = lambda n: pl.BlockSpec((TO, n), im(lambda i: (i, 0)))
    const = lambda shape: pl.BlockSpec(shape, im(lambda i: (0,) * len(shape)))
    tk = pl.BlockSpec((TOP_K, TO), im(lambda i: (0, i)))
    return pl.pallas_call(
        _out_even_kernel,
        out_shape=_out_shapes(n_rows),
        grid_spec=pltpu.PrefetchScalarGridSpec(
            num_scalar_prefetch=3,
            grid=(NT_ALL // OUT_TILES,),
            in_specs=[pl.BlockSpec((TO, D), im(lambda i: (jnp.minimum(i, NT_MAIN // OUT_TILES - 1), 0))),
                      pl.BlockSpec((TO, D), im(lambda i: (jnp.maximum(i - NT_MAIN // OUT_TILES, 0), 0))),
                      tile(512), tile(512), tile(512),
                      pl.BlockSpec((8, 512), im(lambda i: (jnp.maximum(i * (TO // 8) - 1, 0), 0))),
                      pl.BlockSpec((8, 512), im(lambda i: (jnp.minimum((i + 1) * (TO // 8), zblocks - 1), 0))),
                      const((3, 512)), const((D, D)),
                      pl.BlockSpec((None, 6, D), lambda i, grp, fi, la: (grp[OUT_TILES * i], 0, 0)),
                      const((1, D)), const((N_EXPERTS, D)), const((N_EXPERTS, 1))],
            out_specs=(tile(D), tile(D), tk, tk, tk, const((N_EXPERTS, 128))),
            scratch_shapes=[pltpu.VMEM((N_EXPERTS, 128), F32)],
        ),
        compiler_params=_cparams(("arbitrary",)),
        name="out_proj_even",
    )(jnp.asarray(grp), jnp.asarray(first), jnp.asarray(last),
      x2d, c2d, attn, gb, z, z, z, conv_w, w_bf, mod, g, rwt, rb)


def _out_odd(xall, attn, four, w_bf, mod, g, rwt, rb):
    grp, _, _, _ = _tile_tables()
    n_rows = N_MAIN
    im = lambda f: (lambda i, grp: f(i))
    tile = lambda n: pl.BlockSpec((TO, n), im(lambda i: (i, 0)))
    const = lambda shape: pl.BlockSpec(shape, im(lambda i: (0,) * len(shape)))
    tk = pl.BlockSpec((TOP_K, TO), im(lambda i: (0, i)))
    return pl.pallas_call(
        _out_odd_kernel,
        out_shape=_out_shapes(n_rows),
        grid_spec=pltpu.PrefetchScalarGridSpec(
            num_scalar_prefetch=1,
            grid=(NT_MAIN // OUT_TILES,),
            in_specs=[tile(D), tile(512), tile(512), const((D, D)),
                      pl.BlockSpec((None, 6, D), lambda i, grp: (grp[OUT_TILES * i], 0, 0)),
                      const((1, D)), const((N_EXPERTS, D)), const((N_EXPERTS, 1))],
            out_specs=(tile(D), tile(D), tk, tk, tk, const((N_EXPERTS, 128))),
            scratch_shapes=[pltpu.VMEM((N_EXPERTS, 128), F32)],
        ),
        compiler_params=_cparams(("arbitrary",)),
        name="out_proj_odd",
    )(jnp.asarray(grp), xall, attn, four, w_bf, mod, g, rwt, rb)


def _moe_plan(counts_f, top_e_t, rank_t, n_tok):
    counts = counts_f[:, 0].astype(I32)
    padded = (counts + TMM - 1) // TMM * TMM
    e_i = jnp.arange(N_EXPERTS, dtype=I32)
    incl = e_i[None, :] <= e_i[:, None]
    pad_end = jnp.sum(jnp.where(incl, padded[None, :], 0), axis=1)
    pad_start = pad_end - padded
    sel = top_e_t[None] == e_i[:, None, None]
    dest = jnp.sum(jnp.where(sel, pad_start[:, None, None], 0), axis=0) + rank_t
    n_blocks = n_tok * TOP_K // TMM + N_EXPERTS
    blk_start = jnp.arange(n_blocks, dtype=I32) * TMM
    block_e = jnp.minimum(jnp.sum((blk_start[:, None] >= pad_end[None, :]).astype(I32), axis=1),
                          N_EXPERTS - 1)
    n_used = (pad_end[-1] // TMM).reshape(1)
    n_tiles = n_tok // TT
    dest_flat = dest.reshape(TOP_K, n_tiles, TT).transpose(1, 0, 2).reshape(-1)
    return dest_flat, block_e, n_used, pad_end, padded


def _dispatch_kernel(pe_ref, pd_ref, nu_ref, dst_ref, h2_ref, xs_hbm, zbuf, sem, sem_z):
    i = pl.program_id(0)
    n_blocks = xs_hbm.shape[0] // TMM

    @pl.when(i == 0)
    def _():
        zbuf[...] = jnp.zeros_like(zbuf)

        def fill(b):
            return pltpu.make_async_copy(zbuf, xs_hbm.at[pl.ds(pl.multiple_of(b * TMM, TMM), TMM)], sem_z)

        def fill_expert(e, n):
            has_rows = pd_ref[e] > 0

            @pl.when(has_rows)
            def _():
                fill(pe_ref[e] // TMM - 1).start()

            return n + has_rows.astype(I32)

        def fill_tail(b, c):
            fill(b).start()
            return c

        def fill_wait(j, c):
            fill(0).wait()
            return c

        n_fill = lax.fori_loop(0, N_EXPERTS, fill_expert, 0) + n_blocks - nu_ref[0]
        lax.fori_loop(nu_ref[0], n_blocks, fill_tail, 0)
        lax.fori_loop(0, n_fill, fill_wait, 0)

    base = i * (TOP_K * TT)

    def issue(j, c):
        for k in range(TOP_K):
            pltpu.make_async_copy(h2_ref.at[pl.ds(j, 1)], xs_hbm.at[pl.ds(dst_ref[base + k * TT + j], 1)],
                                  sem).start(priority=k % 2)
        return c

    lax.fori_loop(0, TT, issue, 0, unroll=4)
    for _ in range(TOP_K):
        pltpu.make_async_copy(h2_ref, xs_hbm.at[pl.ds(0, TT)], sem).wait()


def _dispatch(h2, dest_flat, pad_end, padded, n_used, cap):
    n_tiles = h2.shape[0] // TT
    return pl.pallas_call(
        _dispatch_kernel,
        out_shape=jax.ShapeDtypeStruct((cap, D), F32),
        grid_spec=pltpu.PrefetchScalarGridSpec(
            num_scalar_prefetch=4,
            grid=(n_tiles,),
            in_specs=[pl.BlockSpec((TT, D), lambda i, pe, pd, nu, dst: (i, 0))],
            out_specs=pl.BlockSpec(memory_space=pl.ANY),
            scratch_shapes=[pltpu.VMEM((TMM, D), F32), pltpu.SemaphoreType.DMA, pltpu.SemaphoreType.DMA],
        ),
        compiler_params=_cparams(("arbitrary",)),
        name="moe_dispatch",
    )(pad_end, padded, n_used, dest_flat, h2)


def _moe_kernel(be_ref, nu_ref, x_ref, wgu_ref, bgu_ref, wdn_ref, bdn_ref, y_ref, wgu_bf, wdn_bf):
    i = pl.program_id(0)
    prev = be_ref[jnp.maximum(i - 1, 0)]

    @pl.when((i < nu_ref[0]) & ((i == 0) | (be_ref[i] != prev)))
    def _():
        wgu_bf[...] = wgu_ref[...].astype(BF16)
        wdn_bf[...] = wdn_ref[...].astype(BF16)

    @pl.when(i < nu_ref[0])
    def _():
        xb = x_ref[...].astype(BF16)
        gu = jnp.dot(xb, wgu_bf[...], preferred_element_type=F32) + bgu_ref[...]
        gate = jnp.minimum(gu[:, :D], SWIGLU_LIMIT)
        up = jnp.clip(gu[:, D:], -SWIGLU_LIMIT, SWIGLU_LIMIT)
        act = (up + 1.0) * (gate * (1.0 / (1.0 + jnp.exp(-SWIGLU_ALPHA * gate))))
        y_ref[...] = jnp.dot(act.astype(BF16), wdn_bf[...], preferred_element_type=F32) + bdn_ref[...]

    @pl.when(i >= nu_ref[0])
    def _():
        y_ref[...] = jnp.zeros_like(y_ref)


def _moe(layer, xs, block_e, n_used, w_gu, b_gu, w_dn, b_dn):
    n_blocks = block_e.shape[0]
    n_l = w_gu.shape[0]
    blk = lambda i, be, nu: (jnp.minimum(i, nu[0] - 1), 0)
    out_blk = lambda i, be, nu: (i, 0)
    exp4 = lambda i, be, nu: (layer, be[jnp.minimum(i, nu[0] - 1)], 0, 0)
    return pl.pallas_call(
        _moe_kernel,
        out_shape=jax.ShapeDtypeStruct((n_blocks * TMM, D), F32),
        grid_spec=pltpu.PrefetchScalarGridSpec(
            num_scalar_prefetch=2,
            grid=(n_blocks,),
            in_specs=[pl.BlockSpec((TMM, D), blk),
                      pl.BlockSpec((None, None, D, 2 * D), exp4),
                      pl.BlockSpec((None, None, 1, 2 * D), exp4),
                      pl.BlockSpec((None, None, D, D), exp4),
                      pl.BlockSpec((None, None, 1, D), exp4)],
            out_specs=pl.BlockSpec((TMM, D), out_blk),
            scratch_shapes=[pltpu.VMEM((D, 2 * D), BF16), pltpu.VMEM((D, D), BF16)],
        ),
        compiler_params=_cparams(("arbitrary",)),
        name="moe_experts",
    )(block_e, n_used, xs, w_gu, b_gu.reshape(n_l, N_EXPERTS, 1, 2 * D), w_dn,
      b_dn.reshape(n_l, N_EXPERTS, 1, D))


def _combine_kernel(final, grp_ref, dst_ref, ys_hbm, x_ref, tw_ref, mod_ref, fn_ref, o_ref, buf, sem):
    i = pl.program_id(0)
    n_tiles = pl.num_programs(0)

    def gather(tile, slot):
        base = tile * (TOP_K * TT)

        def issue(j, c):
            for k in range(TOP_K):
                pltpu.make_async_copy(ys_hbm.at[pl.ds(dst_ref[base + k * TT + j], 1)],
                                      buf.at[slot, k, pl.ds(j, 1)], sem.at[slot]).start(priority=k % 2)
            return c

        lax.fori_loop(0, TT, issue, 0, unroll=4)

    @pl.when(i == 0)
    def _():
        gather(0, 0)

    @pl.when(i + 1 < n_tiles)
    def _():
        gather(i + 1, (i + 1) % 2)

    slot = i % 2
    for k in range(TOP_K):
        pltpu.make_async_copy(ys_hbm.at[pl.ds(0, TT)], buf.at[slot, k], sem.at[slot]).wait()
    tw = tw_ref[...]
    acc = tw[:, 0:1] * buf[slot, 0]
    for k in range(1, TOP_K):
        acc = acc + tw[:, k:k + 1] * buf[slot, k]
    out = x_ref[...] + mod_ref[5:6, :] * acc
    if final:
        ms = jnp.mean(out * out, axis=-1, keepdims=True)
        out = out * lax.rsqrt(ms + EPS) * fn_ref[...]
    o_ref[...] = out


def _combine(xn, ys, dest_flat, top_w_t, mod, final_norm, n_tok, final):
    grp, _, _, _ = _tile_tables()
    n_tiles = n_tok // TT
    tw = top_w_t.T
    return pl.pallas_call(
        functools.partial(_combine_kernel, final),
        out_shape=jax.ShapeDtypeStruct((n_tok, D), F32),
        grid_spec=pltpu.PrefetchScalarGridSpec(
            num_scalar_prefetch=2,
            grid=(n_tiles,),
            in_specs=[pl.BlockSpec(memory_space=pl.ANY),
                      pl.BlockSpec((TT, D), lambda i, grp, dst: (i, 0)),
                      pl.BlockSpec((TT, TOP_K), lambda i, grp, dst: (i, 0)),
                      pl.BlockSpec((None, 6, D), lambda i, grp, dst: (grp[i * (TT // TM)], 0, 0)),
                      pl.BlockSpec((1, D), lambda i, grp, dst: (0, 0))],
            out_specs=pl.BlockSpec((TT, D), lambda i, grp, dst: (i, 0)),
            scratch_shapes=[pltpu.VMEM((2, TOP_K, TT, D), F32), pltpu.SemaphoreType.DMA((2,))],
        ),
        compiler_params=_cparams(("arbitrary",)),
        name="moe_combine",
    )(jnp.asarray(grp), dest_flat, ys, xn, tw, mod, final_norm.reshape(1, D))


def _moe_layer(layer, xn, h2, top_e_t, top_w_t, rank_t, counts, mod, w_gu, b_gu, w_dn, b_dn, final_norm, final):
    n_tok = xn.shape[0]
    dest_flat, block_e, n_used, pad_end, padded = _moe_plan(counts, top_e_t, rank_t, n_tok)
    xs = _dispatch(h2, dest_flat, pad_end, padded, n_used, block_e.shape[0] * TMM)
    ys = _moe(layer, xs, block_e, n_used, w_gu, b_gu, w_dn, b_dn)
    return _combine(xn, ys, dest_flat, top_w_t, mod, final_norm, n_tok, final)


def kernel(x, c, ctx, c_ctx, ada_w, ada_b, norm_mix, norm_ffn, even_w_in, even_w_out, even_conv_w, even_sink, odd_w_in, odd_w_out, odd_rpb, router_w, router_b, moe_w_gu, moe_b_gu, moe_w_dn, moe_b_dn, final_norm):
    x2d = x.reshape(N_MAIN, D)
    c2d = ctx.reshape(N_CTX, D)
    cc = jnp.concatenate([c, c_ctx[None, :], jnp.zeros((3, D), F32)], axis=0)
    mod = _ada(cc, ada_w, ada_b).reshape(2, 8, 6, D)
    cos_f, sin_f = _rope_tables()

    q, k, ks, v, vs, gb, z = _in_even(x2d, c2d, mod[0], norm_mix[0:1], even_w_in[0].astype(BF16), cos_f, sin_f)
    attn = jnp.concatenate([_win_attn(even_sink[0], q, k, ks, v, vs),
                            _ctx_attn(even_sink[0], q, k, ks, v, vs)], axis=0)
    xn, h2, te, tw, rk, cnt = _out_even(x2d, c2d, attn, gb, z, even_conv_w[0], even_w_out[0].astype(BF16),
                                        mod[0], norm_ffn[0:1], router_w[0].T, router_b[0][:, None])
    xall = _moe_layer(0, xn, h2, te, tw, rk, cnt, mod[0], moe_w_gu, moe_b_gu, moe_w_dn, moe_b_dn,
                      final_norm, False)

    q, k, v, f = _in_odd(xall, mod[1], norm_mix[1:2], odd_w_in[0].astype(BF16))
    attn = _na_attn(q, k, v, _na_bias(odd_rpb[0]))
    four = _fourier(f)
    xn, h2, te, tw, rk, cnt = _out_odd(xall, attn, four, odd_w_out[0].astype(BF16),
                                       mod[1], norm_ffn[1:2], router_w[1].T, router_b[1][:, None])
    out = _moe_layer(1, xn, h2, te, tw, rk, cnt, mod[1], moe_w_gu, moe_b_gu, moe_w_dn, moe_b_dn,
                     final_norm, True)
    return out.reshape(BATCH, SEQ, D)
```

```python
import functools

import numpy as np
import jax
import jax.numpy as jnp
from jax import lax
from jax.experimental import pallas as pl
from jax.experimental.pallas import tpu as pltpu

F32 = jnp.float32
BF16 = jnp.bfloat16
I32 = jnp.int32

D = 1024
BATCH = 4
SEQ = 4096
CTX = 256
GRID_W = 64
HEAD_DIM = 64
EPS = 1e-6
ROPE_THETA = 10000.0
N_EXPERTS = 32
TOP_K = 4
SWIGLU_LIMIT = 7.0
SWIGLU_ALPHA = 1.702
NA_ROWS = 8
NA_COLS = 16

N_MAIN = BATCH * SEQ
N_CTX = BATCH * CTX
N_ALL = N_MAIN + N_CTX
TM = 256
NT_MAIN = N_MAIN // TM
NT_ALL = N_ALL // TM
TILES_PER_SEQ = SEQ // TM
TMM = 512
TT = 512
VMEM_LIMIT = 56 * 1024 * 1024


def _cparams(sem, vmem=VMEM_LIMIT):
    return pltpu.CompilerParams(dimension_semantics=sem, vmem_limit_bytes=vmem)


def _rms_mod(x, g, sc, sh):
    ms = jnp.mean(x * x, axis=-1, keepdims=True)
    return (x * lax.rsqrt(ms + EPS) * g) * (1.0 + sc) + sh


def _ada_kernel(c_ref, w_ref, b_ref, o_ref):
    c = c_ref[...]
    s = c * (1.0 / (1.0 + jnp.exp(-c)))
    o_ref[...] = jnp.dot(s, w_ref[...], preferred_element_type=F32,
                         precision=lax.Precision.HIGHEST) + b_ref[...]


def _ada(cc, ada_w, ada_b):
    n_l = ada_w.shape[0]
    tn = 1024
    return pl.pallas_call(
        _ada_kernel,
        out_shape=jax.ShapeDtypeStruct((n_l, 8, 6 * D), F32),
        grid=(n_l, 6 * D // tn),
        in_specs=[pl.BlockSpec((8, D), lambda l, j: (0, 0)),
                  pl.BlockSpec((None, D, tn), lambda l, j: (l, 0, j)),
                  pl.BlockSpec((None, 1, tn), lambda l, j: (l, 0, j))],
        out_specs=pl.BlockSpec((None, 8, tn), lambda l, j: (l, 0, j)),
        compiler_params=_cparams(("arbitrary", "arbitrary")),
        name="ada_mod",
    )(cc, ada_w, ada_b.reshape(n_l, 1, 6 * D))


def _rope_apply(t, cos, sin):
    n = t.shape[1]
    lane = lax.broadcasted_iota(I32, t.shape, 1)
    fwd = pltpu.roll(t, n - 32, 1)
    bwd = pltpu.roll(t, 32, 1)
    rot = jnp.where((lane % 64) < 32, fwd, bwd)
    reps = n // 128
    cosf = jnp.concatenate([cos] * reps, axis=1) if reps > 1 else cos
    sinf = jnp.concatenate([sin] * reps, axis=1) if reps > 1 else sin
    return t * cosf + rot * sinf


def _in_even_kernel(grp_ref, rblk_ref, x_ref, c_ref, mod_ref, g_ref, w_ref, cos_ref, sin_ref,
                    q_ref, k_ref, ks_ref, v_ref, vs_ref, gb_ref, z_ref):
    xt = jnp.where(pl.program_id(0) < NTI_MAIN, x_ref[...], c_ref[...])
    h = _rms_mod(xt, g_ref[...], mod_ref[1:2, :], mod_ref[0:1, :])
    p = jnp.dot(h.astype(BF16), w_ref[...], preferred_element_type=F32)
    cos = cos_ref[...]
    sin = sin_ref[...]
    q = _rope_apply(p[:, 0:512], cos, sin) * (HEAD_DIM ** -0.5)
    k = _rope_apply(p[:, 512:640], cos, sin)
    v = p[:, 640:768]
    q_ref[...] = q.astype(BF16)
    k_ref[...] = k.astype(BF16)
    ks_ref[...] = pltpu.roll(k, 64, 1).astype(BF16)
    v_ref[...] = v.astype(BF16)
    vs_ref[...] = pltpu.roll(v, 64, 1).astype(BF16)
    gb_ref[...] = p[:, 768:1280]
    z_ref[...] = p[:, 1280:1792] * p[:, 1792:2304]


def _in_odd_kernel(grp_ref, x_ref, mod_ref, g_ref, w_ref, q_ref, k_ref, v_ref, f_ref):
    h = _rms_mod(x_ref[...], g_ref[...], mod_ref[1:2, :], mod_ref[0:1, :])
    p = jnp.dot(h.astype(BF16), w_ref[...], preferred_element_type=F32)
    q_ref[...] = (p[:, 0:512] * (HEAD_DIM ** -0.5)).astype(BF16)
    k_ref[...] = p[:, 512:1024].astype(BF16)
    v_ref[...] = p[:, 1024:1536].astype(BF16)
    f_ref[...] = p[:, 1536:2048].astype(BF16)


def _tile_tables():
    t = np.arange(NT_ALL)
    main = t < NT_MAIN
    grp = np.where(main, t // TILES_PER_SEQ, BATCH).astype(np.int32)
    rblk = np.where(main, t % TILES_PER_SEQ, TILES_PER_SEQ).astype(np.int32)
    first = np.where(main, (t % TILES_PER_SEQ) == 0, True).astype(np.int32)
    last = np.where(main, (t % TILES_PER_SEQ) == TILES_PER_SEQ - 1, True).astype(np.int32)
    return grp, rblk, first, last


TI = 512
NTI_MAIN = N_MAIN // TI
NTI_ALL = N_ALL // TI


def _in_tile_tables():
    t = np.arange(NTI_ALL)
    main = t < NTI_MAIN
    grp = np.where(main, t // (SEQ // TI), BATCH).astype(np.int32)
    rblk = np.where(main, t % (SEQ // TI), SEQ // TI).astype(np.int32)
    return grp, rblk


def _rope_tables():
    t = jnp.arange(SEQ, dtype=I32)
    row = (t // GRID_W).astype(F32)
    col = (t % GRID_W).astype(F32)
    n_freq = HEAD_DIM // 4
    inv_freq = jnp.power(ROPE_THETA, -jnp.arange(n_freq, dtype=F32) / n_freq)
    ang = jnp.concatenate([row[:, None] * inv_freq, col[:, None] * inv_freq], axis=-1)
    cos = jnp.cos(ang)
    sin = jnp.sin(ang)
    cos_f = jnp.concatenate([cos, cos, cos, cos], axis=1)
    sin_f = jnp.concatenate([-sin, sin, -sin, sin], axis=1)
    cos_f = jnp.concatenate([cos_f, jnp.ones((TI, 128), F32)], axis=0)
    sin_f = jnp.concatenate([sin_f, jnp.zeros((TI, 128), F32)], axis=0)
    return cos_f, sin_f


def _in_even(x2d, c2d, mod, g, w_bf, cos_f, sin_f):
    grp, rblk = _in_tile_tables()
    row = lambda n, dt: jax.ShapeDtypeStruct((N_ALL, n), dt)
    tile = lambda n: pl.BlockSpec((TI, n), lambda i, grp, rb: (i, 0))
    x_spec = pl.BlockSpec((TI, D), lambda i, grp, rb: (jnp.minimum(i, NTI_MAIN - 1), 0))
    c_spec = pl.BlockSpec((TI, D), lambda i, grp, rb: (jnp.maximum(i - NTI_MAIN, 0), 0))
    return pl.pallas_call(
        _in_even_kernel,
        out_shape=(row(512, BF16), row(128, BF16), row(128, BF16), row(128, BF16), row(128, BF16),
                   row(512, F32), row(512, F32)),
        grid_spec=pltpu.PrefetchScalarGridSpec(
            num_scalar_prefetch=2,
            grid=(NTI_ALL,),
            in_specs=[x_spec, c_spec,
                      pl.BlockSpec((None, 6, D), lambda i, grp, rb: (grp[i], 0, 0)),
                      pl.BlockSpec((1, D), lambda i, grp, rb: (0, 0)),
                      pl.BlockSpec((D, 2304), lambda i, grp, rb: (0, 0)),
                      pl.BlockSpec((TI, 128), lambda i, grp, rb: (rb[i], 0)),
                      pl.BlockSpec((TI, 128), lambda i, grp, rb: (rb[i], 0))],
            out_specs=(tile(512), tile(128), tile(128), tile(128), tile(128), tile(512), tile(512)),
        ),
        compiler_params=_cparams(("arbitrary",)),
        name="in_proj_even",
    )(jnp.asarray(grp), jnp.asarray(rblk), x2d, c2d, mod, g, w_bf, cos_f, sin_f)


def _in_odd(xall, mod, g, w_bf):
    grp, _ = _in_tile_tables()
    row = lambda n, dt: jax.ShapeDtypeStruct((N_ALL, n), dt)
    tile = lambda n: pl.BlockSpec((TI, n), lambda i, grp: (i, 0))
    return pl.pallas_call(
        _in_odd_kernel,
        out_shape=(row(512, BF16), row(512, BF16), row(512, BF16), row(512, BF16)),
        grid_spec=pltpu.PrefetchScalarGridSpec(
            num_scalar_prefetch=1,
            grid=(NTI_ALL,),
            in_specs=[tile(D),
                      pl.BlockSpec((None, 6, D), lambda i, grp: (grp[i], 0, 0)),
                      pl.BlockSpec((1, D), lambda i, grp: (0, 0)),
                      pl.BlockSpec((D, 2048), lambda i, grp: (0, 0))],
            out_specs=(tile(512), tile(512), tile(512), tile(512)),
        ),
        compiler_params=_cparams(("arbitrary",)),
        name="in_proj_odd",
    )(jnp.asarray(grp), xall, mod, g, w_bf)


def _nt(a, b):
    return lax.dot_general(a, b, (((1,), (1,)), ((), ())), preferred_element_type=F32)


def _half_mask(shape, half):
    lane = lax.broadcasted_iota(I32, shape, 1)
    return (lane < 64) if half == 0 else (lane >= 64)


def _win_kernel(sink_ref, q_ref, k_ref, ks_ref, v_ref, vs_ref, kc_ref, ksc_ref, vc_ref, vsc_ref, o_ref):
    n = pl.program_id(1)
    start = pl.multiple_of(jnp.clip((n - 1) * 128, 0, SEQ - 384), 128)
    win = pl.ds(start, 384)
    row = lax.broadcasted_iota(I32, (256, 384), 0)
    col = lax.broadcasted_iota(I32, (256, 384), 1)
    valid = jnp.abs((n * 128 + row % 128) - (start + col)) <= 128
    first = lax.broadcasted_iota(I32, (256, 1), 0) < 128
    kk = (jnp.concatenate([k_ref[win, :], kc_ref[...]], axis=0),
          jnp.concatenate([ks_ref[win, :], ksc_ref[...]], axis=0))
    vv = (jnp.concatenate([v_ref[win, :], vc_ref[...]], axis=0),
          jnp.concatenate([vs_ref[win, :], vsc_ref[...]], axis=0))
    outs = {}
    for hk in range(2):
        for hf in range(2):
            swapped = 0 if hk == hf else 1
            chunks = (2 * hk, 2 * hk + 1)
            qs = [q_ref[:, c * 128:(c + 1) * 128] for c in chunks]
            q2 = jnp.concatenate([jnp.where(_half_mask(t.shape, hf), t, jnp.zeros_like(t)) for t in qs],
                                 axis=0)
            s = _nt(q2, kk[swapped])
            s_loc = jnp.where(valid, s[:, :384], -jnp.inf)
            s_ctx = s[:, 384:]
            sink = jnp.where(first, sink_ref[2 * chunks[0] + hf], sink_ref[2 * chunks[1] + hf])
            m = jnp.maximum(jnp.maximum(jnp.max(s_loc, axis=1, keepdims=True),
                                        jnp.max(s_ctx, axis=1, keepdims=True)), sink)
            p_loc = jnp.exp(s_loc - m)
            p_ctx = jnp.exp(s_ctx - m)
            den = (jnp.sum(p_loc, axis=1, keepdims=True) + jnp.sum(p_ctx, axis=1, keepdims=True)
                   + jnp.exp(sink - m))
            p = jnp.concatenate([p_loc, p_ctx], axis=1).astype(BF16)
            o = jnp.dot(p, vv[swapped], preferred_element_type=F32) / den
            outs[(chunks[0], hf)] = o[:128]
            outs[(chunks[1], hf)] = o[128:]
    for c in range(4):
        o_ref[:, c * 128:(c + 1) * 128] = jnp.where(_half_mask((128, 128), 0),
                                                    outs[(c, 0)], outs[(c, 1)]).astype(BF16)


def _win_attn(sink, q, k, ks, v, vs):
    nb = SEQ // 128
    seq_spec = pl.BlockSpec((SEQ, 128), lambda b, n: (b, 0))
    ctx_spec = pl.BlockSpec((CTX, 128), lambda b, n: (N_MAIN // CTX + b, 0))
    return pl.pallas_call(
        _win_kernel,
        out_shape=jax.ShapeDtypeStruct((N_MAIN, 512), BF16),
        grid=(BATCH, nb),
        in_specs=[pl.BlockSpec(memory_space=pltpu.SMEM),
                  pl.BlockSpec((128, 512), lambda b, n: (b * (SEQ // 128) + n, 0)),
                  seq_spec, seq_spec, seq_spec, seq_spec,
                  ctx_spec, ctx_spec, ctx_spec, ctx_spec],
        out_specs=pl.BlockSpec((128, 512), lambda b, n: (b * (SEQ // 128) + n, 0)),
        compiler_params=_cparams(("arbitrary", "arbitrary")),
        name="window_attn",
    )(sink, q, k, ks, v, vs, k, ks, v, vs)


def _ctx_attn_kernel(sink_ref, q_ref, k_ref, ks_ref, v_ref, vs_ref, o_ref):
    kk = (k_ref[...], ks_ref[...])
    vv = (v_ref[...], vs_ref[...])
    for c in range(4):
        qc = q_ref[:, c * 128:(c + 1) * 128]
        halves = []
        for hf in range(2):
            h = 2 * c + hf
            swapped = 0 if (h // 4) == hf else 1
            qm = jnp.where(_half_mask(qc.shape, hf), qc, jnp.zeros_like(qc))
            s = _nt(qm, kk[swapped])
            sink = sink_ref[h]
            m = jnp.maximum(jnp.max(s, axis=1, keepdims=True), sink)
            p = jnp.exp(s - m)
            den = jnp.sum(p, axis=1, keepdims=True) + jnp.exp(sink - m)
            halves.append(jnp.dot(p.astype(BF16), vv[swapped], preferred_element_type=F32) / den)
        o_ref[:, c * 128:(c + 1) * 128] = jnp.where(_half_mask(halves[0].shape, 0),
                                                    halves[0], halves[1]).astype(BF16)


def _ctx_attn(sink, q, k, ks, v, vs):
    ctx_spec = lambda n: pl.BlockSpec((CTX, n), lambda b: (N_MAIN // CTX + b, 0))
    return pl.pallas_call(
        _ctx_attn_kernel,
        out_shape=jax.ShapeDtypeStruct((N_CTX, 512), BF16),
        grid=(BATCH,),
        in_specs=[pl.BlockSpec(memory_space=pltpu.SMEM),
                  ctx_spec(512), ctx_spec(128), ctx_spec(128), ctx_spec(128), ctx_spec(128)],
        out_specs=pl.BlockSpec((CTX, 512), lambda b: (b, 0)),
        compiler_params=_cparams(("arbitrary",)),
        name="context_attn",
    )(sink, q, k, ks, v, vs)


NA_GROUP = 16
N_GRID_ROWS = SEQ // GRID_W


def _na_kernel(q_ref, k_ref, v_ref, kc_ref, vc_ref, nb_ref, o_ref):
    g = pl.program_id(1)

    def body(i, carry):
        r = g * NA_GROUP + i
        r0 = jnp.clip(r - NA_ROWS // 2, 0, N_GRID_ROWS - NA_ROWS)
        shift = r0 - r + NA_ROWS - 1
        qrows = pl.ds(pl.multiple_of(i * GRID_W, GRID_W), GRID_W)
        krows = pl.ds(pl.multiple_of(r0 * GRID_W, GRID_W), NA_ROWS * GRID_W)
        n_loc = NA_ROWS * GRID_W
        for c in range(4):
            lanes = slice(c * 128, (c + 1) * 128)
            qc = q_ref[qrows, lanes]
            q2 = jnp.concatenate([jnp.where(_half_mask(qc.shape, hf), qc, jnp.zeros_like(qc))
                                  for hf in range(2)], axis=0)
            kcat = jnp.concatenate([k_ref[krows, lanes], kc_ref[:, lanes]], axis=0)
            vcat = jnp.concatenate([v_ref[krows, lanes], vc_ref[:, lanes]], axis=0)
            s = _nt(q2, kcat)
            bias = jnp.concatenate([nb_ref[2 * c, shift], nb_ref[2 * c + 1, shift]], axis=0)
            s_loc = s[:, :n_loc] + bias
            s_ctx = s[:, n_loc:]
            m = jnp.maximum(jnp.max(s_loc, axis=1, keepdims=True), jnp.max(s_ctx, axis=1, keepdims=True))
            p_loc = jnp.exp(s_loc - m)
            p_ctx = jnp.exp(s_ctx - m)
            den = jnp.sum(p_loc, axis=1, keepdims=True) + jnp.sum(p_ctx, axis=1, keepdims=True)
            p = jnp.concatenate([p_loc, p_ctx], axis=1).astype(BF16)
            o = jnp.dot(p, vcat, preferred_element_type=F32) / den
            o_ref[qrows, lanes] = jnp.where(_half_mask(qc.shape, 0), o[:GRID_W], o[GRID_W:]).astype(BF16)
        return carry

    lax.fori_loop(0, NA_GROUP, body, 0, unroll=4)


def _na_bias(rpb):
    col = np.arange(GRID_W)
    c0 = np.clip(col - NA_COLS // 2, 0, GRID_W - NA_COLS)
    col_ok = (col[None, :] >= c0[:, None]) & (col[None, :] < c0[:, None] + NA_COLS)
    dc = np.clip(col[None, :] - col[:, None] + NA_COLS - 1, 0, 2 * NA_COLS - 2)
    onehot = (dc[None] == np.arange(2 * NA_COLS - 1)[:, None, None]).astype(np.float32)
    e = jnp.einsum('hrd,dqk->hrqk', rpb.astype(F32), jnp.asarray(onehot),
                   precision=lax.Precision.HIGHEST)
    e = jnp.where(col_ok[None, None], e, -jnp.inf)
    b = jnp.stack([e[:, s:s + NA_ROWS] for s in range(NA_ROWS)], axis=1)
    b = jnp.transpose(b, (0, 1, 3, 2, 4))
    return b.reshape(rpb.shape[0], NA_ROWS, GRID_W, NA_ROWS * GRID_W)


def _na_attn(q, k, v, nb):
    qrows = NA_GROUP * GRID_W
    n_g = SEQ // qrows
    seq_spec = pl.BlockSpec((SEQ, 512), lambda b, g: (b, 0))
    ctx_spec = pl.BlockSpec((CTX, 512), lambda b, g: (N_MAIN // CTX + b, 0))
    return pl.pallas_call(
        _na_kernel,
        out_shape=jax.ShapeDtypeStruct((N_MAIN, 512), BF16),
        grid=(BATCH, n_g),
        in_specs=[pl.BlockSpec((qrows, 512), lambda b, g: (b * n_g + g, 0)),
                  seq_spec, seq_spec, ctx_spec, ctx_spec,
                  pl.BlockSpec(nb.shape, lambda b, g: (0, 0, 0, 0))],
        out_specs=pl.BlockSpec((qrows, 512), lambda b, g: (b * n_g + g, 0)),
        compiler_params=_cparams(("arbitrary", "arbitrary")),
        name="neighborhood_attn",
    )(q, k, v, k, v, nb)


F_N2_CHUNK = 8
F_K1_CHUNK = 8


def _four1_kernel(x_ref, w_ref, t_ref):
    w = w_ref[...]
    for j in range(F_N2_CHUNK):
        res = jnp.dot(w, x_ref[:, j * 512:(j + 1) * 512], preferred_element_type=F32)
        t_ref[0, j] = res[:64].astype(BF16)
        t_ref[1, j] = res[64:].astype(BF16)


def _four2_kernel(t_ref, m_ref, cs_ref, y_ref):
    cs = cs_ref[...]
    for j in range(F_K1_CHUNK):
        lanes = slice(j * 512, (j + 1) * 512)
        tt = jnp.concatenate([t_ref[0, :, lanes], t_ref[1, :, lanes]], axis=0)
        pp = jnp.dot(m_ref[j], tt, preferred_element_type=F32)
        pc = jnp.concatenate([pp[:64], pp[64:]], axis=1).astype(BF16)
        y_ref[:, lanes] = jnp.dot(pc, cs, preferred_element_type=F32).astype(BF16)


def _fourier_tables():
    a = np.arange(64)
    ang1 = 2.0 * np.pi * np.outer(a, a) / 64.0
    w1 = np.concatenate([np.cos(ang1), -np.sin(ang1)], axis=0)
    k1 = a[:, None, None]
    k2 = a[None, :, None]
    n2 = a[None, None, :]
    theta = 2.0 * np.pi * (n2 * k2 / 64.0 + n2 * k1 / 4096.0)
    mr = np.cos(theta) / 64.0
    mi = -np.sin(theta) / 64.0
    m = np.concatenate([np.concatenate([mr, -mi], axis=2),
                        np.concatenate([mi, mr], axis=2)], axis=1)
    c = np.arange(128)
    angc = 2.0 * np.pi * np.outer(c, c) / 128.0
    eye4 = np.eye(4)
    cc = np.kron(eye4, np.cos(angc)) / np.sqrt(128.0)
    sc = np.kron(eye4, np.sin(angc)) / np.sqrt(128.0)
    cs = np.concatenate([cc, sc], axis=0)
    return tuple(jnp.asarray(t, F32).astype(BF16) for t in (w1, m, cs))


def _fourier(f):
    w1, m, cs = _fourier_tables()
    fv = f.reshape(N_ALL // 64, 64 * 512)
    n_c = 64 // F_N2_CHUNK
    t = pl.pallas_call(
        _four1_kernel,
        out_shape=jax.ShapeDtypeStruct((BATCH, 2, 64, 64, 512), BF16),
        grid=(BATCH, n_c),
        in_specs=[pl.BlockSpec((64, F_N2_CHUNK * 512), lambda b, c: (b, c)),
                  pl.BlockSpec((128, 64), lambda b, c: (0, 0))],
        out_specs=pl.BlockSpec((None, 2, F_N2_CHUNK, 64, 512), lambda b, c: (b, 0, c, 0, 0)),
        compiler_params=_cparams(("arbitrary", "arbitrary")),
        name="fourier_rows",
    )(fv, w1)
    n_k = 64 // F_K1_CHUNK
    t2 = t.reshape(BATCH, 2, 64, 64 * 512)
    y = pl.pallas_call(
        _four2_kernel,
        out_shape=jax.ShapeDtypeStruct((BATCH * 64, 64 * 512), BF16),
        grid=(BATCH, n_k),
        in_specs=[pl.BlockSpec((None, 2, 64, F_K1_CHUNK * 512), lambda b, c: (b, 0, 0, c)),
                  pl.BlockSpec((F_K1_CHUNK, 128, 128), lambda b, c: (c, 0, 0)),
                  pl.BlockSpec((1024, 512), lambda b, c: (0, 0))],
        out_specs=pl.BlockSpec((64, F_K1_CHUNK * 512), lambda b, c: (b, c)),
        compiler_params=_cparams(("arbitrary", "arbitrary")),
        name="fourier_cols",
    )(t2, m, cs)
    return y.reshape(N_MAIN, 512)


OUT_TILES = 2
TO = OUT_TILES * TM


def _route(h2, rwt_ref, rb_ref, carry):
    logits = lax.dot_general(rwt_ref[...], h2, (((1,), (1,)), ((), ())),
                             preferred_element_type=F32,
                             precision=lax.Precision.HIGHEST) + rb_ref[...]
    eidx = lax.broadcasted_iota(I32, logits.shape, 0)
    vals = logits
    sels, tops, idxs = [], [], []
    for _ in range(TOP_K):
        m = jnp.max(vals, axis=0, keepdims=True)
        idx = jnp.min(jnp.where(vals == m, eidx, N_EXPERTS), axis=0, keepdims=True)
        sel = eidx == idx
        sels.append(sel)
        tops.append(m)
        idxs.append(idx)
        vals = jnp.where(sel, -jnp.inf, vals)
    ex = [jnp.exp(t - tops[0]) for t in tops]
    den = ex[0] + ex[1] + ex[2] + ex[3]
    onehot = jnp.zeros(logits.shape, F32)
    for sel in sels:
        onehot = onehot + sel.astype(F32)
    r_i = lax.broadcasted_iota(I32, (TM, TM), 0)
    c_i = lax.broadcasted_iota(I32, (TM, TM), 1)
    upper = (r_i < c_i).astype(BF16)
    prefix = jnp.dot(onehot.astype(BF16), upper, preferred_element_type=F32)
    base = carry[:, 0:1] + prefix
    tw = [e / den for e in ex]
    rk = [jnp.sum(jnp.where(sel, base, 0.0), axis=0, keepdims=True).astype(I32) for sel in sels]
    return idxs, tw, rk, carry + jnp.sum(onehot, axis=1, keepdims=True)


def _out_tail(i, x, y, mod_ref, g_ref, rwt_ref, rb_ref, carry_ref,
              xn_ref, h2_ref, te_ref, tw_ref, rk_ref, cnt_ref):
    @pl.when(i == 0)
    def _():
        carry_ref[...] = jnp.zeros_like(carry_ref)

    carry = carry_ref[...]
    for t in range(OUT_TILES):
        rows = slice(t * TM, (t + 1) * TM)
        xn = x[rows] + mod_ref[2:3, :] * y[rows]
        xn_ref[rows, :] = xn
        h2 = _rms_mod(xn, g_ref[...], mod_ref[4:5, :], mod_ref[3:4, :])
        h2_ref[rows, :] = h2
        te, tw, rk, carry = _route(h2, rwt_ref, rb_ref, carry)
        for k in range(TOP_K):
            te_ref[k:k + 1, rows] = te[k]
            tw_ref[k:k + 1, rows] = tw[k]
            rk_ref[k:k + 1, rows] = rk[k]
    carry_ref[...] = carry
    cnt_ref[...] = carry


def _out_even_kernel(grp_ref, first_ref, last_ref,
                     x_ref, c_ref, a_ref, gb_ref, z_ref, zp_ref, zn_ref, cw_ref, w_ref, mod_ref, g_ref,
                     rwt_ref, rb_ref,
                     xn_ref, h2_ref, te_ref, tw_ref, rk_ref, cnt_ref, carry_ref):
    i = pl.program_id(0)
    z = z_ref[...]
    rid = lax.broadcasted_iota(I32, z.shape, 0)
    zm1 = jnp.where(rid == 0, zp_ref[7:8, :], pltpu.roll(z, 1, 0))
    zp1 = jnp.where(rid == TO - 1, zn_ref[0:1, :], pltpu.roll(z, TO - 1, 0))
    for t in range(OUT_TILES):
        zm1 = jnp.where(jnp.logical_and(rid == t * TM, first_ref[OUT_TILES * i + t] == 1), 0.0, zm1)
        zp1 = jnp.where(jnp.logical_and(rid == (t + 1) * TM - 1, last_ref[OUT_TILES * i + t] == 1), 0.0, zp1)
    conv = gb_ref[...] * (zm1 * cw_ref[0:1, :] + z * cw_ref[1:2, :] + zp1 * cw_ref[2:3, :])
    y = (jnp.dot(a_ref[...], w_ref[0:512, :], preferred_element_type=F32)
         + jnp.dot(conv.astype(BF16), w_ref[512:1024, :], preferred_element_type=F32))
    x = jnp.where(i < NT_MAIN // OUT_TILES, x_ref[...], c_ref[...])
    _out_tail(i, x, y, mod_ref, g_ref, rwt_ref, rb_ref, carry_ref,
              xn_ref, h2_ref, te_ref, tw_ref, rk_ref, cnt_ref)


def _out_odd_kernel(grp_ref, x_ref, a_ref, f_ref, w_ref, mod_ref, g_ref, rwt_ref, rb_ref,
                    xn_ref, h2_ref, te_ref, tw_ref, rk_ref, cnt_ref, carry_ref):
    i = pl.program_id(0)
    y = (jnp.dot(a_ref[...], w_ref[0:512, :], preferred_element_type=F32)
         + jnp.dot(f_ref[...], w_ref[512:1024, :], preferred_element_type=F32))
    _out_tail(i, x_ref[...], y, mod_ref, g_ref, rwt_ref, rb_ref, carry_ref,
              xn_ref, h2_ref, te_ref, tw_ref, rk_ref, cnt_ref)


def _out_shapes(n_rows):
    return (jax.ShapeDtypeStruct((n_rows, D), F32), jax.ShapeDtypeStruct((n_rows, D), F32),
            jax.ShapeDtypeStruct((TOP_K, n_rows), I32), jax.ShapeDtypeStruct((TOP_K, n_rows), F32),
            jax.ShapeDtypeStruct((TOP_K, n_rows), I32), jax.ShapeDtypeStruct((N_EXPERTS, 128), F32))


def _out_even(x2d, c2d, attn, gb, z, conv_w, w_bf, mod, g, rwt, rb):
    grp, _, first, last = _tile_tables()
    n_rows = N_ALL
    zblocks = n_rows // 8
    im = lambda f: (lambda i, grp, fi, la: f(i))
    tile = lambda n: pl.BlockSpec((TO, n), im(lambda i: (i, 0)))
    const = lambda shape: pl.BlockSpec(shape, im(lambda i: (0,) * len(shape)))
    tk = pl.BlockSpec((TOP_K, TO), im(lambda i: (0, i)))
    return pl.pallas_call(
        _out_even_kernel,
        out_shape=_out_shapes(n_rows),
        grid_spec=pltpu.PrefetchScalarGridSpec(
            num_scalar_prefetch=3,
            grid=(NT_ALL // OUT_TILES,),
            in_specs=[pl.BlockSpec((TO, D), im(lambda i: (jnp.minimum(i, NT_MAIN // OUT_TILES - 1), 0))),
                      pl.BlockSpec((TO, D), im(lambda i: (jnp.maximum(i - NT_MAIN // OUT_TILES, 0), 0))),
                      tile(512), tile(512), tile(512),
                      pl.BlockSpec((8, 512), im(lambda i: (jnp.maximum(i * (TO // 8) - 1, 0), 0))),
                      pl.BlockSpec((8, 512), im(lambda i: (jnp.minimum((i + 1) * (TO // 8), zblocks - 1), 0))),
                      const((3, 512)), const((D, D)),
                      pl.BlockSpec((None, 6, D), lambda i, grp, fi, la: (grp[OUT_TILES * i], 0, 0)),
                      const((1, D)), const((N_EXPERTS, D)), const((N_EXPERTS, 1))],
            out_specs=(tile(D), tile(D), tk, tk, tk, const((N_EXPERTS, 128))),
            scratch_shapes=[pltpu.VMEM((N_EXPERTS, 128), F32)],
        ),
        compiler_params=_cparams(("arbitrary",)),
        name="out_proj_even",
    )(jnp.asarray(grp), jnp.asarray(first), jnp.asarray(last),
      x2d, c2d, attn, gb, z, z, z, conv_w, w_bf, mod, g, rwt, rb)


def _out_odd(xall, attn, four, w_bf, mod, g, rwt, rb):
    grp, _, _, _ = _tile_tables()
    n_rows = N_MAIN
    im = lambda f: (lambda i, grp: f(i))
    tile = lambda n: pl.BlockSpec((TO, n), im(lambda i: (i, 0)))
    const = lambda shape: pl.BlockSpec(shape, im(lambda i: (0,) * len(shape)))
    tk = pl.BlockSpec((TOP_K, TO), im(lambda i: (0, i)))
    return pl.pallas_call(
        _out_odd_kernel,
        out_shape=_out_shapes(n_rows),
        grid_spec=pltpu.PrefetchScalarGridSpec(
            num_scalar_prefetch=1,
            grid=(NT_MAIN // OUT_TILES,),
            in_specs=[tile(D), tile(512), tile(512), const((D, D)),
                      pl.BlockSpec((None, 6, D), lambda i, grp: (grp[OUT_TILES * i], 0, 0)),
                      const((1, D)), const((N_EXPERTS, D)), const((N_EXPERTS, 1))],
            out_specs=(tile(D), tile(D), tk, tk, tk, const((N_EXPERTS, 128))),
            scratch_shapes=[pltpu.VMEM((N_EXPERTS, 128), F32)],
        ),
        compiler_params=_cparams(("arbitrary",)),
        name="out_proj_odd",
    )(jnp.asarray(grp), xall, attn, four, w_bf, mod, g, rwt, rb)


def _moe_plan(counts_f, top_e_t, rank_t, n_tok):
    counts = counts_f[:, 0].astype(I32)
    padded = (counts + TMM - 1) // TMM * TMM
    e_i = jnp.arange(N_EXPERTS, dtype=I32)
    incl = e_i[None, :] <= e_i[:, None]
    pad_end = jnp.sum(jnp.where(incl, padded[None, :], 0), axis=1)
    pad_start = pad_end - padded
    sel = top_e_t[None] == e_i[:, None, None]
    dest = jnp.sum(jnp.where(sel, pad_start[:, None, None], 0), axis=0) + rank_t
    n_blocks = n_tok * TOP_K // TMM + N_EXPERTS
    blk_start = jnp.arange(n_blocks, dtype=I32) * TMM
    block_e = jnp.minimum(jnp.sum((blk_start[:, None] >= pad_end[None, :]).astype(I32), axis=1),
                          N_EXPERTS - 1)
    n_used = (pad_end[-1] // TMM).reshape(1)
    cend = pad_start + counts
    cend_b = jnp.sum(jnp.where(block_e[:, None] == e_i[None, :], cend[None, :], 0), axis=1)
    n_valid = jnp.clip(cend_b - blk_start, 0, TMM)
    n_tiles = n_tok // TT
    dest_flat = dest.reshape(TOP_K, n_tiles, TT).transpose(1, 0, 2).reshape(-1)
    return dest_flat, block_e, n_used, n_valid, pad_end, padded


def _dispatch_kernel(pe_ref, pd_ref, nu_ref, dst_ref, h2_ref, xs_hbm, zbuf, sem, sem_z):
    i = pl.program_id(0)
    n_blocks = xs_hbm.shape[0] // TMM

    @pl.when(i == 0)
    def _():
        zbuf[...] = jnp.zeros_like(zbuf)

        def fill(b):
            return pltpu.make_async_copy(zbuf, xs_hbm.at[pl.ds(pl.multiple_of(b * TMM, TMM), TMM)], sem_z)

        def fill_expert(e, n):
            has_rows = pd_ref[e] > 0

            @pl.when(has_rows)
            def _():
                fill(pe_ref[e] // TMM - 1).start()

            return n + has_rows.astype(I32)

        def fill_tail(b, c):
            fill(b).start()
            return c

        def fill_wait(j, c):
            fill(0).wait()
            return c

        n_fill = lax.fori_loop(0, N_EXPERTS, fill_expert, 0) + n_blocks - nu_ref[0]
        lax.fori_loop(nu_ref[0], n_blocks, fill_tail, 0)
        lax.fori_loop(0, n_fill, fill_wait, 0)

    base = i * (TOP_K * TT)

    def issue(j, c):
        for k in range(TOP_K):
            pltpu.make_async_copy(h2_ref.at[pl.ds(j, 1)], xs_hbm.at[pl.ds(dst_ref[base + k * TT + j], 1)],
                                  sem).start(priority=k % 2)
        return c

    lax.fori_loop(0, TT, issue, 0, unroll=4)
    for _ in range(TOP_K):
        pltpu.make_async_copy(h2_ref, xs_hbm.at[pl.ds(0, TT)], sem).wait()


def _dispatch(h2, dest_flat, pad_end, padded, n_used, cap):
    n_tiles = h2.shape[0] // TT
    return pl.pallas_call(
        _dispatch_kernel,
        out_shape=jax.ShapeDtypeStruct((cap, D), F32),
        grid_spec=pltpu.PrefetchScalarGridSpec(
            num_scalar_prefetch=4,
            grid=(n_tiles,),
            in_specs=[pl.BlockSpec((TT, D), lambda i, pe, pd, nu, dst: (i, 0))],
            out_specs=pl.BlockSpec(memory_space=pl.ANY),
            scratch_shapes=[pltpu.VMEM((TMM, D), F32), pltpu.SemaphoreType.DMA, pltpu.SemaphoreType.DMA],
        ),
        compiler_params=_cparams(("arbitrary",)),
        name="moe_dispatch",
    )(pad_end, padded, n_used, dest_flat, h2)


def _moe_kernel(be_ref, nu_ref, nv_ref, x_ref, wgu_ref, bgu_ref, wdn_ref, bdn_ref, y_ref, wgu_bf, wdn_bf):
    i = pl.program_id(0)
    prev = be_ref[jnp.maximum(i - 1, 0)]
    used = i < nu_ref[0]
    half = TMM // 2

    @pl.when(used & ((i == 0) | (be_ref[i] != prev)))
    def _():
        wgu_bf[...] = wgu_ref[...].astype(BF16)
        wdn_bf[...] = wdn_ref[...].astype(BF16)

    def ffn(x):
        gu = jnp.dot(x.astype(BF16), wgu_bf[...], preferred_element_type=F32) + bgu_ref[...]
        gate = jnp.minimum(gu[:, :D], SWIGLU_LIMIT)
        up = jnp.clip(gu[:, D:], -SWIGLU_LIMIT, SWIGLU_LIMIT)
        act = (up + 1.0) * (gate * (1.0 / (1.0 + jnp.exp(-SWIGLU_ALPHA * gate))))
        return jnp.dot(act.astype(BF16), wdn_bf[...], preferred_element_type=F32) + bdn_ref[...]

    @pl.when(used & (nv_ref[i] > half))
    def _():
        y_ref[...] = ffn(x_ref[...])

    @pl.when(used & (nv_ref[i] <= half))
    def _():
        y_ref[0:half, :] = ffn(x_ref[0:half, :])
        y_ref[half:TMM, :] = jnp.zeros((TMM - half, D), F32)

    @pl.when(i >= nu_ref[0])
    def _():
        y_ref[...] = jnp.zeros_like(y_ref)


def _moe(layer, xs, block_e, n_used, n_valid, w_gu, b_gu, w_dn, b_dn):
    n_blocks = block_e.shape[0]
    n_l = w_gu.shape[0]
    blk = lambda i, be, nu, nv: (jnp.minimum(i, nu[0] - 1), 0)
    out_blk = lambda i, be, nu, nv: (i, 0)
    exp4 = lambda i, be, nu, nv: (layer, be[jnp.minimum(i, nu[0] - 1)], 0, 0)
    return pl.pallas_call(
        _moe_kernel,
        out_shape=jax.ShapeDtypeStruct((n_blocks * TMM, D), F32),
        grid_spec=pltpu.PrefetchScalarGridSpec(
            num_scalar_prefetch=3,
            grid=(n_blocks,),
            in_specs=[pl.BlockSpec((TMM, D), blk),
                      pl.BlockSpec((None, None, D, 2 * D), exp4),
                      pl.BlockSpec((None, None, 1, 2 * D), exp4),
                      pl.BlockSpec((None, None, D, D), exp4),
                      pl.BlockSpec((None, None, 1, D), exp4)],
            out_specs=pl.BlockSpec((TMM, D), out_blk),
            scratch_shapes=[pltpu.VMEM((D, 2 * D), BF16), pltpu.VMEM((D, D), BF16)],
        ),
        compiler_params=_cparams(("arbitrary",)),
        name="moe_experts",
    )(block_e, n_used, n_valid, xs, w_gu, b_gu.reshape(n_l, N_EXPERTS, 1, 2 * D), w_dn,
      b_dn.reshape(n_l, N_EXPERTS, 1, D))


def _combine_kernel(final, grp_ref, dst_ref, ys_hbm, x_ref, tw_ref, mod_ref, fn_ref, o_ref, buf, sem):
    i = pl.program_id(0)
    n_tiles = pl.num_programs(0)

    def gather(tile, slot):
        base = tile * (TOP_K * TT)

        def issue(j, c):
            for k in range(TOP_K):
                pltpu.make_async_copy(ys_hbm.at[pl.ds(dst_ref[base + k * TT + j], 1)],
                                      buf.at[slot, k, pl.ds(j, 1)], sem.at[slot]).start(priority=k % 2)
            return c

        lax.fori_loop(0, TT, issue, 0, unroll=4)

    @pl.when(i == 0)
    def _():
        gather(0, 0)

    @pl.when(i + 1 < n_tiles)
    def _():
        gather(i + 1, (i + 1) % 2)

    slot = i % 2
    for k in range(TOP_K):
        pltpu.make_async_copy(ys_hbm.at[pl.ds(0, TT)], buf.at[slot, k], sem.at[slot]).wait()
    tw = tw_ref[...]
    acc = tw[:, 0:1] * buf[slot, 0]
    for k in range(1, TOP_K):
        acc = acc + tw[:, k:k + 1] * buf[slot, k]
    out = x_ref[...] + mod_ref[5:6, :] * acc
    if final:
        ms = jnp.mean(out * out, axis=-1, keepdims=True)
        out = out * lax.rsqrt(ms + EPS) * fn_ref[...]
    o_ref[...] = out


def _combine(xn, ys, dest_flat, top_w_t, mod, final_norm, n_tok, final):
    grp, _, _, _ = _tile_tables()
    n_tiles = n_tok // TT
    tw = top_w_t.T
    return pl.pallas_call(
        functools.partial(_combine_kernel, final),
        out_shape=jax.ShapeDtypeStruct((n_tok, D), F32),
        grid_spec=pltpu.PrefetchScalarGridSpec(
            num_scalar_prefetch=2,
            grid=(n_tiles,),
            in_specs=[pl.BlockSpec(memory_space=pl.ANY),
                      pl.BlockSpec((TT, D), lambda i, grp, dst: (i, 0)),
                      pl.BlockSpec((TT, TOP_K), lambda i, grp, dst: (i, 0)),
                      pl.BlockSpec((None, 6, D), lambda i, grp, dst: (grp[i * (TT // TM)], 0, 0)),
                      pl.BlockSpec((1, D), lambda i, grp, dst: (0, 0))],
            out_specs=pl.BlockSpec((TT, D), lambda i, grp, dst: (i, 0)),
            scratch_shapes=[pltpu.VMEM((2, TOP_K, TT, D), F32), pltpu.SemaphoreType.DMA((2,))],
        ),
        compiler_params=_cparams(("arbitrary",)),
        name="moe_combine",
    )(jnp.asarray(grp), dest_flat, ys, xn, tw, mod, final_norm.reshape(1, D))


def _moe_layer(layer, xn, h2, top_e_t, top_w_t, rank_t, counts, mod, w_gu, b_gu, w_dn, b_dn, final_norm, final):
    n_tok = xn.shape[0]
    dest_flat, block_e, n_used, n_valid, pad_end, padded = _moe_plan(counts, top_e_t, rank_t, n_tok)
    xs = _dispatch(h2, dest_flat, pad_end, padded, n_used, block_e.shape[0] * TMM)
    ys = _moe(layer, xs, block_e, n_used, n_valid, w_gu, b_gu, w_dn, b_dn)
    return _combine(xn, ys, dest_flat, top_w_t, mod, final_norm, n_tok, final)


def kernel(x, c, ctx, c_ctx, ada_w, ada_b, norm_mix, norm_ffn, even_w_in, even_w_out, even_conv_w, even_sink, odd_w_in, odd_w_out, odd_rpb, router_w, router_b, moe_w_gu, moe_b_gu, moe_w_dn, moe_b_dn, final_norm):
    x2d = x.reshape(N_MAIN, D)
    c2d = ctx.reshape(N_CTX, D)
    cc = jnp.concatenate([c, c_ctx[None, :], jnp.zeros((3, D), F32)], axis=0)
    mod = _ada(cc, ada_w, ada_b).reshape(2, 8, 6, D)
    cos_f, sin_f = _rope_tables()

    q, k, ks, v, vs, gb, z = _in_even(x2d, c2d, mod[0], norm_mix[0:1], even_w_in[0].astype(BF16), cos_f, sin_f)
    attn = jnp.concatenate([_win_attn(even_sink[0], q, k, ks, v, vs),
                            _ctx_attn(even_sink[0], q, k, ks, v, vs)], axis=0)
    xn, h2, te, tw, rk, cnt = _out_even(x2d, c2d, attn, gb, z, even_conv_w[0], even_w_out[0].astype(BF16),
                                        mod[0], norm_ffn[0:1], router_w[0].T, router_b[0][:, None])
    xall = _moe_layer(0, xn, h2, te, tw, rk, cnt, mod[0], moe_w_gu, moe_b_gu, moe_w_dn, moe_b_dn,
                      final_norm, False)

    q, k, v, f = _in_odd(xall, mod[1], norm_mix[1:2], odd_w_in[0].astype(BF16))
    attn = _na_attn(q, k, v, _na_bias(odd_rpb[0]))
    four = _fourier(f)
    xn, h2, te, tw, rk, cnt = _out_odd(xall, attn, four, odd_w_out[0].astype(BF16),
                                       mod[1], norm_ffn[1:2], router_w[1].T, router_b[1][:, None])
    out = _moe_layer(1, xn, h2, te, tw, rk, cnt, mod[1], moe_w_gu, moe_b_gu, moe_w_dn, moe_b_dn,
                     final_norm, True)
    return out.reshape(BATCH, SEQ, D)
```

```python
import functools

import numpy as np
import jax
import jax.numpy as jnp
from jax import lax
from jax.experimental import pallas as pl
from jax.experimental.pallas import tpu as pltpu

F32 = jnp.float32
BF16 = jnp.bfloat16
I32 = jnp.int32

D = 1024
BATCH = 4
SEQ = 4096
CTX = 256
GRID_W = 64
HEAD_DIM = 64
EPS = 1e-6
ROPE_THETA = 10000.0
N_EXPERTS = 32
TOP_K = 4
SWIGLU_LIMIT = 7.0
SWIGLU_ALPHA = 1.702
NA_ROWS = 8
NA_COLS = 16

N_MAIN = BATCH * SEQ
N_CTX = BATCH * CTX
N_ALL = N_MAIN + N_CTX
TM = 256
NT_MAIN = N_MAIN // TM
NT_ALL = N_ALL // TM
TILES_PER_SEQ = SEQ // TM
TMM = 512
TT = 512
VMEM_LIMIT = 56 * 1024 * 1024


def _cparams(sem, vmem=VMEM_LIMIT):
    return pltpu.CompilerParams(dimension_semantics=sem, vmem_limit_bytes=vmem)


def _rms_mod(x, g, sc, sh):
    ms = jnp.mean(x * x, axis=-1, keepdims=True)
    return (x * lax.rsqrt(ms + EPS) * g) * (1.0 + sc) + sh


def _ada_kernel(c_ref, w_ref, b_ref, o_ref):
    c = c_ref[...]
    s = c * (1.0 / (1.0 + jnp.exp(-c)))
    o_ref[...] = jnp.dot(s, w_ref[...], preferred_element_type=F32,
                         precision=lax.Precision.HIGHEST) + b_ref[...]


def _ada(cc, ada_w, ada_b):
    n_l = ada_w.shape[0]
    tn = 1024
    return pl.pallas_call(
        _ada_kernel,
        out_shape=jax.ShapeDtypeStruct((n_l, 8, 6 * D), F32),
        grid=(n_l, 6 * D // tn),
        in_specs=[pl.BlockSpec((8, D), lambda l, j: (0, 0)),
                  pl.BlockSpec((None, D, tn), lambda l, j: (l, 0, j)),
                  pl.BlockSpec((None, 1, tn), lambda l, j: (l, 0, j))],
        out_specs=pl.BlockSpec((None, 8, tn), lambda l, j: (l, 0, j)),
        compiler_params=_cparams(("arbitrary", "arbitrary")),
        name="ada_mod",
    )(cc, ada_w, ada_b.reshape(n_l, 1, 6 * D))


def _rope_apply(t, cos, sin):
    n = t.shape[1]
    lane = lax.broadcasted_iota(I32, t.shape, 1)
    fwd = pltpu.roll(t, n - 32, 1)
    bwd = pltpu.roll(t, 32, 1)
    rot = jnp.where((lane % 64) < 32, fwd, bwd)
    reps = n // 128
    cosf = jnp.concatenate([cos] * reps, axis=1) if reps > 1 else cos
    sinf = jnp.concatenate([sin] * reps, axis=1) if reps > 1 else sin
    return t * cosf + rot * sinf


def _in_even_kernel(grp_ref, rblk_ref, x_ref, c_ref, mod_ref, g_ref, w_ref, cos_ref, sin_ref,
                    q_ref, k_ref, ks_ref, v_ref, vs_ref, gb_ref, z_ref):
    xt = jnp.where(pl.program_id(0) < NTI_MAIN, x_ref[...], c_ref[...])
    h = _rms_mod(xt, g_ref[...], mod_ref[1:2, :], mod_ref[0:1, :])
    p = jnp.dot(h.astype(BF16), w_ref[...], preferred_element_type=F32)
    cos = cos_ref[...]
    sin = sin_ref[...]
    q = _rope_apply(p[:, 0:512], cos, sin) * (HEAD_DIM ** -0.5)
    k = _rope_apply(p[:, 512:640], cos, sin)
    v = p[:, 640:768]
    q_ref[...] = q.astype(BF16)
    k_ref[...] = k.astype(BF16)
    ks_ref[...] = pltpu.roll(k, 64, 1).astype(BF16)
    v_ref[...] = v.astype(BF16)
    vs_ref[...] = pltpu.roll(v, 64, 1).astype(BF16)
    gb_ref[...] = p[:, 768:1280]
    z_ref[...] = p[:, 1280:1792] * p[:, 1792:2304]


def _in_odd_kernel(grp_ref, x_ref, mod_ref, g_ref, w_ref, q_ref, k_ref, v_ref, f_ref):
    h = _rms_mod(x_ref[...], g_ref[...], mod_ref[1:2, :], mod_ref[0:1, :])
    p = jnp.dot(h.astype(BF16), w_ref[...], preferred_element_type=F32)
    q_ref[...] = (p[:, 0:512] * (HEAD_DIM ** -0.5)).astype(BF16)
    k_ref[...] = p[:, 512:1024].astype(BF16)
    v_ref[...] = p[:, 1024:1536].astype(BF16)
    f_ref[...] = p[:, 1536:2048].astype(BF16)


def _tile_tables():
    t = np.arange(NT_ALL)
    main = t < NT_MAIN
    grp = np.where(main, t // TILES_PER_SEQ, BATCH).astype(np.int32)
    rblk = np.where(main, t % TILES_PER_SEQ, TILES_PER_SEQ).astype(np.int32)
    first = np.where(main, (t % TILES_PER_SEQ) == 0, True).astype(np.int32)
    last = np.where(main, (t % TILES_PER_SEQ) == TILES_PER_SEQ - 1, True).astype(np.int32)
    return grp, rblk, first, last


TI = 512
NTI_MAIN = N_MAIN // TI
NTI_ALL = N_ALL // TI


def _in_tile_tables():
    t = np.arange(NTI_ALL)
    main = t < NTI_MAIN
    grp = np.where(main, t // (SEQ // TI), BATCH).astype(np.int32)
    rblk = np.where(main, t % (SEQ // TI), SEQ // TI).astype(np.int32)
    return grp, rblk


def _rope_tables():
    t = jnp.arange(SEQ, dtype=I32)
    row = (t // GRID_W).astype(F32)
    col = (t % GRID_W).astype(F32)
    n_freq = HEAD_DIM // 4
    inv_freq = jnp.power(ROPE_THETA, -jnp.arange(n_freq, dtype=F32) / n_freq)
    ang = jnp.concatenate([row[:, None] * inv_freq, col[:, None] * inv_freq], axis=-1)
    cos = jnp.cos(ang)
    sin = jnp.sin(ang)
    cos_f = jnp.concatenate([cos, cos, cos, cos], axis=1)
    sin_f = jnp.concatenate([-sin, sin, -sin, sin], axis=1)
    cos_f = jnp.concatenate([cos_f, jnp.ones((TI, 128), F32)], axis=0)
    sin_f = jnp.concatenate([sin_f, jnp.zeros((TI, 128), F32)], axis=0)
    return cos_f, sin_f


def _in_even(x2d, c2d, mod, g, w_bf, cos_f, sin_f):
    grp, rblk = _in_tile_tables()
    row = lambda n, dt: jax.ShapeDtypeStruct((N_ALL, n), dt)
    tile = lambda n: pl.BlockSpec((TI, n), lambda i, grp, rb: (i, 0))
    x_spec = pl.BlockSpec((TI, D), lambda i, grp, rb: (jnp.minimum(i, NTI_MAIN - 1), 0))
    c_spec = pl.BlockSpec((TI, D), lambda i, grp, rb: (jnp.maximum(i - NTI_MAIN, 0), 0))
    return pl.pallas_call(
        _in_even_kernel,
        out_shape=(row(512, BF16), row(128, BF16), row(128, BF16), row(128, BF16), row(128, BF16),
                   row(512, F32), row(512, F32)),
        grid_spec=pltpu.PrefetchScalarGridSpec(
            num_scalar_prefetch=2,
            grid=(NTI_ALL,),
            in_specs=[x_spec, c_spec,
                      pl.BlockSpec((None, 6, D), lambda i, grp, rb: (grp[i], 0, 0)),
                      pl.BlockSpec((1, D), lambda i, grp, rb: (0, 0)),
                      pl.BlockSpec((D, 2304), lambda i, grp, rb: (0, 0)),
                      pl.BlockSpec((TI, 128), lambda i, grp, rb: (rb[i], 0)),
                      pl.BlockSpec((TI, 128), lambda i, grp, rb: (rb[i], 0))],
            out_specs=(tile(512), tile(128), tile(128), tile(128), tile(128), tile(512), tile(512)),
        ),
        compiler_params=_cparams(("arbitrary",)),
        name="in_proj_even",
    )(jnp.asarray(grp), jnp.asarray(rblk), x2d, c2d, mod, g, w_bf, cos_f, sin_f)


def _in_odd(xall, mod, g, w_bf):
    grp, _ = _in_tile_tables()
    row = lambda n, dt: jax.ShapeDtypeStruct((N_ALL, n), dt)
    tile = lambda n: pl.BlockSpec((TI, n), lambda i, grp: (i, 0))
    return pl.pallas_call(
        _in_odd_kernel,
        out_shape=(row(512, BF16), row(512, BF16), row(512, BF16), row(512, BF16)),
        grid_spec=pltpu.PrefetchScalarGridSpec(
            num_scalar_prefetch=1,
            grid=(NTI_ALL,),
            in_specs=[tile(D),
                      pl.BlockSpec((None, 6, D), lambda i, grp: (grp[i], 0, 0)),
                      pl.BlockSpec((1, D), lambda i, grp: (0, 0)),
                      pl.BlockSpec((D, 2048), lambda i, grp: (0, 0))],
            out_specs=(tile(512), tile(512), tile(512), tile(512)),
        ),
        compiler_params=_cparams(("arbitrary",)),
        name="in_proj_odd",
    )(jnp.asarray(grp), xall, mod, g, w_bf)


def _nt(a, b):
    return lax.dot_general(a, b, (((1,), (1,)), ((), ())), preferred_element_type=F32)


def _half_mask(shape, half):
    lane = lax.broadcasted_iota(I32, shape, 1)
    return (lane < 64) if half == 0 else (lane >= 64)


def _win_kernel(sink_ref, q_ref, k_ref, ks_ref, v_ref, vs_ref, kc_ref, ksc_ref, vc_ref, vsc_ref, o_ref):
    n = pl.program_id(1)
    start = pl.multiple_of(jnp.clip((n - 1) * 128, 0, SEQ - 384), 128)
    win = pl.ds(start, 384)
    row = lax.broadcasted_iota(I32, (256, 384), 0)
    col = lax.broadcasted_iota(I32, (256, 384), 1)
    valid = jnp.abs((n * 128 + row % 128) - (start + col)) <= 128
    first = lax.broadcasted_iota(I32, (256, 1), 0) < 128
    kk = (jnp.concatenate([k_ref[win, :], kc_ref[...]], axis=0),
          jnp.concatenate([ks_ref[win, :], ksc_ref[...]], axis=0))
    vv = (jnp.concatenate([v_ref[win, :], vc_ref[...]], axis=0),
          jnp.concatenate([vs_ref[win, :], vsc_ref[...]], axis=0))
    outs = {}
    for hk in range(2):
        for hf in range(2):
            swapped = 0 if hk == hf else 1
            chunks = (2 * hk, 2 * hk + 1)
            qs = [q_ref[:, c * 128:(c + 1) * 128] for c in chunks]
            q2 = jnp.concatenate([jnp.where(_half_mask(t.shape, hf), t, jnp.zeros_like(t)) for t in qs],
                                 axis=0)
            s = _nt(q2, kk[swapped])
            s_loc = jnp.where(valid, s[:, :384], -jnp.inf)
            s_ctx = s[:, 384:]
            sink = jnp.where(first, sink_ref[2 * chunks[0] + hf], sink_ref[2 * chunks[1] + hf])
            m = jnp.maximum(jnp.maximum(jnp.max(s_loc, axis=1, keepdims=True),
                                        jnp.max(s_ctx, axis=1, keepdims=True)), sink)
            p_loc = jnp.exp(s_loc - m)
            p_ctx = jnp.exp(s_ctx - m)
            den = (jnp.sum(p_loc, axis=1, keepdims=True) + jnp.sum(p_ctx, axis=1, keepdims=True)
                   + jnp.exp(sink - m))
            p = jnp.concatenate([p_loc, p_ctx], axis=1).astype(BF16)
            o = jnp.dot(p, vv[swapped], preferred_element_type=F32) / den
            outs[(chunks[0], hf)] = o[:128]
            outs[(chunks[1], hf)] = o[128:]
    for c in range(4):
        o_ref[:, c * 128:(c + 1) * 128] = jnp.where(_half_mask((128, 128), 0),
                                                    outs[(c, 0)], outs[(c, 1)]).astype(BF16)


def _win_attn(sink, q, k, ks, v, vs):
    nb = SEQ // 128
    seq_spec = pl.BlockSpec((SEQ, 128), lambda b, n: (b, 0))
    ctx_spec = pl.BlockSpec((CTX, 128), lambda b, n: (N_MAIN // CTX + b, 0))
    return pl.pallas_call(
        _win_kernel,
        out_shape=jax.ShapeDtypeStruct((N_MAIN, 512), BF16),
        grid=(BATCH, nb),
        in_specs=[pl.BlockSpec(memory_space=pltpu.SMEM),
                  pl.BlockSpec((128, 512), lambda b, n: (b * (SEQ // 128) + n, 0)),
                  seq_spec, seq_spec, seq_spec, seq_spec,
                  ctx_spec, ctx_spec, ctx_spec, ctx_spec],
        out_specs=pl.BlockSpec((128, 512), lambda b, n: (b * (SEQ // 128) + n, 0)),
        compiler_params=_cparams(("arbitrary", "arbitrary")),
        name="window_attn",
    )(sink, q, k, ks, v, vs, k, ks, v, vs)


def _ctx_attn_kernel(sink_ref, q_ref, k_ref, ks_ref, v_ref, vs_ref, o_ref):
    kk = (k_ref[...], ks_ref[...])
    vv = (v_ref[...], vs_ref[...])
    for c in range(4):
        qc = q_ref[:, c * 128:(c + 1) * 128]
        halves = []
        for hf in range(2):
            h = 2 * c + hf
            swapped = 0 if (h // 4) == hf else 1
            qm = jnp.where(_half_mask(qc.shape, hf), qc, jnp.zeros_like(qc))
            s = _nt(qm, kk[swapped])
            sink = sink_ref[h]
            m = jnp.maximum(jnp.max(s, axis=1, keepdims=True), sink)
            p = jnp.exp(s - m)
            den = jnp.sum(p, axis=1, keepdims=True) + jnp.exp(sink - m)
            halves.append(jnp.dot(p.astype(BF16), vv[swapped], preferred_element_type=F32) / den)
        o_ref[:, c * 128:(c + 1) * 128] = jnp.where(_half_mask(halves[0].shape, 0),
                                                    halves[0], halves[1]).astype(BF16)


def _ctx_attn(sink, q, k, ks, v, vs):
    ctx_spec = lambda n: pl.BlockSpec((CTX, n), lambda b: (N_MAIN // CTX + b, 0))
    return pl.pallas_call(
        _ctx_attn_kernel,
        out_shape=jax.ShapeDtypeStruct((N_CTX, 512), BF16),
        grid=(BATCH,),
        in_specs=[pl.BlockSpec(memory_space=pltpu.SMEM),
                  ctx_spec(512), ctx_spec(128), ctx_spec(128), ctx_spec(128), ctx_spec(128)],
        out_specs=pl.BlockSpec((CTX, 512), lambda b: (b, 0)),
        compiler_params=_cparams(("arbitrary",)),
        name="context_attn",
    )(sink, q, k, ks, v, vs)


NA_GROUP = 16
N_GRID_ROWS = SEQ // GRID_W


def _na_kernel(q_ref, k_ref, v_ref, kc_ref, vc_ref, nb_ref, o_ref):
    g = pl.program_id(1)

    def body(i, carry):
        r = g * NA_GROUP + i
        r0 = jnp.clip(r - NA_ROWS // 2, 0, N_GRID_ROWS - NA_ROWS)
        shift = r0 - r + NA_ROWS - 1
        qrows = pl.ds(pl.multiple_of(i * GRID_W, GRID_W), GRID_W)
        krows = pl.ds(pl.multiple_of(r0 * GRID_W, GRID_W), NA_ROWS * GRID_W)
        n_loc = NA_ROWS * GRID_W
        for c in range(4):
            lanes = slice(c * 128, (c + 1) * 128)
            qc = q_ref[qrows, lanes]
            q2 = jnp.concatenate([jnp.where(_half_mask(qc.shape, hf), qc, jnp.zeros_like(qc))
                                  for hf in range(2)], axis=0)
            kcat = jnp.concatenate([k_ref[krows, lanes], kc_ref[:, lanes]], axis=0)
            vcat = jnp.concatenate([v_ref[krows, lanes], vc_ref[:, lanes]], axis=0)
            s = _nt(q2, kcat)
            bias = jnp.concatenate([nb_ref[2 * c, shift], nb_ref[2 * c + 1, shift]], axis=0)
            s_loc = s[:, :n_loc] + bias
            s_ctx = s[:, n_loc:]
            m = jnp.maximum(jnp.max(s_loc, axis=1, keepdims=True), jnp.max(s_ctx, axis=1, keepdims=True))
            p_loc = jnp.exp(s_loc - m)
            p_ctx = jnp.exp(s_ctx - m)
            den = jnp.sum(p_loc, axis=1, keepdims=True) + jnp.sum(p_ctx, axis=1, keepdims=True)
            p = jnp.concatenate([p_loc, p_ctx], axis=1).astype(BF16)
            o = jnp.dot(p, vcat, preferred_element_type=F32) / den
            o_ref[qrows, lanes] = jnp.where(_half_mask(qc.shape, 0), o[:GRID_W], o[GRID_W:]).astype(BF16)
        return carry

    lax.fori_loop(0, NA_GROUP, body, 0, unroll=4)


def _na_bias(rpb):
    col = np.arange(GRID_W)
    c0 = np.clip(col - NA_COLS // 2, 0, GRID_W - NA_COLS)
    col_ok = (col[None, :] >= c0[:, None]) & (col[None, :] < c0[:, None] + NA_COLS)
    dc = np.clip(col[None, :] - col[:, None] + NA_COLS - 1, 0, 2 * NA_COLS - 2)
    onehot = (dc[None] == np.arange(2 * NA_COLS - 1)[:, None, None]).astype(np.float32)
    e = jnp.einsum('hrd,dqk->hrqk', rpb.astype(F32), jnp.asarray(onehot),
                   precision=lax.Precision.HIGHEST)
    e = jnp.where(col_ok[None, None], e, -jnp.inf)
    b = jnp.stack([e[:, s:s + NA_ROWS] for s in range(NA_ROWS)], axis=1)
    b = jnp.transpose(b, (0, 1, 3, 2, 4))
    return b.reshape(rpb.shape[0], NA_ROWS, GRID_W, NA_ROWS * GRID_W)


def _na_attn(q, k, v, nb):
    qrows = NA_GROUP * GRID_W
    n_g = SEQ // qrows
    seq_spec = pl.BlockSpec((SEQ, 512), lambda b, g: (b, 0))
    ctx_spec = pl.BlockSpec((CTX, 512), lambda b, g: (N_MAIN // CTX + b, 0))
    return pl.pallas_call(
        _na_kernel,
        out_shape=jax.ShapeDtypeStruct((N_MAIN, 512), BF16),
        grid=(BATCH, n_g),
        in_specs=[pl.BlockSpec((qrows, 512), lambda b, g: (b * n_g + g, 0)),
                  seq_spec, seq_spec, ctx_spec, ctx_spec,
                  pl.BlockSpec(nb.shape, lambda b, g: (0, 0, 0, 0))],
        out_specs=pl.BlockSpec((qrows, 512), lambda b, g: (b * n_g + g, 0)),
        compiler_params=_cparams(("arbitrary", "arbitrary")),
        name="neighborhood_attn",
    )(q, k, v, k, v, nb)


F_N2_CHUNK = 8
F_K1_CHUNK = 8


def _four1_kernel(x_ref, w_ref, t_ref):
    w = w_ref[...]
    for j in range(F_N2_CHUNK):
        res = jnp.dot(w, x_ref[:, j * 512:(j + 1) * 512], preferred_element_type=F32)
        t_ref[0, j] = res[:64].astype(BF16)
        t_ref[1, j] = res[64:].astype(BF16)


def _four2_kernel(t_ref, m_ref, cs_ref, y_ref):
    cs = cs_ref[...]
    for j in range(F_K1_CHUNK):
        lanes = slice(j * 512, (j + 1) * 512)
        tt = jnp.concatenate([t_ref[0, :, lanes], t_ref[1, :, lanes]], axis=0)
        pp = jnp.dot(m_ref[j], tt, preferred_element_type=F32)
        pc = jnp.concatenate([pp[:64], pp[64:]], axis=1).astype(BF16)
        y_ref[:, lanes] = jnp.dot(pc, cs, preferred_element_type=F32).astype(BF16)


def _fourier_tables():
    a = np.arange(64)
    ang1 = 2.0 * np.pi * np.outer(a, a) / 64.0
    w1 = np.concatenate([np.cos(ang1), -np.sin(ang1)], axis=0)
    k1 = a[:, None, None]
    k2 = a[None, :, None]
    n2 = a[None, None, :]
    theta = 2.0 * np.pi * (n2 * k2 / 64.0 + n2 * k1 / 4096.0)
    mr = np.cos(theta) / 64.0
    mi = -np.sin(theta) / 64.0
    m = np.concatenate([np.concatenate([mr, -mi], axis=2),
                        np.concatenate([mi, mr], axis=2)], axis=1)
    c = np.arange(128)
    angc = 2.0 * np.pi * np.outer(c, c) / 128.0
    eye4 = np.eye(4)
    cc = np.kron(eye4, np.cos(angc)) / np.sqrt(128.0)
    sc = np.kron(eye4, np.sin(angc)) / np.sqrt(128.0)
    cs = np.concatenate([cc, sc], axis=0)
    return tuple(jnp.asarray(t, F32).astype(BF16) for t in (w1, m, cs))


def _fourier(f):
    w1, m, cs = _fourier_tables()
    fv = f.reshape(N_ALL // 64, 64 * 512)
    n_c = 64 // F_N2_CHUNK
    t = pl.pallas_call(
        _four1_kernel,
        out_shape=jax.ShapeDtypeStruct((BATCH, 2, 64, 64, 512), BF16),
        grid=(BATCH, n_c),
        in_specs=[pl.BlockSpec((64, F_N2_CHUNK * 512), lambda b, c: (b, c)),
                  pl.BlockSpec((128, 64), lambda b, c: (0, 0))],
        out_specs=pl.BlockSpec((None, 2, F_N2_CHUNK, 64, 512), lambda b, c: (b, 0, c, 0, 0)),
        compiler_params=_cparams(("arbitrary", "arbitrary")),
        name="fourier_rows",
    )(fv, w1)
    n_k = 64 // F_K1_CHUNK
    t2 = t.reshape(BATCH, 2, 64, 64 * 512)
    y = pl.pallas_call(
        _four2_kernel,
        out_shape=jax.ShapeDtypeStruct((BATCH * 64, 64 * 512), BF16),
        grid=(BATCH, n_k),
        in_specs=[pl.BlockSpec((None, 2, 64, F_K1_CHUNK * 512), lambda b, c: (b, 0, 0, c)),
                  pl.BlockSpec((F_K1_CHUNK, 128, 128), lambda b, c: (c, 0, 0)),
                  pl.BlockSpec((1024, 512), lambda b, c: (0, 0))],
        out_specs=pl.BlockSpec((64, F_K1_CHUNK * 512), lambda b, c: (b, c)),
        compiler_params=_cparams(("arbitrary", "arbitrary")),
        name="fourier_cols",
    )(t2, m, cs)
    return y.reshape(N_MAIN, 512)


OUT_TILES = 2
TO = OUT_TILES * TM


def _route(h2, rwt_ref, rb_ref, carry):
    logits = lax.dot_general(rwt_ref[...], h2, (((1,), (1,)), ((), ())),
                             preferred_element_type=F32,
                             precision=lax.Precision.HIGHEST) + rb_ref[...]
    eidx = lax.broadcasted_iota(I32, logits.shape, 0)
    vals = logits
    sels, tops, idxs = [], [], []
    for _ in range(TOP_K):
        m = jnp.max(vals, axis=0, keepdims=True)
        idx = jnp.min(jnp.where(vals == m, eidx, N_EXPERTS), axis=0, keepdims=True)
        sel = eidx == idx
        sels.append(sel)
        tops.append(m)
        idxs.append(idx)
        vals = jnp.where(sel, -jnp.inf, vals)
    ex = [jnp.exp(t - tops[0]) for t in tops]
    den = ex[0] + ex[1] + ex[2] + ex[3]
    onehot = jnp.zeros(logits.shape, F32)
    for sel in sels:
        onehot = onehot + sel.astype(F32)
    r_i = lax.broadcasted_iota(I32, (TM, TM), 0)
    c_i = lax.broadcasted_iota(I32, (TM, TM), 1)
    upper = (r_i < c_i).astype(BF16)
    prefix = jnp.dot(onehot.astype(BF16), upper, preferred_element_type=F32)
    base = carry[:, 0:1] + prefix
    tw = [e / den for e in ex]
    rk = [jnp.sum(jnp.where(sel, base, 0.0), axis=0, keepdims=True).astype(I32) for sel in sels]
    return idxs, tw, rk, carry + jnp.sum(onehot, axis=1, keepdims=True)


def _out_tail(i, x, y, mod_ref, g_ref, rwt_ref, rb_ref, carry_ref,
              xn_ref, h2_ref, te_ref, tw_ref, rk_ref, cnt_ref):
    @pl.when(i == 0)
    def _():
        carry_ref[...] = jnp.zeros_like(carry_ref)

    carry = carry_ref[...]
    for t in range(OUT_TILES):
        rows = slice(t * TM, (t + 1) * TM)
        xn = x[rows] + mod_ref[2:3, :] * y[rows]
        xn_ref[rows, :] = xn
        h2 = _rms_mod(xn, g_ref[...], mod_ref[4:5, :], mod_ref[3:4, :])
        h2_ref[rows, :] = h2
        te, tw, rk, carry = _route(h2, rwt_ref, rb_ref, carry)
        for k in range(TOP_K):
            te_ref[k:k + 1, rows] = te[k]
            tw_ref[k:k + 1, rows] = tw[k]
            rk_ref[k:k + 1, rows] = rk[k]
    carry_ref[...] = carry
    cnt_ref[...] = carry


def _out_even_kernel(grp_ref, first_ref, last_ref,
                     x_ref, c_ref, a_ref, ac_ref, gb_ref, z_ref, zp_ref, zn_ref, cw_ref, w_ref, mod_ref, g_ref,
                     rwt_ref, rb_ref,
                     xn_ref, h2_ref, te_ref, tw_ref, rk_ref, cnt_ref, carry_ref):
    i = pl.program_id(0)
    z = z_ref[...]
    rid = lax.broadcasted_iota(I32, z.shape, 0)
    zm1 = jnp.where(rid == 0, zp_ref[7:8, :], pltpu.roll(z, 1, 0))
    zp1 = jnp.where(rid == TO - 1, zn_ref[0:1, :], pltpu.roll(z, TO - 1, 0))
    for t in range(OUT_TILES):
        zm1 = jnp.where(jnp.logical_and(rid == t * TM, first_ref[OUT_TILES * i + t] == 1), 0.0, zm1)
        zp1 = jnp.where(jnp.logical_and(rid == (t + 1) * TM - 1, last_ref[OUT_TILES * i + t] == 1), 0.0, zp1)
    conv = gb_ref[...] * (zm1 * cw_ref[0:1, :] + z * cw_ref[1:2, :] + zp1 * cw_ref[2:3, :])
    attn = jnp.where(i < NT_MAIN // OUT_TILES, a_ref[...], ac_ref[...])
    y = (jnp.dot(attn, w_ref[0:512, :], preferred_element_type=F32)
         + jnp.dot(conv.astype(BF16), w_ref[512:1024, :], preferred_element_type=F32))
    x = jnp.where(i < NT_MAIN // OUT_TILES, x_ref[...], c_ref[...])
    _out_tail(i, x, y, mod_ref, g_ref, rwt_ref, rb_ref, carry_ref,
              xn_ref, h2_ref, te_ref, tw_ref, rk_ref, cnt_ref)


def _out_odd_kernel(grp_ref, x_ref, a_ref, f_ref, w_ref, mod_ref, g_ref, rwt_ref, rb_ref,
                    xn_ref, h2_ref, te_ref, tw_ref, rk_ref, cnt_ref, carry_ref):
    i = pl.program_id(0)
    y = (jnp.dot(a_ref[...], w_ref[0:512, :], preferred_element_type=F32)
         + jnp.dot(f_ref[...], w_ref[512:1024, :], preferred_element_type=F32))
    _out_tail(i, x_ref[...], y, mod_ref, g_ref, rwt_ref, rb_ref, carry_ref,
              xn_ref, h2_ref, te_ref, tw_ref, rk_ref, cnt_ref)


def _out_shapes(n_rows):
    return (jax.ShapeDtypeStruct((n_rows, D), F32), jax.ShapeDtypeStruct((n_rows, D), F32),
            jax.ShapeDtypeStruct((TOP_K, n_rows), I32), jax.ShapeDtypeStruct((TOP_K, n_rows), F32),
            jax.ShapeDtypeStruct((TOP_K, n_rows), I32), jax.ShapeDtypeStruct((N_EXPERTS, 128), F32))


def _out_even(x2d, c2d, attn, attn_ctx, gb, z, conv_w, w_bf, mod, g, rwt, rb):
    grp, _, first, last = _tile_tables()
    n_rows = N_ALL
    zblocks = n_rows // 8
    im = lambda f: (lambda i, grp, fi, la: f(i))
    tile = lambda n: pl.BlockSpec((TO, n), im(lambda i: (i, 0)))
    const = lambda shape: pl.BlockSpec(shape, im(lambda i: (0,) * len(shape)))
    tk = pl.BlockSpec((TOP_K, TO), im(lambda i: (0, i)))
    return pl.pallas_call(
        _out_even_kernel,
        out_shape=_out_shapes(n_rows),
        grid_spec=pltpu.PrefetchScalarGridSpec(
            num_scalar_prefetch=3,
            grid=(NT_ALL // OUT_TILES,),
            in_specs=[pl.BlockSpec((TO, D), im(lambda i: (jnp.minimum(i, NT_MAIN // OUT_TILES - 1), 0))),
                      pl.BlockSpec((TO, D), im(lambda i: (jnp.maximum(i - NT_MAIN // OUT_TILES, 0), 0))),
                      pl.BlockSpec((TO, 512), im(lambda i: (jnp.minimum(i, NT_MAIN // OUT_TILES - 1), 0))),
                      pl.BlockSpec((TO, 512), im(lambda i: (jnp.maximum(i - NT_MAIN // OUT_TILES, 0), 0))),
                      tile(512), tile(512),
                      pl.BlockSpec((8, 512), im(lambda i: (jnp.maximum(i * (TO // 8) - 1, 0), 0))),
                      pl.BlockSpec((8, 512), im(lambda i: (jnp.minimum((i + 1) * (TO // 8), zblocks - 1), 0))),
                      const((3, 512)), const((D, D)),
                      pl.BlockSpec((None, 6, D), lambda i, grp, fi, la: (grp[OUT_TILES * i], 0, 0)),
                      const((1, D)), const((N_EXPERTS, D)), const((N_EXPERTS, 1))],
            out_specs=(tile(D), tile(D), tk, tk, tk, const((N_EXPERTS, 128))),
            scratch_shapes=[pltpu.VMEM((N_EXPERTS, 128), F32)],
        ),
        compiler_params=_cparams(("arbitrary",)),
        name="out_proj_even",
    )(jnp.asarray(grp), jnp.asarray(first), jnp.asarray(last),
      x2d, c2d, attn, attn_ctx, gb, z, z, z, conv_w, w_bf, mod, g, rwt, rb)


def _out_odd(xall, attn, four, w_bf, mod, g, rwt, rb):
    grp, _, _, _ = _tile_tables()
    n_rows = N_MAIN
    im = lambda f: (lambda i, grp: f(i))
    tile = lambda n: pl.BlockSpec((TO, n), im(lambda i: (i, 0)))
    const = lambda shape: pl.BlockSpec(shape, im(lambda i: (0,) * len(shape)))
    tk = pl.BlockSpec((TOP_K, TO), im(lambda i: (0, i)))
    return pl.pallas_call(
        _out_odd_kernel,
        out_shape=_out_shapes(n_rows),
        grid_spec=pltpu.PrefetchScalarGridSpec(
            num_scalar_prefetch=1,
            grid=(NT_MAIN // OUT_TILES,),
            in_specs=[tile(D), tile(512), tile(512), const((D, D)),
                      pl.BlockSpec((None, 6, D), lambda i, grp: (grp[OUT_TILES * i], 0, 0)),
                      const((1, D)), const((N_EXPERTS, D)), const((N_EXPERTS, 1))],
            out_specs=(tile(D), tile(D), tk, tk, tk, const((N_EXPERTS, 128))),
            scratch_shapes=[pltpu.VMEM((N_EXPERTS, 128), F32)],
        ),
        compiler_params=_cparams(("arbitrary",)),
        name="out_proj_odd",
    )(jnp.asarray(grp), xall, attn, four, w_bf, mod, g, rwt, rb)


def _moe_plan(counts_f, top_e_t, rank_t, n_tok):
    counts = counts_f[:, 0].astype(I32)
    padded = (counts + TMM - 1) // TMM * TMM
    e_i = jnp.arange(N_EXPERTS, dtype=I32)
    incl = e_i[None, :] <= e_i[:, None]
    pad_end = jnp.sum(jnp.where(incl, padded[None, :], 0), axis=1)
    pad_start = pad_end - padded
    sel = top_e_t[None] == e_i[:, None, None]
    dest = jnp.sum(jnp.where(sel, pad_start[:, None, None], 0), axis=0) + rank_t
    n_blocks = n_tok * TOP_K // TMM + N_EXPERTS
    blk_start = jnp.arange(n_blocks, dtype=I32) * TMM
    block_e = jnp.minimum(jnp.sum((blk_start[:, None] >= pad_end[None, :]).astype(I32), axis=1),
                          N_EXPERTS - 1)
    n_used = (pad_end[-1] // TMM).reshape(1)
    cend = pad_start + counts
    cend_b = jnp.sum(jnp.where(block_e[:, None] == e_i[None, :], cend[None, :], 0), axis=1)
    n_valid = jnp.clip(cend_b - blk_start, 0, TMM)
    n_tiles = n_tok // TT
    dest_flat = dest.reshape(TOP_K, n_tiles, TT).transpose(1, 0, 2).reshape(-1)
    return dest_flat, block_e, n_used, n_valid, pad_end, padded


def _dispatch_kernel(pe_ref, pd_ref, nu_ref, dst_ref, h2_ref, xs_hbm, zbuf, sem, sem_z):
    i = pl.program_id(0)
    n_blocks = xs_hbm.shape[0] // TMM

    @pl.when(i == 0)
    def _():
        zbuf[...] = jnp.zeros_like(zbuf)

        def fill(b):
            return pltpu.make_async_copy(zbuf, xs_hbm.at[pl.ds(pl.multiple_of(b * TMM, TMM), TMM)], sem_z)

        def fill_expert(e, n):
            has_rows = pd_ref[e] > 0

            @pl.when(has_rows)
            def _():
                fill(pe_ref[e] // TMM - 1).start()

            return n + has_rows.astype(I32)

        def fill_tail(b, c):
            fill(b).start()
            return c

        def fill_wait(j, c):
            fill(0).wait()
            return c

        n_fill = lax.fori_loop(0, N_EXPERTS, fill_expert, 0) + n_blocks - nu_ref[0]
        lax.fori_loop(nu_ref[0], n_blocks, fill_tail, 0)
        lax.fori_loop(0, n_fill, fill_wait, 0)

    base = i * (TOP_K * TT)

    def issue(j, c):
        for k in range(TOP_K):
            pltpu.make_async_copy(h2_ref.at[pl.ds(j, 1)], xs_hbm.at[pl.ds(dst_ref[base + k * TT + j], 1)],
                                  sem).start(priority=k % 2)
        return c

    lax.fori_loop(0, TT, issue, 0, unroll=4)
    for _ in range(TOP_K):
        pltpu.make_async_copy(h2_ref, xs_hbm.at[pl.ds(0, TT)], sem).wait()


def _dispatch(h2, dest_flat, pad_end, padded, n_used, cap):
    n_tiles = h2.shape[0] // TT
    return pl.pallas_call(
        _dispatch_kernel,
        out_shape=jax.ShapeDtypeStruct((cap, D), F32),
        grid_spec=pltpu.PrefetchScalarGridSpec(
            num_scalar_prefetch=4,
            grid=(n_tiles,),
            in_specs=[pl.BlockSpec((TT, D), lambda i, pe, pd, nu, dst: (i, 0))],
            out_specs=pl.BlockSpec(memory_space=pl.ANY),
            scratch_shapes=[pltpu.VMEM((TMM, D), F32), pltpu.SemaphoreType.DMA, pltpu.SemaphoreType.DMA],
        ),
        compiler_params=_cparams(("arbitrary",)),
        name="moe_dispatch",
    )(pad_end, padded, n_used, dest_flat, h2)


def _moe_kernel(be_ref, nu_ref, nv_ref, x_ref, wgu_ref, bgu_ref, wdn_ref, bdn_ref, y_ref, wgu_bf, wdn_bf):
    i = pl.program_id(0)
    prev = be_ref[jnp.maximum(i - 1, 0)]
    used = i < nu_ref[0]
    half = TMM // 2

    @pl.when(used & ((i == 0) | (be_ref[i] != prev)))
    def _():
        wgu_bf[...] = wgu_ref[...].astype(BF16)
        wdn_bf[...] = wdn_ref[...].astype(BF16)

    def ffn(x):
        gu = jnp.dot(x.astype(BF16), wgu_bf[...], preferred_element_type=F32) + bgu_ref[...]
        gate = jnp.minimum(gu[:, :D], SWIGLU_LIMIT)
        up = jnp.clip(gu[:, D:], -SWIGLU_LIMIT, SWIGLU_LIMIT)
        act = (up + 1.0) * (gate * (1.0 / (1.0 + jnp.exp(-SWIGLU_ALPHA * gate))))
        return jnp.dot(act.astype(BF16), wdn_bf[...], preferred_element_type=F32) + bdn_ref[...]

    @pl.when(used & (nv_ref[i] > half))
    def _():
        y_ref[...] = ffn(x_ref[...])

    @pl.when(used & (nv_ref[i] <= half))
    def _():
        y_ref[0:half, :] = ffn(x_ref[0:half, :])
        y_ref[half:TMM, :] = jnp.zeros((TMM - half, D), F32)

    @pl.when(i >= nu_ref[0])
    def _():
        y_ref[...] = jnp.zeros_like(y_ref)


def _moe(layer, xs, block_e, n_used, n_valid, w_gu, b_gu, w_dn, b_dn):
    n_blocks = block_e.shape[0]
    n_l = w_gu.shape[0]
    blk = lambda i, be, nu, nv: (jnp.minimum(i, nu[0] - 1), 0)
    out_blk = lambda i, be, nu, nv: (i, 0)
    exp4 = lambda i, be, nu, nv: (layer, be[jnp.minimum(i, nu[0] - 1)], 0, 0)
    return pl.pallas_call(
        _moe_kernel,
        out_shape=jax.ShapeDtypeStruct((n_blocks * TMM, D), F32),
        grid_spec=pltpu.PrefetchScalarGridSpec(
            num_scalar_prefetch=3,
            grid=(n_blocks,),
            in_specs=[pl.BlockSpec((TMM, D), blk),
                      pl.BlockSpec((None, None, D, 2 * D), exp4),
                      pl.BlockSpec((None, None, 1, 2 * D), exp4),
                      pl.BlockSpec((None, None, D, D), exp4),
                      pl.BlockSpec((None, None, 1, D), exp4)],
            out_specs=pl.BlockSpec((TMM, D), out_blk),
            scratch_shapes=[pltpu.VMEM((D, 2 * D), BF16), pltpu.VMEM((D, D), BF16)],
        ),
        compiler_params=_cparams(("arbitrary",)),
        name="moe_experts",
    )(block_e, n_used, n_valid, xs, w_gu, b_gu.reshape(n_l, N_EXPERTS, 1, 2 * D), w_dn,
      b_dn.reshape(n_l, N_EXPERTS, 1, D))


def _combine_kernel(final, grp_ref, dst_ref, ys_hbm, x_ref, tw_ref, mod_ref, fn_ref, o_ref, buf, sem):
    i = pl.program_id(0)
    n_tiles = pl.num_programs(0)

    def gather(tile, slot):
        base = tile * (TOP_K * TT)

        def issue(j, c):
            for k in range(TOP_K):
                pltpu.make_async_copy(ys_hbm.at[pl.ds(dst_ref[base + k * TT + j], 1)],
                                      buf.at[slot, k, pl.ds(j, 1)], sem.at[slot]).start(priority=k % 2)
            return c

        lax.fori_loop(0, TT, issue, 0, unroll=4)

    @pl.when(i == 0)
    def _():
        gather(0, 0)

    @pl.when(i + 1 < n_tiles)
    def _():
        gather(i + 1, (i + 1) % 2)

    slot = i % 2
    for k in range(TOP_K):
        pltpu.make_async_copy(ys_hbm.at[pl.ds(0, TT)], buf.at[slot, k], sem.at[slot]).wait()
    tw = tw_ref[...]
    acc = tw[:, 0:1] * buf[slot, 0]
    for k in range(1, TOP_K):
        acc = acc + tw[:, k:k + 1] * buf[slot, k]
    out = x_ref[...] + mod_ref[5:6, :] * acc
    if final:
        ms = jnp.mean(out * out, axis=-1, keepdims=True)
        out = out * lax.rsqrt(ms + EPS) * fn_ref[...]
    o_ref[...] = out


def _combine(xn, ys, dest_flat, top_w_t, mod, final_norm, n_tok, final):
    grp, _, _, _ = _tile_tables()
    n_tiles = n_tok // TT
    tw = top_w_t.T
    return pl.pallas_call(
        functools.partial(_combine_kernel, final),
        out_shape=jax.ShapeDtypeStruct((n_tok, D), F32),
        grid_spec=pltpu.PrefetchScalarGridSpec(
            num_scalar_prefetch=2,
            grid=(n_tiles,),
            in_specs=[pl.BlockSpec(memory_space=pl.ANY),
                      pl.BlockSpec((TT, D), lambda i, grp, dst: (i, 0)),
                      pl.BlockSpec((TT, TOP_K), lambda i, grp, dst: (i, 0)),
                      pl.BlockSpec((None, 6, D), lambda i, grp, dst: (grp[i * (TT // TM)], 0, 0)),
                      pl.BlockSpec((1, D), lambda i, grp, dst: (0, 0))],
            out_specs=pl.BlockSpec((TT, D), lambda i, grp, dst: (i, 0)),
            scratch_shapes=[pltpu.VMEM((2, TOP_K, TT, D), F32), pltpu.SemaphoreType.DMA((2,))],
        ),
        compiler_params=_cparams(("arbitrary",)),
        name="moe_combine",
    )(jnp.asarray(grp), dest_flat, ys, xn, tw, mod, final_norm.reshape(1, D))


def _moe_layer(layer, xn, h2, top_e_t, top_w_t, rank_t, counts, mod, w_gu, b_gu, w_dn, b_dn, final_norm, final):
    n_tok = xn.shape[0]
    dest_flat, block_e, n_used, n_valid, pad_end, padded = _moe_plan(counts, top_e_t, rank_t, n_tok)
    xs = _dispatch(h2, dest_flat, pad_end, padded, n_used, block_e.shape[0] * TMM)
    ys = _moe(layer, xs, block_e, n_used, n_valid, w_gu, b_gu, w_dn, b_dn)
    return _combine(xn, ys, dest_flat, top_w_t, mod, final_norm, n_tok, final)


def kernel(x, c, ctx, c_ctx, ada_w, ada_b, norm_mix, norm_ffn, even_w_in, even_w_out, even_conv_w, even_sink, odd_w_in, odd_w_out, odd_rpb, router_w, router_b, moe_w_gu, moe_b_gu, moe_w_dn, moe_b_dn, final_norm):
    x2d = x.reshape(N_MAIN, D)
    c2d = ctx.reshape(N_CTX, D)
    cc = jnp.concatenate([c, c_ctx[None, :], jnp.zeros((3, D), F32)], axis=0)
    mod = _ada(cc, ada_w, ada_b).reshape(2, 8, 6, D)
    cos_f, sin_f = _rope_tables()

    q, k, ks, v, vs, gb, z = _in_even(x2d, c2d, mod[0], norm_mix[0:1], even_w_in[0].astype(BF16), cos_f, sin_f)
    attn = _win_attn(even_sink[0], q, k, ks, v, vs)
    attn_ctx = _ctx_attn(even_sink[0], q, k, ks, v, vs)
    xn, h2, te, tw, rk, cnt = _out_even(x2d, c2d, attn, attn_ctx, gb, z, even_conv_w[0], even_w_out[0].astype(BF16),
                                        mod[0], norm_ffn[0:1], router_w[0].T, router_b[0][:, None])
    xall = _moe_layer(0, xn, h2, te, tw, rk, cnt, mod[0], moe_w_gu, moe_b_gu, moe_w_dn, moe_b_dn,
                      final_norm, False)

    q, k, v, f = _in_odd(xall, mod[1], norm_mix[1:2], odd_w_in[0].astype(BF16))
    attn = _na_attn(q, k, v, _na_bias(odd_rpb[0]))
    four = _fourier(f)
    xn, h2, te, tw, rk, cnt = _out_odd(xall, attn, four, odd_w_out[0].astype(BF16),
                                       mod[1], norm_ffn[1:2], router_w[1].T, router_b[1][:, None])
    out = _moe_layer(1, xn, h2, te, tw, rk, cnt, mod[1], moe_w_gu, moe_b_gu, moe_w_dn, moe_b_dn,
                     final_norm, True)
    return out.reshape(BATCH, SEQ, D)
```
